```python
import math
import jax, jax.numpy as jnp
from jax import lax
import numpy as np

D_MODEL = 1024
BATCH = 4
SEQ = 4096
DEPTH = 4

GRID_W = 64
CTX_LEN = 256
HEAD_DIM = 64
A_HEADS = 6
A_QK = HEAD_DIM // 2
A_W = A_HEADS * HEAD_DIM
Q_BLOCK = 128
ROPE_THETA = 10000.0
B_HEADS = 6
B_W = B_HEADS * HEAD_DIM
CONV_W = 5
GDN_CHUNK = 64
C_HEADS = 4
C_W = C_HEADS * HEAD_DIM
WIN_ROWS = 8
WIN_COLS = 16
MIX_W = A_W + B_W + C_W
PROJ_SIZES = (3 * A_W, 3 * B_W, B_W, 2 * B_HEADS, 2 * B_HEADS, 3 * C_W)
D_IN = sum(PROJ_SIZES)
D_FF = -(-8 * D_MODEL // (3 * 256)) * 256
EPS = 1e-6

kernel_name = 'hybrid_diffattn_gdn_natten_dit'


def rmsnorm(x, w):
    xf = x.astype(jnp.float32)
    y = xf * lax.rsqrt(jnp.mean(xf * xf, axis=-1, keepdims=True) + EPS)
    return (y * w.astype(jnp.float32)).astype(x.dtype)


def l2norm(x):
    return x * lax.rsqrt(jnp.sum(x * x, axis=-1, keepdims=True) + EPS)


def modulate(h, shift, scale):
    return h * (1.0 + scale) + shift


def swiglu(h, w_in, w_out):
    g, u = jnp.split(h @ w_in, 2, axis=-1)
    return (jax.nn.silu(g) * u) @ w_out


def axial_rope_tables(n_tok):
    half = A_QK // 2
    inv_freq = 1.0 / (ROPE_THETA ** (jnp.arange(0, half, 2, dtype=jnp.float32) / half))
    t = jnp.arange(n_tok, dtype=jnp.int32)
    ang_r = (t // GRID_W).astype(jnp.float32)[:, None] * inv_freq
    ang_c = (t % GRID_W).astype(jnp.float32)[:, None] * inv_freq
    ang = jnp.concatenate([ang_r, ang_r, ang_c, ang_c], axis=-1)
    return jnp.cos(ang), jnp.sin(ang)


def apply_axial_rope(x, cos, sin):
    r1, r2, c1, c2 = jnp.split(x, 4, axis=-1)
    rot = jnp.concatenate([-r2, r1, -c2, c1], axis=-1)
    cos = cos[:, None, None, :].astype(x.dtype)
    sin = sin[:, None, None, :].astype(x.dtype)
    return x * cos + rot * sin


def diff_attention(qkv_l, qkv_c, lam, lam_init, norm_w, cos, sin, with_ctx):
    nb, t, _ = qkv_l.shape
    qk_heads = lambda z: z.reshape(z.shape[0], z.shape[1], A_HEADS, 2, A_QK)
    v_heads = lambda z: z.reshape(z.shape[0], z.shape[1], A_HEADS, HEAD_DIM)
    q_l, k_l, v_l = jnp.split(qkv_l, 3, axis=-1)
    q_c, k_c, v_c = jnp.split(qkv_c, 3, axis=-1)
    q_l = apply_axial_rope(qk_heads(q_l), cos, sin)
    k_l = apply_axial_rope(qk_heads(k_l), cos, sin)
    k_c, v_c = qk_heads(k_c), v_heads(v_c)
    keys = jnp.concatenate([k_c, k_l], axis=1)
    vals = jnp.concatenate([v_c, v_heads(v_l)], axis=1)
    scale = A_QK ** -0.5

    def attend(q, k, v):
        s = jnp.einsum('bqhmd,bkhmd->bhmqk', q, k).astype(jnp.float32) * scale
        p = jax.nn.softmax(s, axis=-1)
        w = (p[:, :, 0] - lam * p[:, :, 1]).astype(v.dtype)
        return jnp.einsum('bhqk,bkhd->bqhd', w, v)

    def post(o):
        o = rmsnorm(o, norm_w) * (1.0 - lam_init)
        return o.reshape(o.shape[0], o.shape[1], A_W)

    n_blk = t // Q_BLOCK
    q_blocks = q_l.reshape(nb, n_blk, Q_BLOCK, A_HEADS, 2, A_QK).transpose(1, 0, 2, 3, 4, 5)
    o_l = lax.map(lambda qb: attend(qb, keys, vals), q_blocks)
    o_l = o_l.transpose(1, 0, 2, 3, 4).reshape(nb, t, A_HEADS, HEAD_DIM)
    y_c = post(attend(qk_heads(q_c), k_c, v_c)) if with_ctx else None
    return post(o_l), y_c


def short_conv(x, w):
    y = lax.conv_general_dilated(
        x, w[:, None, :].astype(x.dtype), window_strides=(1,),
        padding=[(CONV_W // 2, CONV_W // 2)], dimension_numbers=('NWC', 'WIO', 'NWC'),
        feature_group_count=x.shape[-1])
    return jax.nn.silu(y)


def chunk_gated_delta(q, k, v, g, beta, s0):
    nb, nh, t, dk = q.shape
    dv = v.shape[-1]
    c = GDN_CHUNK
    n = t // c
    ch = lambda a: a.reshape(nb, nh, n, c, *a.shape[3:])
    q, k, v, g, beta = ch(q), ch(k), ch(v), ch(g), ch(beta)
    gc = jnp.cumsum(g, axis=-1)
    idx = jnp.arange(c)
    incl = idx[:, None] >= idx[None, :]
    strict = idx[:, None] > idx[None, :]
    decay = jnp.exp(jnp.where(incl, gc[..., :, None] - gc[..., None, :], -jnp.inf))
    kb = k * beta[..., None]
    lmat = jnp.where(strict, jnp.einsum('bhnid,bhnjd->bhnij', kb, k) * decay, 0.0)
    eye = jnp.eye(c, dtype=q.dtype)
    t_inv = lax.linalg.triangular_solve(eye + lmat, jnp.broadcast_to(eye, lmat.shape),
                                        left_side=True, lower=True, unit_diagonal=True)
    u = t_inv @ (v * beta[..., None])
    w = t_inv @ (kb * jnp.exp(gc)[..., None])
    a_qk = jnp.einsum('bhnid,bhnjd->bhnij', q, k) * decay
    g_last = gc[..., -1]
    k_tail = k * jnp.exp(g_last[..., None] - gc)[..., None]
    q_head = q * jnp.exp(gc)[..., None]
    xs = tuple(jnp.moveaxis(a, 2, 0) for a in (q_head, k_tail, u, w, a_qk, g_last))

    def step(s, inp):
        qh, kt, ui, wi, ai, gl = inp
        v_new = ui - wi @ s
        o = qh @ s + ai @ v_new
        s = s * jnp.exp(gl)[..., None, None] + jnp.swapaxes(kt, -1, -2) @ v_new
        return s, o

    s, o = lax.scan(step, s0, xs)
    return jnp.moveaxis(o, 0, 2).reshape(nb, nh, t, dv), s


def gdn_prepare(qkv, alpha, beta, conv_w, a_log, dt_bias):
    nb, t = qkv.shape[:2]
    q, k, v = jnp.split(short_conv(qkv, conv_w), 3, axis=-1)
    heads = lambda z: z.reshape(nb, t, B_HEADS, HEAD_DIM).transpose(0, 2, 1, 3).astype(jnp.float32)
    q = l2norm(heads(q)) * (HEAD_DIM ** -0.5)
    k = l2norm(heads(k))
    v = heads(v)
    alpha = alpha.reshape(nb, t, 2, B_HEADS).astype(jnp.float32)
    beta = beta.reshape(nb, t, 2, B_HEADS).astype(jnp.float32)
    g = -jnp.exp(a_log.astype(jnp.float32)) * jax.nn.softplus(alpha + dt_bias.astype(jnp.float32))
    b = jax.nn.sigmoid(beta)
    return q, k, v, g.transpose(2, 0, 3, 1), b.transpose(2, 0, 3, 1)


def bidir_scan(q, k, v, g, b, s_fwd, s_bwd):
    flip = lambda z: jnp.flip(z, axis=2)
    o_f, sf = chunk_gated_delta(q, k, v, g[0], b[0], s_fwd)
    o_b, sb = chunk_gated_delta(flip(q), flip(k), flip(v), flip(g[1]), flip(b[1]), s_bwd)
    return o_f + flip(o_b), sf, sb


def gated_deltanet(qkv_l, gate_l, alpha_l, beta_l, qkv_c, gate_c, alpha_c, beta_c,
                   conv_w, a_log, dt_bias, norm_w, with_ctx):
    lat = gdn_prepare(qkv_l, alpha_l, beta_l, conv_w, a_log, dt_bias)
    cx = gdn_prepare(qkv_c, alpha_c, beta_c, conv_w, a_log, dt_bias)
    s0 = jnp.zeros((qkv_c.shape[0], B_HEADS, HEAD_DIM, HEAD_DIM), jnp.float32)
    o_c, s_f, s_b = bidir_scan(*cx, s0, s0)
    o_l, _, _ = bidir_scan(*lat, s_f, s_b)

    def out(o, gate):
        nb, _, t, _ = o.shape
        gt = jax.nn.silu(gate.reshape(nb, t, B_HEADS, HEAD_DIM).astype(jnp.float32))
        o = rmsnorm(o.transpose(0, 2, 1, 3), norm_w) * gt
        return o.reshape(nb, t, B_W).astype(gate.dtype)

    return out(o_l, gate_l), (out(o_c, gate_c) if with_ctx else None)


def neighborhood_attention(qkv_l, qkv_c, rel_bias, with_ctx):
    nb, t, _ = qkv_l.shape
    rows = t // GRID_W
    wr = min(WIN_ROWS, rows)
    nk = wr * WIN_COLS
    heads = lambda z: z.reshape(z.shape[0], z.shape[1], C_HEADS, HEAD_DIM).transpose(0, 2, 1, 3)
    q_l, k_l, v_l = (heads(z) for z in jnp.split(qkv_l, 3, axis=-1))
    q_c, k_c, v_c = (heads(z) for z in jnp.split(qkv_c, 3, axis=-1))
    scale = HEAD_DIM ** -0.5
    cq = jnp.arange(GRID_W)
    kcols = jnp.clip(cq - WIN_COLS // 2, 0, GRID_W - WIN_COLS)[:, None] + jnp.arange(WIN_COLS)
    dcol = kcols - cq[:, None] + (WIN_COLS - 1)
    q_rows = q_l.reshape(nb, C_HEADS, rows, GRID_W, HEAD_DIM)

    def row_block(r):
        krows = jnp.clip(r - wr // 2, 0, rows - wr) + jnp.arange(wr)
        drow = krows - r + (WIN_ROWS - 1)
        idx = (krows[None, :, None] * GRID_W + kcols[:, None, :]).reshape(GRID_W, nk)
        kn = jnp.take(k_l, idx, axis=2)
        vn = jnp.take(v_l, idx, axis=2)
        bias = rel_bias[:, drow[None, :, None], dcol[:, None, :]].reshape(C_HEADS, GRID_W, nk)
        qr = lax.dynamic_index_in_dim(q_rows, r, axis=2, keepdims=False)
        s_n = jnp.einsum('bhqd,bhqkd->bhqk', qr, kn).astype(jnp.float32) * scale + bias.astype(jnp.float32)
        s_c = jnp.einsum('bhqd,bhkd->bhqk', qr, k_c).astype(jnp.float32) * scale
        p = jax.nn.softmax(jnp.concatenate([s_n, s_c], axis=-1), axis=-1).astype(v_l.dtype)
        return (jnp.einsum('bhqk,bhqkd->bhqd', p[..., :nk], vn)
                + jnp.einsum('bhqk,bhkd->bhqd', p[..., nk:], v_c))

    o = lax.map(row_block, jnp.arange(rows))
    y_l = o.transpose(1, 0, 3, 2, 4).reshape(nb, t, C_W)
    y_c = None
    if with_ctx:
        s = jnp.einsum('bhqd,bhkd->bhqk', q_c, k_c).astype(jnp.float32) * scale
        p = jax.nn.softmax(s, axis=-1).astype(v_c.dtype)
        oc = jnp.einsum('bhqk,bhkd->bhqd', p, v_c)
        y_c = oc.transpose(0, 2, 1, 3).reshape(oc.shape[0], oc.shape[2], C_W)
    return y_l, y_c


def setup_inputs(seed: int = 0) -> dict:
    key = jax.random.key(seed)
    ks = jax.random.split(key, 24)
    nrm = lambda k, shape: jax.random.normal(k, shape, jnp.float32)
    D = D_MODEL
    x = nrm(ks[0], (BATCH, SEQ, D))
    c = nrm(ks[1], (BATCH, D))
    ctx = nrm(ks[2], (BATCH, CTX_LEN, D))
    c_ctx = nrm(ks[3], (D,))
    w_mod = nrm(ks[4], (DEPTH, D, 6 * D)) * (0.5 * D ** -0.5)
    b_mod = 0.02 * nrm(ks[5], (DEPTH, 6 * D))
    norm1_w = 1.0 + 0.05 * nrm(ks[6], (DEPTH, D))
    norm2_w = 1.0 + 0.05 * nrm(ks[7], (DEPTH, D))
    w_in = nrm(ks[8], (DEPTH, D, D_IN)) * D ** -0.5
    w_out = nrm(ks[9], (DEPTH, MIX_W, D)) * MIX_W ** -0.5
    lambda_q1 = 0.1 * nrm(ks[10], (DEPTH, A_QK))
    lambda_k1 = 0.1 * nrm(ks[11], (DEPTH, A_QK))
    lambda_q2 = 0.1 * nrm(ks[12], (DEPTH, A_QK))
    lambda_k2 = 0.1 * nrm(ks[13], (DEPTH, A_QK))
    diff_norm_w = 1.0 + 0.05 * nrm(ks[14], (DEPTH, HEAD_DIM))
    conv_w = nrm(ks[15], (DEPTH, CONV_W, 3 * B_W)) * CONV_W ** -0.5
    a_log = jnp.log(jax.random.uniform(ks[16], (DEPTH, 2, B_HEADS), jnp.float32, 1.0, 16.0))
    dt = jnp.exp(jax.random.uniform(ks[17], (DEPTH, 2, B_HEADS), jnp.float32,
                                    math.log(1e-3), math.log(1e-1)))
    dt_bias = dt + jnp.log(-jnp.expm1(-dt))
    gdn_norm_w = 1.0 + 0.05 * nrm(ks[18], (DEPTH, HEAD_DIM))
    na_bias = 0.1 * nrm(ks[19], (DEPTH, C_HEADS, 2 * WIN_ROWS - 1, 2 * WIN_COLS - 1))
    w_ffn_in = nrm(ks[20], (DEPTH, D, 2 * D_FF)) * D ** -0.5
    w_ffn_out = nrm(ks[21], (DEPTH, D_FF, D)) * D_FF ** -0.5
    final_norm_w = 1.0 + 0.05 * nrm(ks[22], (D,))
    return {'x': x, 'c': c, 'ctx': ctx, 'c_ctx': c_ctx, 'w_mod': w_mod, 'b_mod': b_mod,
            'norm1_w': norm1_w, 'norm2_w': norm2_w, 'w_in': w_in, 'w_out': w_out,
            'lambda_q1': lambda_q1, 'lambda_k1': lambda_k1, 'lambda_q2': lambda_q2,
            'lambda_k2': lambda_k2, 'diff_norm_w': diff_norm_w, 'conv_w': conv_w,
            'a_log': a_log, 'dt_bias': dt_bias, 'gdn_norm_w': gdn_norm_w, 'na_bias': na_bias,
            'w_ffn_in': w_ffn_in, 'w_ffn_out': w_ffn_out, 'final_norm_w': final_norm_w}


def reference(x, c, ctx, c_ctx, w_mod, b_mod, norm1_w, norm2_w, w_in, w_out,
              lambda_q1, lambda_k1, lambda_q2, lambda_k2, diff_norm_w,
              conv_w, a_log, dt_bias, gdn_norm_w, na_bias,
              w_ffn_in, w_ffn_out, final_norm_w):
    cos, sin = axial_rope_tables(x.shape[1])
    split_at = [int(s) for s in np.cumsum(PROJ_SIZES)[:-1]]
    silu_c = jax.nn.silu(c)
    silu_cc = jax.nn.silu(c_ctx)
    xl, xc = x, ctx
    for l in range(DEPTH):
        with_ctx = l < DEPTH - 1
        mod_l = jnp.split((silu_c @ w_mod[l] + b_mod[l])[:, None, :], 6, axis=-1)
        mod_c = jnp.split(silu_cc @ w_mod[l] + b_mod[l], 6, axis=-1)
        hl = modulate(rmsnorm(xl, norm1_w[l]), mod_l[0], mod_l[1])
        hc = modulate(rmsnorm(xc, norm1_w[l]), mod_c[0], mod_c[1])
        qkv_a_l, qkv_b_l, gate_b_l, alpha_l, beta_l, qkv_c_l = jnp.split(hl @ w_in[l], split_at, axis=-1)
        qkv_a_c, qkv_b_c, gate_b_c, alpha_c, beta_c, qkv_c_c = jnp.split(hc @ w_in[l], split_at, axis=-1)

        lam_init = 0.8 - 0.6 * math.exp(-0.3 * l)
        lam = (jnp.exp(jnp.sum(lambda_q1[l].astype(jnp.float32) * lambda_k1[l].astype(jnp.float32)))
               - jnp.exp(jnp.sum(lambda_q2[l].astype(jnp.float32) * lambda_k2[l].astype(jnp.float32)))
               + lam_init)
        ya_l, ya_c = diff_attention(qkv_a_l, qkv_a_c, lam, lam_init, diff_norm_w[l], cos, sin, with_ctx)
        yb_l, yb_c = gated_deltanet(qkv_b_l, gate_b_l, alpha_l, beta_l, qkv_b_c, gate_b_c, alpha_c, beta_c,
                                    conv_w[l], a_log[l], dt_bias[l], gdn_norm_w[l], with_ctx)
        yc_l, yc_c = neighborhood_attention(qkv_c_l, qkv_c_c, na_bias[l], with_ctx)

        xl = xl + mod_l[2] * (jnp.concatenate([ya_l, yb_l, yc_l], axis=-1) @ w_out[l])
        hl = modulate(rmsnorm(xl, norm2_w[l]), mod_l[3], mod_l[4])
        xl = xl + mod_l[5] * swiglu(hl, w_ffn_in[l], w_ffn_out[l])
        if with_ctx:
            xc = xc + mod_c[2] * (jnp.concatenate([ya_c, yb_c, yc_c], axis=-1) @ w_out[l])
            hc = modulate(rmsnorm(xc, norm2_w[l]), mod_c[3], mod_c[4])
            xc = xc + mod_c[5] * swiglu(hc, w_ffn_in[l], w_ffn_out[l])
    return rmsnorm(xl, final_norm_w)
```

```python
import functools
import math

import jax
import jax.numpy as jnp
import numpy as np
from jax import lax
from jax.experimental import pallas as pl
from jax.experimental.pallas import tpu as pltpu

F32 = jnp.float32
BF16 = jnp.bfloat16
D_MOD = D_IN = D_OUT = D_FFN = D_ATT = D_NA = D_GDN = BF16
D_GA = D_GI = D_GT = D_GS = BF16

HEAD_DIM = 64
A_QK = HEAD_DIM // 2
GRID_W = 64
ROPE_THETA = 10000.0
CONV_W = 5
GDN_CHUNK = 64
WIN_ROWS = 8
WIN_COLS = 16
EPS = 1e-6

LANES = 128
SUBLANES = 8
TILE = 256
NA_WIN_TILES = 3
NEG = -1e30
VMEM_LIMIT = 56 * 1024 * 1024


def _cparams(sem):
    return pltpu.CompilerParams(dimension_semantics=sem, vmem_limit_bytes=VMEM_LIMIT)


def _silu(x):
    return x * (1.0 / (1.0 + jnp.exp(-x)))


def _dot(a, b):
    return jnp.dot(a, b, preferred_element_type=F32)


def _dot_nt(a, b):
    return lax.dot_general(a, b, (((1,), (1,)), ((), ())), preferred_element_type=F32)


def _split3(x):
    hi = x.astype(BF16)
    r1 = x - hi.astype(F32)
    mid = r1.astype(BF16)
    lo = (r1 - mid.astype(F32)).astype(BF16)
    return hi, mid, lo


def _dot_sel_right(x, sel):
    hi, mid, lo = _split3(x)
    return _dot(hi, sel) + _dot(mid, sel) + _dot(lo, sel)


def _dot_sel_left(sel, x):
    hi, mid, lo = _split3(x)
    return _dot(sel, hi) + _dot(sel, mid) + _dot(sel, lo)


def _mod_kernel(c_ref, w_ref, b_ref, o_ref):
    s = _silu(c_ref[...])
    o_ref[0] = _dot(s.astype(D_MOD), w_ref[0].astype(D_MOD)) + b_ref[0]


def _modulation(cc, w_mod, b_mod):
    depth, d, n = w_mod.shape
    rows = cc.shape[0]
    tn = 1536
    assert n % tn == 0
    return pl.pallas_call(
        _mod_kernel,
        grid=(depth, n // tn),
        in_specs=[
            pl.BlockSpec((rows, d), lambda l, j: (0, 0)),
            pl.BlockSpec((1, d, tn), lambda l, j: (l, 0, j)),
            pl.BlockSpec((1, 1, tn), lambda l, j: (l, 0, j)),
        ],
        out_specs=pl.BlockSpec((1, rows, tn), lambda l, j: (l, 0, j)),
        out_shape=jax.ShapeDtypeStruct((depth, rows, n), F32),
        compiler_params=_cparams(("parallel", "parallel")),
        name="modulation",
    )(cc, w_mod, b_mod.reshape(depth, 1, n))


def _norm_mod(x, nw, shift, scale):
    ms = jnp.mean(x * x, axis=-1, keepdims=True)
    y = x * lax.rsqrt(ms + EPS) * nw
    return y * (1.0 + scale) + shift


def _rope_lanes(x, cos, sin):
    parts = []
    for j in range(x.shape[1] // LANES):
        xj = x[:, LANES * j:LANES * (j + 1)]
        nxt = pltpu.roll(xj, LANES - 8, axis=1)
        prv = pltpu.roll(xj, 8, axis=1)
        lane = lax.broadcasted_iota(jnp.int32, xj.shape, 1)
        parts.append(jnp.where((lane % 16) < 8, -nxt, prv))
    rot = jnp.concatenate(parts, axis=1)
    return x * cos + rot * sin


def _rope_rows(x, cos, sin):
    n = x.shape[0]
    nxt = pltpu.roll(x, n - 8, axis=0)
    prv = pltpu.roll(x, 8, axis=0)
    row = lax.broadcasted_iota(jnp.int32, x.shape, 0)
    rot = jnp.where((row % 16) < 8, -nxt, prv)
    return x * cos + rot * sin


def _inproj_kernel(x_ref, mod_ref, nw_ref, wm_ref, wk_ref, cos_ref, sin_ref, cost_ref, sint_ref,
                   qa_ref, kat_ref, va_ref, qkvb_ref, gate_ref, ab_ref, qc_ref, kct_ref, vc_ref,
                   *, a_w, b_w, c_w):
    mod = mod_ref[0]
    h = _norm_mod(x_ref[0], nw_ref[...], mod[0:1], mod[1:2]).astype(D_IN)
    acc = _dot(h, wm_ref[...])
    kt = _dot_nt(wk_ref[...], h)
    o = 0
    qa = _rope_lanes(acc[:, o:o + a_w], cos_ref[...], sin_ref[...])
    qa_ref[0] = (qa * (A_QK ** -0.5)).astype(D_ATT)
    o += a_w
    va_ref[0] = acc[:, o:o + a_w].astype(D_ATT)
    o += a_w
    qkvb_ref[0] = acc[:, o:o + 3 * b_w]
    o += 3 * b_w
    gate_ref[0] = acc[:, o:o + b_w]
    o += b_w
    qc_ref[0] = (acc[:, o:o + c_w] * (HEAD_DIM ** -0.5)).astype(D_NA)
    o += c_w
    vc_ref[0] = acc[:, o:o + c_w].astype(D_NA)
    o += c_w
    ab_ref[0] = acc[:, o:o + LANES]
    kat_ref[0, 0] = _rope_rows(kt[:a_w], cost_ref[...], sint_ref[...]).astype(D_ATT)
    kct_ref[0, 0] = kt[a_w:].astype(D_NA)


def _inproj(xs, mods, nw, wm, wk, cos, sin, cost, sint, *, n_batch, n_ctx_tiles, a_w, b_w, c_w):
    nb, tt, d = xs.shape
    nt = tt // TILE
    nm = wm.shape[1]
    ctx_row = n_batch

    def mod_idx(b, i):
        return (jnp.where(i < n_ctx_tiles, ctx_row, b), 0, 0)

    tok = lambda w: pl.BlockSpec((1, TILE, w), lambda b, i: (b, i, 0))
    tokt = lambda w: pl.BlockSpec((1, 1, w, TILE), lambda b, i: (b, i, 0, 0))
    out_shape = (
        jax.ShapeDtypeStruct((nb, tt, a_w), D_ATT),
        jax.ShapeDtypeStruct((nb, nt, a_w, TILE), D_ATT),
        jax.ShapeDtypeStruct((nb, tt, a_w), D_ATT),
        jax.ShapeDtypeStruct((nb, tt, 3 * b_w), F32),
        jax.ShapeDtypeStruct((nb, tt, b_w), F32),
        jax.ShapeDtypeStruct((nb, tt, LANES), F32),
        jax.ShapeDtypeStruct((nb, tt, c_w), D_NA),
        jax.ShapeDtypeStruct((nb, nt, c_w, TILE), D_NA),
        jax.ShapeDtypeStruct((nb, tt, c_w), D_NA),
    )
    return pl.pallas_call(
        functools.partial(_inproj_kernel, a_w=a_w, b_w=b_w, c_w=c_w),
        grid=(nb, nt),
        in_specs=[
            tok(d),
            pl.BlockSpec((1, 6, d), mod_idx),
            pl.BlockSpec((1, d), lambda b, i: (0, 0)),
            pl.BlockSpec((d, nm), lambda b, i: (0, 0)),
            pl.BlockSpec((a_w + c_w, d), lambda b, i: (0, 0)),
            pl.BlockSpec((TILE, a_w), lambda b, i: (i, 0)),
            pl.BlockSpec((TILE, a_w), lambda b, i: (i, 0)),
            pl.BlockSpec((a_w, TILE), lambda b, i: (0, i)),
            pl.BlockSpec((a_w, TILE), lambda b, i: (0, i)),
        ],
        out_specs=(tok(a_w), tokt(a_w), tok(a_w), tok(3 * b_w), tok(b_w), tok(LANES),
                   tok(c_w), tokt(c_w), tok(c_w)),
        out_shape=out_shape,
        compiler_params=_cparams(("parallel", "parallel")),
        name="inproj",
    )(xs, mods, nw, wm, wk, cos, sin, cost, sint)


def _diffattn_kernel(lam_ref, nw_ref, q_ref, kt_ref, v_ref, o_ref, *, n_ctx_tiles, n_key_tiles, lam_init):
    i = pl.program_id(2)
    lv = lam_ref[...]
    lam = (jnp.exp(jnp.sum(lv[0:1] * lv[1:2], axis=-1, keepdims=True))
           - jnp.exp(jnp.sum(lv[2:3] * lv[3:4], axis=-1, keepdims=True)) + lam_init)
    q = q_ref[0]
    tq = q.shape[0]
    lane = lax.broadcasted_iota(jnp.int32, (tq, LANES), 1)
    zero = jnp.zeros_like(q)
    qm = [jnp.where((lane // A_QK) == m, q, zero) for m in range(LANES // A_QK)]
    n_maps = len(qm)
    nk = jnp.where(i < n_ctx_tiles, n_ctx_tiles, n_key_tiles)

    def body(j, carry):
        ms, ls, accs = carry
        ktile = kt_ref[0, j]
        vtile = v_ref[0, pl.ds(pl.multiple_of(j * TILE, TILE), TILE), :]
        nms, nls, naccs = [], [], []
        for m in range(n_maps):
            s = _dot(qm[m], ktile)
            m_new = jnp.maximum(ms[m], jnp.max(s, axis=-1, keepdims=True))
            alpha = jnp.exp(ms[m] - m_new)
            p = jnp.exp(s - m_new)
            nls.append(alpha * ls[m] + jnp.sum(p, axis=-1, keepdims=True))
            naccs.append(alpha * accs[m] + _dot(p.astype(D_ATT), vtile))
            nms.append(m_new)
        return tuple(nms), tuple(nls), tuple(naccs)

    init = (tuple(jnp.full((tq, 1), NEG, F32) for _ in range(n_maps)),
            tuple(jnp.zeros((tq, 1), F32) for _ in range(n_maps)),
            tuple(jnp.zeros((tq, LANES), F32) for _ in range(n_maps)))
    _, ls, accs = lax.fori_loop(0, nk, body, init)
    o_lo = accs[0] / ls[0] - lam * (accs[1] / ls[1])
    o_hi = accs[2] / ls[2] - lam * (accs[3] / ls[3])
    lo = lane < HEAD_DIM
    o = jnp.where(lo, o_lo, o_hi)
    sq = o * o
    ms_lo = jnp.sum(jnp.where(lo, sq, 0.0), axis=-1, keepdims=True) * (1.0 / HEAD_DIM)
    ms_hi = jnp.sum(jnp.where(lo, 0.0, sq), axis=-1, keepdims=True) * (1.0 / HEAD_DIM)
    y = o * lax.rsqrt(jnp.where(lo, ms_lo, ms_hi) + EPS) * nw_ref[...]
    o_ref[0] = (y * (1.0 - lam_init)).astype(D_OUT)


def _diffattn(lam_vecs, nw2, qa, kat, va, *, n_ctx_tiles, lam_init):
    nb, tt, a_w = qa.shape
    nt = tt // TILE
    return pl.pallas_call(
        functools.partial(_diffattn_kernel, n_ctx_tiles=n_ctx_tiles, n_key_tiles=nt, lam_init=lam_init),
        grid=(nb, a_w // LANES, nt),
        in_specs=[
            pl.BlockSpec(lam_vecs.shape, lambda b, p, i: (0, 0)),
            pl.BlockSpec((1, LANES), lambda b, p, i: (0, 0)),
            pl.BlockSpec((1, TILE, LANES), lambda b, p, i: (b, i, p)),
            pl.BlockSpec((1, nt, LANES, TILE), lambda b, p, i: (b, 0, p, 0)),
            pl.BlockSpec((1, tt, LANES), lambda b, p, i: (b, 0, p)),
        ],
        out_specs=pl.BlockSpec((1, TILE, LANES), lambda b, p, i: (b, i, p)),
        out_shape=jax.ShapeDtypeStruct((nb, tt, a_w), D_OUT),
        compiler_params=_cparams(("parallel", "parallel", "parallel")),
        name="diffattn",
    )(lam_vecs, nw2, qa, kat, va)


def _na_kernel(q_ref, kt_ref, v_ref, bias_ref, o_ref, *, n_ctx_tiles, n_key_tiles):
    i = pl.program_id(2)
    n_lat = n_key_tiles - n_ctx_tiles
    w0 = n_ctx_tiles + jnp.clip(i - n_ctx_tiles - 1, 0, n_lat - NA_WIN_TILES)
    q = q_ref[0]
    tq = q.shape[0]
    lane = lax.broadcasted_iota(jnp.int32, (tq, LANES), 1)
    lo = lane < HEAD_DIM
    zero = jnp.zeros_like(q)
    outs = []
    for hd in range(2):
        qh = jnp.where(lo if hd == 0 else jnp.logical_not(lo), q, zero)
        tiles = [w0 + c for c in range(NA_WIN_TILES)] + list(range(n_ctx_tiles))
        scores = []
        for c, t in enumerate(tiles):
            s = _dot(qh, kt_ref[0, t])
            if c < NA_WIN_TILES:
                s = s + bias_ref[hd, 0, :, TILE * c:TILE * (c + 1)]
            scores.append(s)
        m = scores[0].max(axis=-1, keepdims=True)
        for s in scores[1:]:
            m = jnp.maximum(m, s.max(axis=-1, keepdims=True))
        l = jnp.zeros((tq, 1), F32)
        acc = jnp.zeros((tq, LANES), F32)
        for s, t in zip(scores, tiles):
            p = jnp.exp(s - m)
            l = l + jnp.sum(p, axis=-1, keepdims=True)
            vt = v_ref[0, pl.ds(pl.multiple_of(t * TILE, TILE), TILE), :]
            acc = acc + _dot(p.astype(D_NA), vt)
        outs.append(acc / l)
    o_ref[0] = jnp.where(lo, outs[0], outs[1]).astype(D_OUT)


def _na(qc, kct, vc, bias, *, n_ctx_tiles):
    nb, tt, c_w = qc.shape
    nt = tt // TILE
    n_lat = nt - n_ctx_tiles

    def bias_idx(b, p, i):
        il = i - n_ctx_tiles
        cls = jnp.where(il < 0, 3, jnp.where(il == 0, 0, jnp.where(il == n_lat - 1, 2, 1)))
        return (p, cls, 0, 0)

    return pl.pallas_call(
        functools.partial(_na_kernel, n_ctx_tiles=n_ctx_tiles, n_key_tiles=nt),
        grid=(nb, c_w // LANES, nt),
        in_specs=[
            pl.BlockSpec((1, TILE, LANES), lambda b, p, i: (b, i, p)),
            pl.BlockSpec((1, nt, LANES, TILE), lambda b, p, i: (b, 0, p, 0)),
            pl.BlockSpec((1, tt, LANES), lambda b, p, i: (b, 0, p)),
            pl.BlockSpec((2, 1, TILE, NA_WIN_TILES * TILE), bias_idx),
        ],
        out_specs=pl.BlockSpec((1, TILE, LANES), lambda b, p, i: (b, i, p)),
        out_shape=jax.ShapeDtypeStruct((nb, tt, c_w), D_OUT),
        compiler_params=_cparams(("parallel", "parallel", "parallel")),
        name="nbr_attn",
    )(qc, kct, vc, bias)


def _na_bias_table(rel_bias, seq):
    rows = seq // GRID_W
    rpt = TILE // GRID_W
    wrows = NA_WIN_TILES * rpt
    assert rows >= wrows and rows >= WIN_ROWS
    qi = np.arange(TILE)
    ki = np.arange(NA_WIN_TILES * TILE)
    qr, qcol = qi // GRID_W, qi % GRID_W
    kr, kcol = ki // GRID_W, ki % GRID_W
    kc0 = np.clip(qcol - WIN_COLS // 2, 0, GRID_W - WIN_COLS)
    col_ok = (kcol[None, :] >= kc0[:, None]) & (kcol[None, :] < kc0[:, None] + WIN_COLS)
    dcol = np.clip(kcol[None, :] - qcol[:, None] + (WIN_COLS - 1), 0, 2 * WIN_COLS - 2)
    tables = []
    n_lat = rows // rpt
    for il in (0, min(1, n_lat - 1), n_lat - 1):
        r_abs = il * rpt + qr
        ws = int(np.clip(il - 1, 0, n_lat - NA_WIN_TILES)) * rpt
        k_abs = ws + kr
        kr0 = np.clip(r_abs - WIN_ROWS // 2, 0, rows - WIN_ROWS)
        row_ok = (k_abs[None, :] >= kr0[:, None]) & (k_abs[None, :] < kr0[:, None] + WIN_ROWS)
        drow = np.clip(k_abs[None, :] - r_abs[:, None] + (WIN_ROWS - 1), 0, 2 * WIN_ROWS - 2)
        vals = rel_bias[:, drow, dcol].astype(F32)
        tables.append(jnp.where(jnp.asarray(row_ok & col_ok)[None], vals, NEG))
    tables.append(jnp.full_like(tables[0], NEG))
    return jnp.stack(tables, axis=1)


def _gdn_prep_kernel(x_ref, prev_ref, next_ref, ab_ref, cw_ref, alog_ref, dtb_ref,
                     q_ref, k_ref, v_ref, sc_ref, sct_ref, *, n_ctx_tiles, n_tiles, n_heads):
    i = pl.program_id(1)
    first = jnp.logical_or(i == 0, i == n_ctx_tiles)
    last = jnp.logical_or(i == n_ctx_tiles - 1, i == n_tiles - 1)
    prev = jnp.where(first, 0.0, prev_ref[0, 0])
    nxt = jnp.where(last, 0.0, next_ref[0, 0])
    ext = jnp.concatenate([prev, x_ref[0], nxt], axis=0)
    cw = cw_ref[...]
    y = None
    for j in range(CONV_W):
        o = SUBLANES - CONV_W // 2 + j
        term = ext[o:o + TILE] * cw[j:j + 1]
        y = term if y is None else y + term
    y = _silu(y)
    b_w = n_heads * HEAD_DIM
    for h in range(n_heads):
        qh = y[:, h * HEAD_DIM:(h + 1) * HEAD_DIM]
        kh = y[:, b_w + h * HEAD_DIM:b_w + (h + 1) * HEAD_DIM]
        vh = y[:, 2 * b_w + h * HEAD_DIM:2 * b_w + (h + 1) * HEAD_DIM]
        q_ref[0, h] = qh * lax.rsqrt(jnp.sum(qh * qh, axis=-1, keepdims=True) + EPS) * (HEAD_DIM ** -0.5)
        k_ref[0, h] = kh * lax.rsqrt(jnp.sum(kh * kh, axis=-1, keepdims=True) + EPS)
        v_ref[0, h] = vh

    ab = ab_ref[0]
    lane = lax.broadcasted_iota(jnp.int32, ab.shape, 1)
    z = ab + dtb_ref[...]
    softplus = jnp.maximum(z, 0.0) + jnp.log(1.0 + jnp.exp(-jnp.abs(z)))
    g = jnp.where(lane < 2 * n_heads, -jnp.exp(alog_ref[...]) * softplus, 0.0)
    beta = 1.0 / (1.0 + jnp.exp(-ab))
    r = lax.broadcasted_iota(jnp.int32, (TILE, TILE), 0)
    c = lax.broadcasted_iota(jnp.int32, (TILE, TILE), 1)
    same = (r // GDN_CHUNK) == (c // GDN_CHUNK)
    tri_f = jnp.where(jnp.logical_and(same, c <= r), 1.0, 0.0).astype(BF16)
    tri_b = jnp.where(jnp.logical_and(same, c >= r), 1.0, 0.0).astype(BF16)
    gc = jnp.where(lane < n_heads, _dot_sel_left(tri_f, g), _dot_sel_left(tri_b, g))
    rr = lax.broadcasted_iota(jnp.int32, (LANES, LANES), 0)
    cc = lax.broadcasted_iota(jnp.int32, (LANES, LANES), 1)
    hh, jj = cc // SUBLANES, cc % SUBLANES
    valid = hh < n_heads
    sel_gb = jnp.where(jnp.logical_and(valid, jnp.logical_and(jj < 4, rr == jj * n_heads + hh)), 1.0, 0.0).astype(BF16)
    sel_gc = jnp.where(jnp.logical_and(valid, jnp.logical_and(jnp.logical_and(jj >= 4, jj < 6),
                                                              rr == (jj - 4) * n_heads + hh)), 1.0, 0.0).astype(BF16)
    gb = jnp.where(lane < 2 * n_heads, g, beta)
    packed = _dot_sel_right(gb, sel_gb) + _dot_sel_right(gc, sel_gc)
    packed_t = packed.T
    for h in range(n_heads):
        sc_ref[0, h] = packed[:, SUBLANES * h:SUBLANES * (h + 1)]
        sct_ref[0, h] = packed_t[SUBLANES * h:SUBLANES * (h + 1), :]


def _gdn_prep(qkvb, ab, conv_w, alog_v, dtb_v, *, n_ctx_tiles, n_heads):
    nb, tt, w3 = qkvb.shape
    nt = tt // TILE
    rows8 = TILE // SUBLANES
    x8 = qkvb.reshape(nb, tt // SUBLANES, SUBLANES, w3)
    hd = lambda w: pl.BlockSpec((1, n_heads, TILE, w), lambda b, i: (b, 0, i, 0))
    return pl.pallas_call(
        functools.partial(_gdn_prep_kernel, n_ctx_tiles=n_ctx_tiles, n_tiles=nt, n_heads=n_heads),
        grid=(nb, nt),
        in_specs=[
            pl.BlockSpec((1, TILE, w3), lambda b, i: (b, i, 0)),
            pl.BlockSpec((1, 1, SUBLANES, w3), lambda b, i: (b, jnp.maximum(i * rows8 - 1, 0), 0, 0)),
            pl.BlockSpec((1, 1, SUBLANES, w3), lambda b, i: (b, jnp.minimum((i + 1) * rows8, nt * rows8 - 1), 0, 0)),
            pl.BlockSpec((1, TILE, LANES), lambda b, i: (b, i, 0)),
            pl.BlockSpec((CONV_W, w3), lambda b, i: (0, 0)),
            pl.BlockSpec((1, LANES), lambda b, i: (0, 0)),
            pl.BlockSpec((1, LANES), lambda b, i: (0, 0)),
        ],
        out_specs=(hd(HEAD_DIM), hd(HEAD_DIM), hd(HEAD_DIM), hd(SUBLANES),
                   pl.BlockSpec((1, n_heads, SUBLANES, TILE), lambda b, i: (b, 0, 0, i))),
        out_shape=(
            jax.ShapeDtypeStruct((nb, n_heads, tt, HEAD_DIM), F32),
            jax.ShapeDtypeStruct((nb, n_heads, tt, HEAD_DIM), F32),
            jax.ShapeDtypeStruct((nb, n_heads, tt, HEAD_DIM), F32),
            jax.ShapeDtypeStruct((nb, n_heads, tt, SUBLANES), F32),
            jax.ShapeDtypeStruct((nb, n_heads, SUBLANES, tt), F32),
        ),
        compiler_params=_cparams(("parallel", "parallel")),
        name="gdn_prep",
    )(qkvb, x8, x8, ab, conv_w, alog_v, dtb_v)


def _bmm(a, b):
    return jnp.einsum("nik,nkj->nij", a, b, preferred_element_type=F32)


def _bmm_nt(a, b):
    return jnp.einsum("nik,njk->nij", a, b, preferred_element_type=F32)


def _tri_inverse(lmat, ii, jj):
    cast = lambda z: z.astype(D_GI)
    blk = 16
    same = (ii // blk) == (jj // blk)
    dmat = jnp.where(same, lmat, 0.0)
    eye = jnp.where(ii == jj, 1.0, 0.0)
    db = cast(dmat)
    x = eye - dmat
    p = _bmm(db, db)
    n_sq = int(math.log2(blk)) - 1
    for step in range(n_sq):
        pb, xb = cast(p), cast(x)
        x = x + _bmm(pb, xb)
        if step < n_sq - 1:
            p = _bmm(pb, pb)
    while blk < lmat.shape[-1]:
        wider = (ii // (2 * blk)) == (jj // (2 * blk))
        off = jnp.where(jnp.logical_and(wider, jnp.logical_not(same)), lmat, 0.0)
        xb = cast(x)
        x = x - _bmm(xb, cast(_bmm(cast(off), xb)))
        same = wider
        blk *= 2
    return x


def _gdn_local_kernel(q_ref, k_ref, v_ref, sc_ref, sct_ref,
                      u_ref, w_ref, qh_ref, aqk_ref, ktt_ref, egl_ref):
    c = GDN_CHUNK
    cpt = TILE // c
    n = 2 * cpt

    def both(x):
        x3 = x.reshape(cpt, c, x.shape[-1])
        return jnp.concatenate([x3, x3], axis=0)

    q, k, v = both(q_ref[0, 0]), both(k_ref[0, 0]), both(v_ref[0, 0])
    sc = sc_ref[0, 0].reshape(cpt, c, SUBLANES)
    beta = jnp.concatenate([sc[:, :, 2:3], sc[:, :, 3:4]], axis=0)
    gcc = jnp.concatenate([sc[:, :, 4:5], sc[:, :, 5:6]], axis=0)
    sct = sct_ref[0, 0]
    gcr = jnp.stack([sct[4 + d:5 + d, ch * c:(ch + 1) * c] for d in range(2) for ch in range(cpt)], axis=0)

    ii = lax.broadcasted_iota(jnp.int32, (n, c, c), 1)
    jj = lax.broadcasted_iota(jnp.int32, (n, c, c), 2)
    fwd = lax.broadcasted_iota(jnp.int32, (n, c, c), 0) < cpt
    incl = jnp.logical_or(jnp.logical_and(fwd, ii >= jj), jnp.logical_and(jnp.logical_not(fwd), ii <= jj))
    strict = jnp.logical_and(incl, ii != jj)
    decay = jnp.where(incl, jnp.exp(jnp.where(incl, gcc - gcr, 0.0)), 0.0)
    kb = k * beta
    a = _bmm_nt(jnp.concatenate([kb, q], axis=1).astype(D_GA), k.astype(D_GA))
    lmat = jnp.where(strict, a[:, :c] * decay, 0.0)
    aqk = a[:, c:] * decay
    tinv = _tri_inverse(lmat, ii, jj).astype(D_GT)
    egc = jnp.exp(gcc)
    u = _bmm(tinv, (v * beta).astype(D_GT))
    w = _bmm(tinv, (kb * egc).astype(D_GT))
    fwd1 = lax.broadcasted_iota(jnp.int32, (n, 1, 1), 0) < cpt
    glast = jnp.where(fwd1, gcc[:, c - 1:c, :], gcc[:, 0:1, :])
    ktail = (k * jnp.exp(glast - gcc)).astype(D_GDN)
    eye = jnp.where(ii == jj, 1.0, 0.0).astype(D_GDN)
    split = lambda z: z.reshape((2, cpt) + z.shape[1:])
    u_ref[0, 0] = split(u)
    w_ref[0, 0] = split(w.astype(D_GDN))
    qh_ref[0, 0] = split((q * egc).astype(D_GDN))
    aqk_ref[0, 0] = split(aqk.astype(D_GDN))
    ktt_ref[0, 0] = split(_bmm_nt(eye, ktail).astype(D_GDN))
    egl_ref[0, 0] = split(jnp.broadcast_to(jnp.exp(glast), (n, 1, HEAD_DIM)))


def _gdn_local(qn, kn, vv, sc, sct):
    nb, nh, tt, hd = qn.shape
    nt = tt // TILE
    nc = tt // GDN_CHUNK
    cpt = TILE // GDN_CHUNK
    tok = lambda w: pl.BlockSpec((1, 1, TILE, w), lambda b, h, i: (b, h, i, 0))
    chunked = lambda r, w: pl.BlockSpec((1, 1, 2, cpt, r, w), lambda b, h, i: (b, h, 0, i, 0, 0))
    shape = lambda r, w, dt: jax.ShapeDtypeStruct((nb, nh, 2, nc, r, w), dt)
    return pl.pallas_call(
        _gdn_local_kernel,
        grid=(nb, nh, nt),
        in_specs=[tok(hd), tok(hd), tok(hd), tok(SUBLANES),
                  pl.BlockSpec((1, 1, SUBLANES, TILE), lambda b, h, i: (b, h, 0, i))],
        out_specs=(chunked(GDN_CHUNK, hd), chunked(GDN_CHUNK, hd), chunked(GDN_CHUNK, hd),
                   chunked(GDN_CHUNK, GDN_CHUNK), chunked(hd, GDN_CHUNK), chunked(1, hd)),
        out_shape=(
            shape(GDN_CHUNK, hd, F32),
            shape(GDN_CHUNK, hd, D_GDN),
            shape(GDN_CHUNK, hd, D_GDN),
            shape(GDN_CHUNK, GDN_CHUNK, D_GDN),
            shape(hd, GDN_CHUNK, D_GDN),
            shape(1, hd, F32),
        ),
        compiler_params=_cparams(("parallel", "parallel", "parallel")),
        name="gdn_local",
    )(qn, kn, vv, sc, sct)


def _gdn_scan_kernel(uf, ub, wf, wb, qf, qb, af, ab, kf, kb, ef, eb, of_ref, ob_ref, s_ref, *, cpt, n_heads):
    @pl.when(pl.program_id(1) == 0)
    def _():
        s_ref[...] = jnp.zeros_like(s_ref)

    s = s_ref[...]
    for t in range(cpt):
        tb = cpt - 1 - t
        pair = lambda f, b: jnp.concatenate([f[0, :, 0, t], b[0, :, 0, tb]], axis=0)
        sb = s.astype(D_GS)
        v_new = pair(uf, ub) - _bmm(pair(wf, wb), sb)
        vb = v_new.astype(D_GS)
        o = _bmm(pair(qf, qb), sb) + _bmm(pair(af, ab), vb)
        s = s * pair(ef, eb) + _bmm(pair(kf, kb), vb)
        of_ref[0, :, t] = o[:n_heads]
        ob_ref[0, :, tb] = o[n_heads:]
    s_ref[...] = s


def _gdn_scan(u, w, qh, aqk, ktt, egl, *, n_ctx_tiles):
    nb, nh, _, nc, c, hd = u.shape
    cpt = TILE // c
    nt = nc // cpt

    def bwd_tile(j):
        return jnp.where(j < n_ctx_tiles, n_ctx_tiles - 1 - j, nt - 1 + n_ctx_tiles - j)

    def specs(a):
        blk = (1, nh, 1, cpt) + a.shape[4:]
        return [pl.BlockSpec(blk, lambda b, j: (b, 0, 0, j, 0, 0)),
                pl.BlockSpec(blk, lambda b, j: (b, 0, 1, bwd_tile(j), 0, 0))]

    args, in_specs = [], []
    for a in (u, w, qh, aqk, ktt, egl):
        args += [a, a]
        in_specs += specs(a)
    oshape = jax.ShapeDtypeStruct((nb, nh, nc, c, hd), F32)
    return pl.pallas_call(
        functools.partial(_gdn_scan_kernel, cpt=cpt, n_heads=nh),
        grid=(nb, nt),
        in_specs=in_specs,
        out_specs=(pl.BlockSpec((1, nh, cpt, c, hd), lambda b, j: (b, 0, j, 0, 0)),
                   pl.BlockSpec((1, nh, cpt, c, hd), lambda b, j: (b, 0, bwd_tile(j), 0, 0))),
        out_shape=(oshape, oshape),
        scratch_shapes=[pltpu.VMEM((2 * nh, hd, hd), F32)],
        compiler_params=_cparams(("parallel", "arbitrary")),
        name="gdn_scan",
    )(*args)


def _outproj_kernel(x_ref, mod_ref, ya_ref, of_ref, ob_ref, gate_ref, gnw_ref, yc_ref, w_ref, o_ref,
                    *, a_w, b_w, n_heads):
    acc = _dot(ya_ref[0], w_ref[0:a_w, :]) + _dot(yc_ref[0], w_ref[a_w + b_w:, :])
    gate = gate_ref[0]
    for h in range(n_heads):
        o = of_ref[0, h] + ob_ref[0, h]
        ms = jnp.mean(o * o, axis=-1, keepdims=True)
        y = o * lax.rsqrt(ms + EPS) * gnw_ref[...]
        yb = (y * _silu(gate[:, h * HEAD_DIM:(h + 1) * HEAD_DIM])).astype(D_OUT)
        acc = acc + _dot(yb, w_ref[a_w + h * HEAD_DIM:a_w + (h + 1) * HEAD_DIM, :])
    o_ref[0] = x_ref[0] + mod_ref[0][2:3] * acc


def _outproj(xs, mods, ya, scan_f, scan_b, gate, gnw, yc, w, *, n_batch, n_ctx_tiles, n_heads):
    nb, tt, d = xs.shape
    nt = tt // TILE
    a_w, b_w, c_w = ya.shape[2], gate.shape[2], yc.shape[2]

    def mod_idx(b, i):
        return (jnp.where(i < n_ctx_tiles, n_batch, b), 0, 0)

    tok = lambda wd: pl.BlockSpec((1, TILE, wd), lambda b, i: (b, i, 0))
    heads = pl.BlockSpec((1, n_heads, TILE, HEAD_DIM), lambda b, i: (b, 0, i, 0))
    return pl.pallas_call(
        functools.partial(_outproj_kernel, a_w=a_w, b_w=b_w, n_heads=n_heads),
        grid=(nb, nt),
        in_specs=[
            tok(d),
            pl.BlockSpec((1, 6, d), mod_idx),
            tok(a_w),
            heads,
            heads,
            tok(b_w),
            pl.BlockSpec((1, HEAD_DIM), lambda b, i: (0, 0)),
            tok(c_w),
            pl.BlockSpec(w.shape, lambda b, i: (0, 0)),
        ],
        out_specs=tok(d),
        out_shape=jax.ShapeDtypeStruct((nb, tt, d), F32),
        compiler_params=_cparams(("parallel", "parallel")),
        name="outproj",
    )(xs, mods, ya, scan_f, scan_b, gate, gnw, yc, w)


def _ffn_kernel(x_ref, mod_ref, nw_ref, wi_ref, wo_ref, o_ref, *, d_ff):
    mod = mod_ref[0]
    x = x_ref[0]
    h = _norm_mod(x, nw_ref[...], mod[3:4], mod[4:5]).astype(D_FFN)
    gu = _dot(h, wi_ref[...])
    a = (_silu(gu[:, :d_ff]) * gu[:, d_ff:]).astype(D_FFN)
    o_ref[0] = x + mod[5:6] * _dot(a, wo_ref[...])


def _ffn(xs, mods, nw, wi, wo, *, n_batch, n_ctx_tiles):
    nb, tt, d = xs.shape
    nt = tt // TILE
    d_ff = wo.shape[0]

    def mod_idx(b, i):
        return (jnp.where(i < n_ctx_tiles, n_batch, b), 0, 0)

    tok = pl.BlockSpec((1, TILE, d), lambda b, i: (b, i, 0))
    return pl.pallas_call(
        functools.partial(_ffn_kernel, d_ff=d_ff),
        grid=(nb, nt),
        in_specs=[
            tok,
            pl.BlockSpec((1, 6, d), mod_idx),
            pl.BlockSpec((1, d), lambda b, i: (0, 0)),
            pl.BlockSpec(wi.shape, lambda b, i: (0, 0), pipeline_mode=pl.Buffered(1)),
            pl.BlockSpec(wo.shape, lambda b, i: (0, 0), pipeline_mode=pl.Buffered(1)),
        ],
        out_specs=tok,
        out_shape=jax.ShapeDtypeStruct((nb, tt, d), F32),
        compiler_params=_cparams(("parallel", "parallel")),
        name="ffn",
    )(xs, mods, nw, wi, wo)


def _final_kernel(x_ref, nw_ref, o_ref):
    x = x_ref[0]
    ms = jnp.mean(x * x, axis=-1, keepdims=True)
    o_ref[0] = x * lax.rsqrt(ms + EPS) * nw_ref[...]


def _final_norm(xs, nw, *, n_ctx_tiles):
    nb, tt, d = xs.shape
    n_lat = tt // TILE - n_ctx_tiles
    return pl.pallas_call(
        _final_kernel,
        grid=(nb, n_lat),
        in_specs=[pl.BlockSpec((1, TILE, d), lambda b, i: (b, i + n_ctx_tiles, 0)),
                  pl.BlockSpec((1, d), lambda b, i: (0, 0))],
        out_specs=pl.BlockSpec((1, TILE, d), lambda b, i: (b, i, 0)),
        out_shape=jax.ShapeDtypeStruct((nb, n_lat * TILE, d), F32),
        compiler_params=_cparams(("parallel", "parallel")),
        name="final_norm",
    )(xs, nw)


def _rope_tables(seq, ctx_len, n_rep):
    half = A_QK // 2
    inv_freq = 1.0 / (ROPE_THETA ** (jnp.arange(0, half, 2, dtype=F32) / half))
    t = jnp.arange(seq, dtype=jnp.int32)
    ang_r = (t // GRID_W).astype(F32)[:, None] * inv_freq
    ang_c = (t % GRID_W).astype(F32)[:, None] * inv_freq
    ang = jnp.concatenate([ang_r, ang_r, ang_c, ang_c], axis=-1)
    cos = jnp.concatenate([jnp.ones((ctx_len, A_QK), F32), jnp.cos(ang)], axis=0)
    sin = jnp.concatenate([jnp.zeros((ctx_len, A_QK), F32), jnp.sin(ang)], axis=0)
    return jnp.tile(cos, (1, n_rep)), jnp.tile(sin, (1, n_rep))


def kernel(x, c, ctx, c_ctx, w_mod, b_mod, norm1_w, norm2_w, w_in, w_out, lambda_q1, lambda_k1, lambda_q2,
           lambda_k2, diff_norm_w, conv_w, a_log, dt_bias, gdn_norm_w, na_bias, w_ffn_in, w_ffn_out,
           final_norm_w):
    nb, seq, d = x.shape
    ctx_len = ctx.shape[1]
    depth = w_mod.shape[0]
    b_heads = a_log.shape[-1]
    c_heads = na_bias.shape[1]
    b_w = b_heads * HEAD_DIM
    c_w = c_heads * HEAD_DIM
    a_w = w_out.shape[1] - b_w - c_w
    assert seq % TILE == 0 and ctx_len % TILE == 0 and seq % GRID_W == 0
    assert a_w % LANES == 0 and b_w % LANES == 0 and c_w % LANES == 0 and 4 * b_heads <= LANES
    n_ctx_tiles = ctx_len // TILE

    xs = jnp.concatenate([ctx, x], axis=1)

    n_rows = -(-(nb + 1) // SUBLANES) * SUBLANES
    cc = jnp.concatenate([c, c_ctx[None, :], jnp.zeros((n_rows - nb - 1, d), F32)], axis=0)
    mods_all = _modulation(cc, w_mod, b_mod).reshape(depth, n_rows, 6, d)

    cos, sin = _rope_tables(seq, ctx_len, a_w // A_QK)
    cost, sint = cos.T, sin.T

    sizes = (3 * a_w, 3 * b_w, b_w, 2 * b_heads, 2 * b_heads, 3 * c_w)
    offs = np.concatenate([[0], np.cumsum(sizes)])
    o_a, o_b, o_g, o_al, o_be, o_c = (int(v) for v in offs[:6])
    pad = LANES - 4 * b_heads

    for l in range(depth):
        wl = w_in[l]
        wm = jnp.concatenate([
            wl[:, o_a:o_a + a_w],
            wl[:, o_a + 2 * a_w:o_a + 3 * a_w],
            wl[:, o_b:o_b + 3 * b_w],
            wl[:, o_g:o_g + b_w],
            wl[:, o_c:o_c + c_w],
            wl[:, o_c + 2 * c_w:o_c + 3 * c_w],
            wl[:, o_al:o_al + 4 * b_heads],
            jnp.zeros((d, pad), F32),
        ], axis=1).astype(D_IN)
        wk = jnp.concatenate([wl[:, o_a + a_w:o_a + 2 * a_w], wl[:, o_c + c_w:o_c + 2 * c_w]], axis=1).T.astype(D_IN)
        mods = mods_all[l]
        lam_init = 0.8 - 0.6 * math.exp(-0.3 * l)

        qa, kat, va, qkvb, gate, ab, qc, kct, vc = _inproj(
            xs, mods, norm1_w[l][None, :], wm, wk, cos, sin, cost, sint,
            n_batch=nb, n_ctx_tiles=n_ctx_tiles, a_w=a_w, b_w=b_w, c_w=c_w)

        lam_vecs = jnp.stack([lambda_q1[l], lambda_k1[l], lambda_q2[l], lambda_k2[l]], axis=0).astype(F32)
        nw2 = jnp.tile(diff_norm_w[l], LANES // HEAD_DIM)[None, :]
        ya = _diffattn(lam_vecs, nw2, qa, kat, va, n_ctx_tiles=n_ctx_tiles, lam_init=lam_init)

        yc = _na(qc, kct, vc, _na_bias_table(na_bias[l], seq), n_ctx_tiles=n_ctx_tiles)

        alog_v = jnp.concatenate([a_log[l].reshape(-1), jnp.zeros((LANES - 2 * b_heads,), F32)])[None, :]
        dtb_v = jnp.concatenate([dt_bias[l].reshape(-1), jnp.zeros((LANES - 2 * b_heads,), F32)])[None, :]
        qn, kn, vv, sc, sct = _gdn_prep(qkvb, ab, conv_w[l], alog_v, dtb_v,
                                        n_ctx_tiles=n_ctx_tiles, n_heads=b_heads)
        u, w, qh, aqk, ktt, egl = _gdn_local(qn, kn, vv, sc, sct)
        scan_f, scan_b = _gdn_scan(u, w, qh, aqk, ktt, egl, n_ctx_tiles=n_ctx_tiles)
        scan_f = scan_f.reshape(nb, b_heads, ctx_len + seq, HEAD_DIM)
        scan_b = scan_b.reshape(nb, b_heads, ctx_len + seq, HEAD_DIM)

        xs = _outproj(xs, mods, ya, scan_f, scan_b, gate, gdn_norm_w[l][None, :], yc, w_out[l].astype(D_OUT),
                      n_batch=nb, n_ctx_tiles=n_ctx_tiles, n_heads=b_heads)
        xs = _ffn(xs, mods, norm2_w[l][None, :], w_ffn_in[l].astype(D_FFN), w_ffn_out[l].astype(D_FFN),
                  n_batch=nb, n_ctx_tiles=n_ctx_tiles)

    return _final_norm(xs, final_norm_w[None, :], n_ctx_tiles=n_ctx_tiles)
```

```python
import functools
import math

import jax
import jax.numpy as jnp
import numpy as np
from jax import lax
from jax.experimental import pallas as pl
from jax.experimental.pallas import tpu as pltpu

F32 = jnp.float32
BF16 = jnp.bfloat16
D_MOD = D_IN = D_OUT = D_FFN = D_ATT = D_NA = D_GDN = BF16
D_GA = D_GI = D_GT = D_GS = BF16

HEAD_DIM = 64
A_QK = HEAD_DIM // 2
GRID_W = 64
ROPE_THETA = 10000.0
CONV_W = 5
GDN_CHUNK = 64
WIN_ROWS = 8
WIN_COLS = 16
EPS = 1e-6

LANES = 128
SUBLANES = 8
TILE = 256
NA_WIN_TILES = 3
GDN_HEADS_PER_STEP = 6
NEG = -1e30
VMEM_LIMIT = 56 * 1024 * 1024


def _cparams(sem):
    return pltpu.CompilerParams(dimension_semantics=sem, vmem_limit_bytes=VMEM_LIMIT)


def _silu(x):
    return x * (1.0 / (1.0 + jnp.exp(-x)))


def _dot(a, b):
    return jnp.dot(a, b, preferred_element_type=F32)


def _dot_nt(a, b):
    return lax.dot_general(a, b, (((1,), (1,)), ((), ())), preferred_element_type=F32)


def _split3(x):
    hi = x.astype(BF16)
    r1 = x - hi.astype(F32)
    mid = r1.astype(BF16)
    lo = (r1 - mid.astype(F32)).astype(BF16)
    return hi, mid, lo


def _dot_sel_right(x, sel):
    hi, mid, lo = _split3(x)
    return _dot(hi, sel) + _dot(mid, sel) + _dot(lo, sel)


def _dot_sel_left(sel, x):
    hi, mid, lo = _split3(x)
    return _dot(sel, hi) + _dot(sel, mid) + _dot(sel, lo)


def _mod_kernel(c_ref, w_ref, b_ref, o_ref):
    s = _silu(c_ref[...])
    o_ref[0] = _dot(s.astype(D_MOD), w_ref[0].astype(D_MOD)) + b_ref[0]


def _modulation(cc, w_mod, b_mod):
    depth, d, n = w_mod.shape
    rows = cc.shape[0]
    tn = 1536
    assert n % tn == 0
    return pl.pallas_call(
        _mod_kernel,
        grid=(depth, n // tn),
        in_specs=[
            pl.BlockSpec((rows, d), lambda l, j: (0, 0)),
            pl.BlockSpec((1, d, tn), lambda l, j: (l, 0, j)),
            pl.BlockSpec((1, 1, tn), lambda l, j: (l, 0, j)),
        ],
        out_specs=pl.BlockSpec((1, rows, tn), lambda l, j: (l, 0, j)),
        out_shape=jax.ShapeDtypeStruct((depth, rows, n), F32),
        compiler_params=_cparams(("parallel", "parallel")),
        name="modulation",
    )(cc, w_mod, b_mod.reshape(depth, 1, n))


def _norm_mod(x, nw, shift, scale):
    ms = jnp.mean(x * x, axis=-1, keepdims=True)
    y = x * lax.rsqrt(ms + EPS) * nw
    return y * (1.0 + scale) + shift


def _rope_lanes(x, cos, sin):
    parts = []
    for j in range(x.shape[1] // LANES):
        xj = x[:, LANES * j:LANES * (j + 1)]
        nxt = pltpu.roll(xj, LANES - 8, axis=1)
        prv = pltpu.roll(xj, 8, axis=1)
        lane = lax.broadcasted_iota(jnp.int32, xj.shape, 1)
        parts.append(jnp.where((lane % 16) < 8, -nxt, prv))
    rot = jnp.concatenate(parts, axis=1)
    return x * cos + rot * sin


def _rope_rows(x, cos, sin):
    n = x.shape[0]
    nxt = pltpu.roll(x, n - 8, axis=0)
    prv = pltpu.roll(x, 8, axis=0)
    row = lax.broadcasted_iota(jnp.int32, x.shape, 0)
    rot = jnp.where((row % 16) < 8, -nxt, prv)
    return x * cos + rot * sin


def _inproj_kernel(x_ref, mod_ref, nw_ref, wm_ref, wk_ref, cos_ref, sin_ref, cost_ref, sint_ref,
                   qa_ref, kat_ref, va_ref, qkvb_ref, gate_ref, ab_ref, qc_ref, kct_ref, vc_ref,
                   *, a_w, b_w, c_w):
    mod = mod_ref[0]
    h = _norm_mod(x_ref[0], nw_ref[...], mod[0:1], mod[1:2]).astype(D_IN)
    acc = _dot(h, wm_ref[...])
    kt = _dot_nt(wk_ref[...], h)
    o = 0
    qa = _rope_lanes(acc[:, o:o + a_w], cos_ref[...], sin_ref[...])
    qa_ref[0] = (qa * (A_QK ** -0.5 * math.log2(math.e))).astype(D_ATT)
    o += a_w
    va_ref[0] = acc[:, o:o + a_w].astype(D_ATT)
    o += a_w
    qkvb_ref[0] = acc[:, o:o + 3 * b_w]
    o += 3 * b_w
    gate_ref[0] = acc[:, o:o + b_w]
    o += b_w
    qc_ref[0] = (acc[:, o:o + c_w] * (HEAD_DIM ** -0.5)).astype(D_NA)
    o += c_w
    vc_ref[0] = acc[:, o:o + c_w].astype(D_NA)
    o += c_w
    ab_ref[0] = acc[:, o:o + LANES]
    kat_ref[0, 0] = _rope_rows(kt[:a_w], cost_ref[...], sint_ref[...]).astype(D_ATT)
    kct_ref[0, 0] = kt[a_w:].astype(D_NA)


def _inproj(xs, mods, nw, wm, wk, cos, sin, cost, sint, *, n_batch, n_ctx_tiles, a_w, b_w, c_w):
    nb, tt, d = xs.shape
    nt = tt // TILE
    nm = wm.shape[1]
    ctx_row = n_batch

    def mod_idx(b, i):
        return (jnp.where(i < n_ctx_tiles, ctx_row, b), 0, 0)

    tok = lambda w: pl.BlockSpec((1, TILE, w), lambda b, i: (b, i, 0))
    tokt = lambda w: pl.BlockSpec((1, 1, w, TILE), lambda b, i: (b, i, 0, 0))
    out_shape = (
        jax.ShapeDtypeStruct((nb, tt, a_w), D_ATT),
        jax.ShapeDtypeStruct((nb, nt, a_w, TILE), D_ATT),
        jax.ShapeDtypeStruct((nb, tt, a_w), D_ATT),
        jax.ShapeDtypeStruct((nb, tt, 3 * b_w), F32),
        jax.ShapeDtypeStruct((nb, tt, b_w), F32),
        jax.ShapeDtypeStruct((nb, tt, LANES), F32),
        jax.ShapeDtypeStruct((nb, tt, c_w), D_NA),
        jax.ShapeDtypeStruct((nb, nt, c_w, TILE), D_NA),
        jax.ShapeDtypeStruct((nb, tt, c_w), D_NA),
    )
    return pl.pallas_call(
        functools.partial(_inproj_kernel, a_w=a_w, b_w=b_w, c_w=c_w),
        grid=(nb, nt),
        in_specs=[
            tok(d),
            pl.BlockSpec((1, 6, d), mod_idx),
            pl.BlockSpec((1, d), lambda b, i: (0, 0)),
            pl.BlockSpec((d, nm), lambda b, i: (0, 0)),
            pl.BlockSpec((a_w + c_w, d), lambda b, i: (0, 0)),
            pl.BlockSpec((TILE, a_w), lambda b, i: (i, 0)),
            pl.BlockSpec((TILE, a_w), lambda b, i: (i, 0)),
            pl.BlockSpec((a_w, TILE), lambda b, i: (0, i)),
            pl.BlockSpec((a_w, TILE), lambda b, i: (0, i)),
        ],
        out_specs=(tok(a_w), tokt(a_w), tok(a_w), tok(3 * b_w), tok(b_w), tok(LANES),
                   tok(c_w), tokt(c_w), tok(c_w)),
        out_shape=out_shape,
        compiler_params=_cparams(("parallel", "parallel")),
        name="inproj",
    )(xs, mods, nw, wm, wk, cos, sin, cost, sint)


def _diffattn_kernel(lam_ref, nw_ref, q_ref, kt_ref, v_ref, o_ref, mx_ref, acc_ref,
                     *, n_ctx_tiles, n_key_tiles, lam_init):
    i = pl.program_id(2)
    lv = lam_ref[...]
    lam = (jnp.exp(jnp.sum(lv[0:1] * lv[1:2], axis=-1, keepdims=True))
           - jnp.exp(jnp.sum(lv[2:3] * lv[3:4], axis=-1, keepdims=True)) + lam_init)
    q = q_ref[0]
    tq = q.shape[0]
    lane = lax.broadcasted_iota(jnp.int32, (tq, LANES), 1)
    lo = lane < HEAD_DIM
    zero = jnp.zeros_like(q)
    qm = [jnp.where((lane // A_QK) == m, q, zero) for m in range(LANES // A_QK)]
    n_maps = len(qm)
    n_lat = n_key_tiles - n_ctx_tiles
    group = max(u for u in (4, 2, 1) if n_lat % u == 0)
    n_groups = jnp.where(i < n_ctx_tiles, 0, n_lat // group)
    ctx_tiles = list(range(n_ctx_tiles))
    lat_tiles = lambda g: [n_ctx_tiles + g * group + u for u in range(group)]

    mx_ref[...] = jnp.full(mx_ref.shape, NEG, F32)

    def max_tiles(tiles):
        ktiles = [kt_ref[0, j] for j in tiles]
        for m in range(n_maps):
            cur = mx_ref[m]
            for ktile in ktiles:
                s = _dot(qm[m], ktile)
                cur = jnp.maximum(cur, jnp.maximum(s[:, :LANES], s[:, LANES:]))
            mx_ref[m] = cur

    max_tiles(ctx_tiles)
    lax.fori_loop(0, n_groups, lambda g, c: (max_tiles(lat_tiles(g)), c)[1], 0)
    row_max = [jnp.max(mx_ref[m], axis=-1, keepdims=True) for m in range(n_maps)]

    acc_ref[...] = jnp.zeros(acc_ref.shape, F32)
    lane_v = lax.broadcasted_iota(jnp.int32, (TILE, LANES), 1)

    def pv_tiles(tiles):
        ktiles = [kt_ref[0, j] for j in tiles]
        v_ext = []
        for j in tiles:
            start = j * TILE if isinstance(j, int) else pl.multiple_of(j * TILE, TILE)
            vtile = v_ref[0, pl.ds(start, TILE), :]
            ones = jnp.ones_like(vtile)
            v_ext.append((jnp.where(lane_v < HEAD_DIM, vtile, ones), jnp.where(lane_v < HEAD_DIM, ones, vtile)))
        for m in range(n_maps):
            acc = acc_ref[m]
            for ktile, ve in zip(ktiles, v_ext):
                p = jnp.exp2(_dot(qm[m], ktile) - row_max[m])
                acc = acc + _dot(p.astype(D_ATT), ve[m // 2])
            acc_ref[m] = acc

    pv_tiles(ctx_tiles)
    lax.fori_loop(0, n_groups, lambda g, c: (pv_tiles(lat_tiles(g)), c)[1], 0)
    attn = []
    for m in range(n_maps):
        a = acc_ref[m]
        attn.append(a / pltpu.roll(a, HEAD_DIM, axis=1))
    o = jnp.where(lo, attn[0] - lam * attn[1], attn[2] - lam * attn[3])
    sq = o * o
    ms_lo = jnp.sum(jnp.where(lo, sq, 0.0), axis=-1, keepdims=True) * (1.0 / HEAD_DIM)
    ms_hi = jnp.sum(jnp.where(lo, 0.0, sq), axis=-1, keepdims=True) * (1.0 / HEAD_DIM)
    y = o * lax.rsqrt(jnp.where(lo, ms_lo, ms_hi) + EPS) * nw_ref[...]
    o_ref[0] = (y * (1.0 - lam_init)).astype(D_OUT)


def _diffattn(lam_vecs, nw2, qa, kat, va, *, n_ctx_tiles, lam_init):
    nb, tt, a_w = qa.shape
    nt = tt // TILE
    return pl.pallas_call(
        functools.partial(_diffattn_kernel, n_ctx_tiles=n_ctx_tiles, n_key_tiles=nt, lam_init=lam_init),
        grid=(nb, a_w // LANES, nt),
        in_specs=[
            pl.BlockSpec(lam_vecs.shape, lambda b, p, i: (0, 0)),
            pl.BlockSpec((1, LANES), lambda b, p, i: (0, 0)),
            pl.BlockSpec((1, TILE, LANES), lambda b, p, i: (b, i, p)),
            pl.BlockSpec((1, nt, LANES, TILE), lambda b, p, i: (b, 0, p, 0)),
            pl.BlockSpec((1, tt, LANES), lambda b, p, i: (b, 0, p)),
        ],
        out_specs=pl.BlockSpec((1, TILE, LANES), lambda b, p, i: (b, i, p)),
        out_shape=jax.ShapeDtypeStruct((nb, tt, a_w), D_OUT),
        scratch_shapes=[pltpu.VMEM((LANES // A_QK, TILE, LANES), F32),
                        pltpu.VMEM((LANES // A_QK, TILE, LANES), F32)],
        compiler_params=_cparams(("parallel", "parallel", "parallel")),
        name="diffattn",
    )(lam_vecs, nw2, qa, kat, va)


def _na_kernel(q_ref, kt_ref, v_ref, bias_ref, o_ref, *, n_ctx_tiles, n_key_tiles):
    i = pl.program_id(2)
    n_lat = n_key_tiles - n_ctx_tiles
    w0 = n_ctx_tiles + jnp.clip(i - n_ctx_tiles - 1, 0, n_lat - NA_WIN_TILES)
    q = q_ref[0]
    tq = q.shape[0]
    lane = lax.broadcasted_iota(jnp.int32, (tq, LANES), 1)
    lo = lane < HEAD_DIM
    zero = jnp.zeros_like(q)
    outs = []
    for hd in range(2):
        qh = jnp.where(lo if hd == 0 else jnp.logical_not(lo), q, zero)
        tiles = [w0 + c for c in range(NA_WIN_TILES)] + list(range(n_ctx_tiles))
        scores = []
        for c, t in enumerate(tiles):
            s = _dot(qh, kt_ref[0, t])
            if c < NA_WIN_TILES:
                s = s + bias_ref[hd, 0, :, TILE * c:TILE * (c + 1)]
            scores.append(s)
        m = scores[0].max(axis=-1, keepdims=True)
        for s in scores[1:]:
            m = jnp.maximum(m, s.max(axis=-1, keepdims=True))
        l = jnp.zeros((tq, 1), F32)
        acc = jnp.zeros((tq, LANES), F32)
        for s, t in zip(scores, tiles):
            p = jnp.exp(s - m)
            l = l + jnp.sum(p, axis=-1, keepdims=True)
            vt = v_ref[0, pl.ds(pl.multiple_of(t * TILE, TILE), TILE), :]
            acc = acc + _dot(p.astype(D_NA), vt)
        outs.append(acc / l)
    o_ref[0] = jnp.where(lo, outs[0], outs[1]).astype(D_OUT)


def _na(qc, kct, vc, bias, *, n_ctx_tiles):
    nb, tt, c_w = qc.shape
    nt = tt // TILE
    n_lat = nt - n_ctx_tiles

    def bias_idx(b, p, i):
        il = i - n_ctx_tiles
        cls = jnp.where(il < 0, 3, jnp.where(il == 0, 0, jnp.where(il == n_lat - 1, 2, 1)))
        return (p, cls, 0, 0)

    return pl.pallas_call(
        functools.partial(_na_kernel, n_ctx_tiles=n_ctx_tiles, n_key_tiles=nt),
        grid=(nb, c_w // LANES, nt),
        in_specs=[
            pl.BlockSpec((1, TILE, LANES), lambda b, p, i: (b, i, p)),
            pl.BlockSpec((1, nt, LANES, TILE), lambda b, p, i: (b, 0, p, 0)),
            pl.BlockSpec((1, tt, LANES), lambda b, p, i: (b, 0, p)),
            pl.BlockSpec((2, 1, TILE, NA_WIN_TILES * TILE), bias_idx),
        ],
        out_specs=pl.BlockSpec((1, TILE, LANES), lambda b, p, i: (b, i, p)),
        out_shape=jax.ShapeDtypeStruct((nb, tt, c_w), D_OUT),
        compiler_params=_cparams(("parallel", "parallel", "parallel")),
        name="nbr_attn",
    )(qc, kct, vc, bias)


def _na_bias_table(rel_bias, seq):
    rows = seq // GRID_W
    rpt = TILE // GRID_W
    wrows = NA_WIN_TILES * rpt
    assert rows >= wrows and rows >= WIN_ROWS
    n_heads = rel_bias.shape[0]
    n_dr, n_dc = 2 * WIN_ROWS - 1, 2 * WIN_COLS - 1
    qcol, kcol = np.arange(GRID_W)[:, None], np.arange(GRID_W)[None, :]
    kc0 = np.clip(qcol - WIN_COLS // 2, 0, GRID_W - WIN_COLS)
    col_ok = (kcol >= kc0) & (kcol < kc0 + WIN_COLS)
    dcol = kcol - qcol + (WIN_COLS - 1)
    sel_c = ((dcol[None] == np.arange(n_dc)[:, None, None]) & col_ok[None]).astype(np.float32)
    sel_c = sel_c.reshape(n_dc, GRID_W * GRID_W)
    qr, kr = np.arange(rpt)[:, None], np.arange(wrows)[None, :]
    tables = []
    n_lat = rows // rpt
    for il in (0, min(1, n_lat - 1), n_lat - 1):
        r_abs = il * rpt + qr
        k_abs = int(np.clip(il - 1, 0, n_lat - NA_WIN_TILES)) * rpt + kr
        kr0 = np.clip(r_abs - WIN_ROWS // 2, 0, rows - WIN_ROWS)
        row_ok = (k_abs >= kr0) & (k_abs < kr0 + WIN_ROWS)
        drow = k_abs - r_abs + (WIN_ROWS - 1)
        sel_r = ((drow[:, :, None] == np.arange(n_dr)) & row_ok[:, :, None]).astype(np.float32)
        by_row = jnp.einsum("pd,hdc->hpc", sel_r.reshape(rpt * wrows, n_dr), rel_bias.astype(F32),
                            precision=lax.Precision.HIGHEST)
        vals = jnp.einsum("hpc,cx->hpx", by_row, sel_c, precision=lax.Precision.HIGHEST)
        vals = vals.reshape(n_heads, rpt, wrows, GRID_W, GRID_W).transpose(0, 1, 3, 2, 4)
        valid = (row_ok[:, None, :, None] & col_ok[None, :, None, :]).reshape(TILE, NA_WIN_TILES * TILE)
        tables.append(jnp.where(jnp.asarray(valid)[None], vals.reshape(n_heads, TILE, NA_WIN_TILES * TILE), NEG))
    tables.append(jnp.full_like(tables[0], NEG))
    return jnp.stack(tables, axis=1)


def _gdn_prep_kernel(x_ref, prev_ref, next_ref, ab_ref, cw_ref, alog_ref, dtb_ref,
                     q_ref, k_ref, v_ref, sc_ref, sct_ref, *, n_ctx_tiles, n_tiles, n_heads):
    i = pl.program_id(1)
    first = jnp.logical_or(i == 0, i == n_ctx_tiles)
    last = jnp.logical_or(i == n_ctx_tiles - 1, i == n_tiles - 1)
    prev = jnp.where(first, 0.0, prev_ref[0, 0])
    nxt = jnp.where(last, 0.0, next_ref[0, 0])
    ext = jnp.concatenate([prev, x_ref[0], nxt], axis=0)
    cw = cw_ref[...]
    y = None
    for j in range(CONV_W):
        o = SUBLANES - CONV_W // 2 + j
        term = ext[o:o + TILE] * cw[j:j + 1]
        y = term if y is None else y + term
    y = _silu(y)
    b_w = n_heads * HEAD_DIM
    for h in range(n_heads):
        qh = y[:, h * HEAD_DIM:(h + 1) * HEAD_DIM]
        kh = y[:, b_w + h * HEAD_DIM:b_w + (h + 1) * HEAD_DIM]
        vh = y[:, 2 * b_w + h * HEAD_DIM:2 * b_w + (h + 1) * HEAD_DIM]
        q_ref[0, h] = qh * lax.rsqrt(jnp.sum(qh * qh, axis=-1, keepdims=True) + EPS) * (HEAD_DIM ** -0.5)
        k_ref[0, h] = kh * lax.rsqrt(jnp.sum(kh * kh, axis=-1, keepdims=True) + EPS)
        v_ref[0, h] = vh

    ab = ab_ref[0]
    lane = lax.broadcasted_iota(jnp.int32, ab.shape, 1)
    z = ab + dtb_ref[...]
    softplus = jnp.maximum(z, 0.0) + jnp.log(1.0 + jnp.exp(-jnp.abs(z)))
    g = jnp.where(lane < 2 * n_heads, -jnp.exp(alog_ref[...]) * softplus, 0.0)
    beta = 1.0 / (1.0 + jnp.exp(-ab))
    r = lax.broadcasted_iota(jnp.int32, (TILE, TILE), 0)
    c = lax.broadcasted_iota(jnp.int32, (TILE, TILE), 1)
    same = (r // GDN_CHUNK) == (c // GDN_CHUNK)
    tri_f = jnp.where(jnp.logical_and(same, c <= r), 1.0, 0.0).astype(BF16)
    tri_b = jnp.where(jnp.logical_and(same, c >= r), 1.0, 0.0).astype(BF16)
    gc = jnp.where(lane < n_heads, _dot_sel_left(tri_f, g), _dot_sel_left(tri_b, g))
    rr = lax.broadcasted_iota(jnp.int32, (LANES, LANES), 0)
    cc = lax.broadcasted_iota(jnp.int32, (LANES, LANES), 1)
    hh, jj = cc // SUBLANES, cc % SUBLANES
    valid = hh < n_heads
    sel_gb = jnp.where(jnp.logical_and(valid, jnp.logical_and(jj < 4, rr == jj * n_heads + hh)), 1.0, 0.0).astype(BF16)
    sel_gc = jnp.where(jnp.logical_and(valid, jnp.logical_and(jnp.logical_and(jj >= 4, jj < 6),
                                                              rr == (jj - 4) * n_heads + hh)), 1.0, 0.0).astype(BF16)
    gb = jnp.where(lane < 2 * n_heads, g, beta)
    packed = _dot_sel_right(gb, sel_gb) + _dot_sel_right(gc, sel_gc)
    packed_t = packed.T
    for h in range(n_heads):
        sc_ref[0, h] = packed[:, SUBLANES * h:SUBLANES * (h + 1)]
        sct_ref[0, h] = packed_t[SUBLANES * h:SUBLANES * (h + 1), :]


def _gdn_prep(qkvb, ab, conv_w, alog_v, dtb_v, *, n_ctx_tiles, n_heads):
    nb, tt, w3 = qkvb.shape
    nt = tt // TILE
    rows8 = TILE // SUBLANES
    x8 = qkvb.reshape(nb, tt // SUBLANES, SUBLANES, w3)
    hd = lambda w: pl.BlockSpec((1, n_heads, TILE, w), lambda b, i: (b, 0, i, 0))
    return pl.pallas_call(
        functools.partial(_gdn_prep_kernel, n_ctx_tiles=n_ctx_tiles, n_tiles=nt, n_heads=n_heads),
        grid=(nb, nt),
        in_specs=[
            pl.BlockSpec((1, TILE, w3), lambda b, i: (b, i, 0)),
            pl.BlockSpec((1, 1, SUBLANES, w3), lambda b, i: (b, jnp.maximum(i * rows8 - 1, 0), 0, 0)),
            pl.BlockSpec((1, 1, SUBLANES, w3), lambda b, i: (b, jnp.minimum((i + 1) * rows8, nt * rows8 - 1), 0, 0)),
            pl.BlockSpec((1, TILE, LANES), lambda b, i: (b, i, 0)),
            pl.BlockSpec((CONV_W, w3), lambda b, i: (0, 0)),
            pl.BlockSpec((1, LANES), lambda b, i: (0, 0)),
            pl.BlockSpec((1, LANES), lambda b, i: (0, 0)),
        ],
        out_specs=(hd(HEAD_DIM), hd(HEAD_DIM), hd(HEAD_DIM), hd(SUBLANES),
                   pl.BlockSpec((1, n_heads, SUBLANES, TILE), lambda b, i: (b, 0, 0, i))),
        out_shape=(
            jax.ShapeDtypeStruct((nb, n_heads, tt, HEAD_DIM), F32),
            jax.ShapeDtypeStruct((nb, n_heads, tt, HEAD_DIM), F32),
            jax.ShapeDtypeStruct((nb, n_heads, tt, HEAD_DIM), F32),
            jax.ShapeDtypeStruct((nb, n_heads, tt, SUBLANES), F32),
            jax.ShapeDtypeStruct((nb, n_heads, SUBLANES, tt), F32),
        ),
        compiler_params=_cparams(("parallel", "parallel")),
        name="gdn_prep",
    )(qkvb, x8, x8, ab, conv_w, alog_v, dtb_v)


def _bmm(a, b):
    return jnp.einsum("nik,nkj->nij", a, b, preferred_element_type=F32)


def _bmm_nt(a, b):
    return jnp.einsum("nik,njk->nij", a, b, preferred_element_type=F32)


def _tri_inverse(lmat, ii, jj):
    cast = lambda z: z.astype(D_GI)
    blk = 16
    same = (ii // blk) == (jj // blk)
    dmat = jnp.where(same, lmat, 0.0)
    eye = jnp.where(ii == jj, 1.0, 0.0)
    db = cast(dmat)
    x = eye - dmat
    p = _bmm(db, db)
    n_sq = int(math.log2(blk)) - 1
    for step in range(n_sq):
        pb, xb = cast(p), cast(x)
        x = x + _bmm(pb, xb)
        if step < n_sq - 1:
            p = _bmm(pb, pb)
    while blk < lmat.shape[-1]:
        wider = (ii // (2 * blk)) == (jj // (2 * blk))
        off = jnp.where(jnp.logical_and(wider, jnp.logical_not(same)), lmat, 0.0)
        xb = cast(x)
        x = x - _bmm(xb, cast(_bmm(cast(off), xb)))
        same = wider
        blk *= 2
    return x


def _gdn_local_kernel(q_ref, k_ref, v_ref, sc_ref, sct_ref,
                      u_ref, w_ref, qh_ref, aqk_ref, ktt_ref, egl_ref):
    c = GDN_CHUNK
    cpt = TILE // c
    hps = q_ref.shape[1]
    n = hps * 2 * cpt

    def both(x):
        x4 = x.reshape(hps, cpt, c, x.shape[-1])
        return jnp.concatenate([x4, x4], axis=1).reshape(n, c, x.shape[-1])

    def per_dir(x, col):
        x4 = x.reshape(hps, cpt, c, x.shape[-1])
        return jnp.concatenate([x4[..., col:col + 1], x4[..., col + 1:col + 2]], axis=1).reshape(n, c, 1)

    q, k, v = both(q_ref[0]), both(k_ref[0]), both(v_ref[0])
    beta = per_dir(sc_ref[0], 2)
    gcc = per_dir(sc_ref[0], 4)
    sct = sct_ref[0]
    gcr = jnp.stack([sct[h, 4 + d:5 + d, ch * c:(ch + 1) * c]
                     for h in range(hps) for d in range(2) for ch in range(cpt)], axis=0)

    ii = lax.broadcasted_iota(jnp.int32, (n, c, c), 1)
    jj = lax.broadcasted_iota(jnp.int32, (n, c, c), 2)
    fwd = (lax.broadcasted_iota(jnp.int32, (n, c, c), 0) // cpt) % 2 == 0
    incl = jnp.logical_or(jnp.logical_and(fwd, ii >= jj), jnp.logical_and(jnp.logical_not(fwd), ii <= jj))
    strict = jnp.logical_and(incl, ii != jj)
    decay = jnp.where(incl, jnp.exp(jnp.where(incl, gcc - gcr, 0.0)), 0.0)
    kb = k * beta
    a = _bmm_nt(jnp.concatenate([kb, q], axis=1).astype(D_GA), k.astype(D_GA))
    lmat = jnp.where(strict, a[:, :c] * decay, 0.0)
    aqk = a[:, c:] * decay
    tinv = _tri_inverse(lmat, ii, jj).astype(D_GT)
    egc = jnp.exp(gcc)
    u = _bmm(tinv, (v * beta).astype(D_GT))
    w = _bmm(tinv, (kb * egc).astype(D_GT))
    fwd1 = (lax.broadcasted_iota(jnp.int32, (n, 1, 1), 0) // cpt) % 2 == 0
    glast = jnp.where(fwd1, gcc[:, c - 1:c, :], gcc[:, 0:1, :])
    ktail = (k * jnp.exp(glast - gcc)).astype(D_GDN)
    eye = jnp.where(ii == jj, 1.0, 0.0).astype(D_GDN)
    split = lambda z: z.reshape((hps, 2, cpt) + z.shape[1:])
    u_ref[0] = split(u)
    w_ref[0] = split(w.astype(D_GDN))
    qh_ref[0] = split((q * egc).astype(D_GDN))
    aqk_ref[0] = split(aqk.astype(D_GDN))
    ktt_ref[0] = split(_bmm_nt(eye, ktail).astype(D_GDN))
    egl_ref[0] = split(jnp.broadcast_to(jnp.exp(glast), (n, 1, HEAD_DIM)))


def _gdn_local(qn, kn, vv, sc, sct):
    nb, nh, tt, hd = qn.shape
    nt = tt // TILE
    nc = tt // GDN_CHUNK
    cpt = TILE // GDN_CHUNK
    hps = GDN_HEADS_PER_STEP if nh % GDN_HEADS_PER_STEP == 0 else 1
    tok = lambda w: pl.BlockSpec((1, hps, TILE, w), lambda b, h, i: (b, h, i, 0))
    chunked = lambda r, w: pl.BlockSpec((1, hps, 2, cpt, r, w), lambda b, h, i: (b, h, 0, i, 0, 0))
    shape = lambda r, w, dt: jax.ShapeDtypeStruct((nb, nh, 2, nc, r, w), dt)
    return pl.pallas_call(
        _gdn_local_kernel,
        grid=(nb, nh // hps, nt),
        in_specs=[tok(hd), tok(hd), tok(hd), tok(SUBLANES),
                  pl.BlockSpec((1, hps, SUBLANES, TILE), lambda b, h, i: (b, h, 0, i))],
        out_specs=(chunked(GDN_CHUNK, hd), chunked(GDN_CHUNK, hd), chunked(GDN_CHUNK, hd),
                   chunked(GDN_CHUNK, GDN_CHUNK), chunked(hd, GDN_CHUNK), chunked(1, hd)),
        out_shape=(
            shape(GDN_CHUNK, hd, F32),
            shape(GDN_CHUNK, hd, D_GDN),
            shape(GDN_CHUNK, hd, D_GDN),
            shape(GDN_CHUNK, GDN_CHUNK, D_GDN),
            shape(hd, GDN_CHUNK, D_GDN),
            shape(1, hd, F32),
        ),
        compiler_params=_cparams(("parallel", "parallel", "parallel")),
        name="gdn_local",
    )(qn, kn, vv, sc, sct)


def _gdn_scan_kernel(uf, ub, wf, wb, qf, qb, af, ab, kf, kb, ef, eb, of_ref, ob_ref, s_ref, *, cpt, n_heads):
    @pl.when(pl.program_id(1) == 0)
    def _():
        s_ref[...] = jnp.zeros_like(s_ref)

    s = s_ref[...]
    for t in range(cpt):
        tb = cpt - 1 - t
        pair = lambda f, b: jnp.concatenate([f[0, :, 0, t], b[0, :, 0, tb]], axis=0)
        sb = s.astype(D_GS)
        v_new = pair(uf, ub) - _bmm(pair(wf, wb), sb)
        vb = v_new.astype(D_GS)
        o = _bmm(pair(qf, qb), sb) + _bmm(pair(af, ab), vb)
        s = s * pair(ef, eb) + _bmm(pair(kf, kb), vb)
        of_ref[0, :, t] = o[:n_heads]
        ob_ref[0, :, tb] = o[n_heads:]
    s_ref[...] = s


def _gdn_scan(u, w, qh, aqk, ktt, egl, *, n_ctx_tiles):
    nb, nh, _, nc, c, hd = u.shape
    cpt = TILE // c
    nt = nc // cpt

    def bwd_tile(j):
        return jnp.where(j < n_ctx_tiles, n_ctx_tiles - 1 - j, nt - 1 + n_ctx_tiles - j)

    def specs(a):
        blk = (1, nh, 1, cpt) + a.shape[4:]
        return [pl.BlockSpec(blk, lambda b, j: (b, 0, 0, j, 0, 0)),
                pl.BlockSpec(blk, lambda b, j: (b, 0, 1, bwd_tile(j), 0, 0))]

    args, in_specs = [], []
    for a in (u, w, qh, aqk, ktt, egl):
        args += [a, a]
        in_specs += specs(a)
    oshape = jax.ShapeDtypeStruct((nb, nh, nc, c, hd), F32)
    return pl.pallas_call(
        functools.partial(_gdn_scan_kernel, cpt=cpt, n_heads=nh),
        grid=(nb, nt),
        in_specs=in_specs,
        out_specs=(pl.BlockSpec((1, nh, cpt, c, hd), lambda b, j: (b, 0, j, 0, 0)),
                   pl.BlockSpec((1, nh, cpt, c, hd), lambda b, j: (b, 0, bwd_tile(j), 0, 0))),
        out_shape=(oshape, oshape),
        scratch_shapes=[pltpu.VMEM((2 * nh, hd, hd), F32)],
        compiler_params=_cparams(("parallel", "arbitrary")),
        name="gdn_scan",
    )(*args)


def _outproj_kernel(x_ref, mod_ref, ya_ref, of_ref, ob_ref, gate_ref, gnw_ref, yc_ref, w_ref, o_ref,
                    *, a_w, b_w, n_heads):
    acc = _dot(ya_ref[0], w_ref[0:a_w, :]) + _dot(yc_ref[0], w_ref[a_w + b_w:, :])
    gate = gate_ref[0]
    for h in range(n_heads):
        o = of_ref[0, h] + ob_ref[0, h]
        ms = jnp.mean(o * o, axis=-1, keepdims=True)
        y = o * lax.rsqrt(ms + EPS) * gnw_ref[...]
        yb = (y * _silu(gate[:, h * HEAD_DIM:(h + 1) * HEAD_DIM])).astype(D_OUT)
        acc = acc + _dot(yb, w_ref[a_w + h * HEAD_DIM:a_w + (h + 1) * HEAD_DIM, :])
    o_ref[0] = x_ref[0] + mod_ref[0][2:3] * acc


def _outproj(xs, mods, ya, scan_f, scan_b, gate, gnw, yc, w, *, n_batch, n_ctx_tiles, n_heads):
    nb, tt, d = xs.shape
    nt = tt // TILE
    a_w, b_w, c_w = ya.shape[2], gate.shape[2], yc.shape[2]

    def mod_idx(b, i):
        return (jnp.where(i < n_ctx_tiles, n_batch, b), 0, 0)

    tok = lambda wd: pl.BlockSpec((1, TILE, wd), lambda b, i: (b, i, 0))
    heads = pl.BlockSpec((1, n_heads, TILE, HEAD_DIM), lambda b, i: (b, 0, i, 0))
    return pl.pallas_call(
        functools.partial(_outproj_kernel, a_w=a_w, b_w=b_w, n_heads=n_heads),
        grid=(nb, nt),
        in_specs=[
            tok(d),
            pl.BlockSpec((1, 6, d), mod_idx),
            tok(a_w),
            heads,
            heads,
            tok(b_w),
            pl.BlockSpec((1, HEAD_DIM), lambda b, i: (0, 0)),
            tok(c_w),
            pl.BlockSpec(w.shape, lambda b, i: (0, 0)),
        ],
        out_specs=tok(d),
        out_shape=jax.ShapeDtypeStruct((nb, tt, d), F32),
        compiler_params=_cparams(("parallel", "parallel")),
        name="outproj",
    )(xs, mods, ya, scan_f, scan_b, gate, gnw, yc, w)


def _ffn_kernel(x_ref, mod_ref, nw_ref, wi_ref, wo_ref, o_ref, *, d_ff):
    mod = mod_ref[0]
    x = x_ref[0]
    h = _norm_mod(x, nw_ref[...], mod[3:4], mod[4:5]).astype(D_FFN)
    gu = _dot(h, wi_ref[...])
    a = (_silu(gu[:, :d_ff]) * gu[:, d_ff:]).astype(D_FFN)
    o_ref[0] = x + mod[5:6] * _dot(a, wo_ref[...])


def _ffn(xs, mods, nw, wi, wo, *, n_batch, n_ctx_tiles):
    nb, tt, d = xs.shape
    nt = tt // TILE
    d_ff = wo.shape[0]

    def mod_idx(b, i):
        return (jnp.where(i < n_ctx_tiles, n_batch, b), 0, 0)

    tok = pl.BlockSpec((1, TILE, d), lambda b, i: (b, i, 0))
    return pl.pallas_call(
        functools.partial(_ffn_kernel, d_ff=d_ff),
        grid=(nb, nt),
        in_specs=[
            tok,
            pl.BlockSpec((1, 6, d), mod_idx),
            pl.BlockSpec((1, d), lambda b, i: (0, 0)),
            pl.BlockSpec(wi.shape, lambda b, i: (0, 0), pipeline_mode=pl.Buffered(1)),
            pl.BlockSpec(wo.shape, lambda b, i: (0, 0), pipeline_mode=pl.Buffered(1)),
        ],
        out_specs=tok,
        out_shape=jax.ShapeDtypeStruct((nb, tt, d), F32),
        compiler_params=_cparams(("parallel", "parallel")),
        name="ffn",
    )(xs, mods, nw, wi, wo)


def _final_kernel(x_ref, nw_ref, o_ref):
    x = x_ref[0]
    ms = jnp.mean(x * x, axis=-1, keepdims=True)
    o_ref[0] = x * lax.rsqrt(ms + EPS) * nw_ref[...]


def _final_norm(xs, nw, *, n_ctx_tiles):
    nb, tt, d = xs.shape
    n_lat = tt // TILE - n_ctx_tiles
    return pl.pallas_call(
        _final_kernel,
        grid=(nb, n_lat),
        in_specs=[pl.BlockSpec((1, TILE, d), lambda b, i: (b, i + n_ctx_tiles, 0)),
                  pl.BlockSpec((1, d), lambda b, i: (0, 0))],
        out_specs=pl.BlockSpec((1, TILE, d), lambda b, i: (b, i, 0)),
        out_shape=jax.ShapeDtypeStruct((nb, n_lat * TILE, d), F32),
        compiler_params=_cparams(("parallel", "parallel")),
        name="final_norm",
    )(xs, nw)


def _rope_tables(seq, ctx_len, n_rep):
    half = A_QK // 2
    inv_freq = 1.0 / (ROPE_THETA ** (jnp.arange(0, half, 2, dtype=F32) / half))
    t = jnp.arange(seq, dtype=jnp.int32)
    ang_r = (t // GRID_W).astype(F32)[:, None] * inv_freq
    ang_c = (t % GRID_W).astype(F32)[:, None] * inv_freq
    ang = jnp.concatenate([ang_r, ang_r, ang_c, ang_c], axis=-1)
    cos = jnp.concatenate([jnp.ones((ctx_len, A_QK), F32), jnp.cos(ang)], axis=0)
    sin = jnp.concatenate([jnp.zeros((ctx_len, A_QK), F32), jnp.sin(ang)], axis=0)
    return jnp.tile(cos, (1, n_rep)), jnp.tile(sin, (1, n_rep))


def kernel(x, c, ctx, c_ctx, w_mod, b_mod, norm1_w, norm2_w, w_in, w_out, lambda_q1, lambda_k1, lambda_q2,
           lambda_k2, diff_norm_w, conv_w, a_log, dt_bias, gdn_norm_w, na_bias, w_ffn_in, w_ffn_out,
           final_norm_w):
    nb, seq, d = x.shape
    ctx_len = ctx.shape[1]
    depth = w_mod.shape[0]
    b_heads = a_log.shape[-1]
    c_heads = na_bias.shape[1]
    b_w = b_heads * HEAD_DIM
    c_w = c_heads * HEAD_DIM
    a_w = w_out.shape[1] - b_w - c_w
    assert seq % TILE == 0 and ctx_len % TILE == 0 and seq % GRID_W == 0
    assert a_w % LANES == 0 and b_w % LANES == 0 and c_w % LANES == 0 and 4 * b_heads <= LANES
    n_ctx_tiles = ctx_len // TILE

    xs = jnp.concatenate([ctx, x], axis=1)

    n_rows = -(-(nb + 1) // SUBLANES) * SUBLANES
    cc = jnp.concatenate([c, c_ctx[None, :], jnp.zeros((n_rows - nb - 1, d), F32)], axis=0)
    mods_all = _modulation(cc, w_mod, b_mod).reshape(depth, n_rows, 6, d)

    cos, sin = _rope_tables(seq, ctx_len, a_w // A_QK)
    cost, sint = cos.T, sin.T

    sizes = (3 * a_w, 3 * b_w, b_w, 2 * b_heads, 2 * b_heads, 3 * c_w)
    offs = np.concatenate([[0], np.cumsum(sizes)])
    o_a, o_b, o_g, o_al, o_be, o_c = (int(v) for v in offs[:6])
    pad = LANES - 4 * b_heads

    for l in range(depth):
        wl = w_in[l]
        wm = jnp.concatenate([
            wl[:, o_a:o_a + a_w],
            wl[:, o_a + 2 * a_w:o_a + 3 * a_w],
            wl[:, o_b:o_b + 3 * b_w],
            wl[:, o_g:o_g + b_w],
            wl[:, o_c:o_c + c_w],
            wl[:, o_c + 2 * c_w:o_c + 3 * c_w],
            wl[:, o_al:o_al + 4 * b_heads],
            jnp.zeros((d, pad), F32),
        ], axis=1).astype(D_IN)
        wk = jnp.concatenate([wl[:, o_a + a_w:o_a + 2 * a_w], wl[:, o_c + c_w:o_c + 2 * c_w]], axis=1).T.astype(D_IN)
        mods = mods_all[l]
        lam_init = 0.8 - 0.6 * math.exp(-0.3 * l)

        qa, kat, va, qkvb, gate, ab, qc, kct, vc = _inproj(
            xs, mods, norm1_w[l][None, :], wm, wk, cos, sin, cost, sint,
            n_batch=nb, n_ctx_tiles=n_ctx_tiles, a_w=a_w, b_w=b_w, c_w=c_w)

        lam_vecs = jnp.stack([lambda_q1[l], lambda_k1[l], lambda_q2[l], lambda_k2[l]], axis=0).astype(F32)
        nw2 = jnp.tile(diff_norm_w[l], LANES // HEAD_DIM)[None, :]
        ya = _diffattn(lam_vecs, nw2, qa, kat, va, n_ctx_tiles=n_ctx_tiles, lam_init=lam_init)

        yc = _na(qc, kct, vc, _na_bias_table(na_bias[l], seq), n_ctx_tiles=n_ctx_tiles)

        alog_v = jnp.concatenate([a_log[l].reshape(-1), jnp.zeros((LANES - 2 * b_heads,), F32)])[None, :]
        dtb_v = jnp.concatenate([dt_bias[l].reshape(-1), jnp.zeros((LANES - 2 * b_heads,), F32)])[None, :]
        qn, kn, vv, sc, sct = _gdn_prep(qkvb, ab, conv_w[l], alog_v, dtb_v,
                                        n_ctx_tiles=n_ctx_tiles, n_heads=b_heads)
        u, w, qh, aqk, ktt, egl = _gdn_local(qn, kn, vv, sc, sct)
        scan_f, scan_b = _gdn_scan(u, w, qh, aqk, ktt, egl, n_ctx_tiles=n_ctx_tiles)
        scan_f = scan_f.reshape(nb, b_heads, ctx_len + seq, HEAD_DIM)
        scan_b = scan_b.reshape(nb, b_heads, ctx_len + seq, HEAD_DIM)

        xs = _outproj(xs, mods, ya, scan_f, scan_b, gate, gdn_norm_w[l][None, :], yc, w_out[l].astype(D_OUT),
                      n_batch=nb, n_ctx_tiles=n_ctx_tiles, n_heads=b_heads)
        xs = _ffn(xs, mods, norm2_w[l][None, :], w_ffn_in[l].astype(D_FFN), w_ffn_out[l].astype(D_FFN),
                  n_batch=nb, n_ctx_tiles=n_ctx_tiles)

    return _final_norm(xs, final_norm_w[None, :], n_ctx_tiles=n_ctx_tiles)
```

```python
import functools
import math

import jax
import jax.numpy as jnp
import numpy as np
from jax import lax
from jax.experimental import pallas as pl
from jax.experimental.pallas import tpu as pltpu

F32 = jnp.float32
BF16 = jnp.bfloat16
D_MOD = D_IN = D_OUT = D_FFN = D_ATT = D_NA = D_GDN = BF16
D_GA = D_GI = D_GT = D_GS = BF16

HEAD_DIM = 64
A_QK = HEAD_DIM // 2
GRID_W = 64
ROPE_THETA = 10000.0
CONV_W = 5
GDN_CHUNK = 64
WIN_ROWS = 8
WIN_COLS = 16
EPS = 1e-6

LANES = 128
SUBLANES = 8
TILE = 256
NA_WIN_TILES = 3
GDN_HEADS_PER_STEP = 6
NEG = -1e30
VMEM_LIMIT = 56 * 1024 * 1024


def _cparams(sem):
    return pltpu.CompilerParams(dimension_semantics=sem, vmem_limit_bytes=VMEM_LIMIT)


def _silu(x):
    return x * (1.0 / (1.0 + jnp.exp(-x)))


def _dot(a, b):
    return jnp.dot(a, b, preferred_element_type=F32)


def _dot_nt(a, b):
    return lax.dot_general(a, b, (((1,), (1,)), ((), ())), preferred_element_type=F32)


def _split3(x):
    hi = x.astype(BF16)
    r1 = x - hi.astype(F32)
    mid = r1.astype(BF16)
    lo = (r1 - mid.astype(F32)).astype(BF16)
    return hi, mid, lo


def _dot_sel_right(x, sel):
    hi, mid, lo = _split3(x)
    return _dot(hi, sel) + _dot(mid, sel) + _dot(lo, sel)


def _dot_sel_left(sel, x):
    hi, mid, lo = _split3(x)
    return _dot(sel, hi) + _dot(sel, mid) + _dot(sel, lo)


def _mod_kernel(c_ref, w_ref, b_ref, o_ref):
    s = _silu(c_ref[...])
    o_ref[0] = _dot(s.astype(D_MOD), w_ref[0].astype(D_MOD)) + b_ref[0]


def _modulation(cc, w_mod, b_mod):
    depth, d, n = w_mod.shape
    rows = cc.shape[0]
    tn = 1536
    assert n % tn == 0
    return pl.pallas_call(
        _mod_kernel,
        grid=(depth, n // tn),
        in_specs=[
            pl.BlockSpec((rows, d), lambda l, j: (0, 0)),
            pl.BlockSpec((1, d, tn), lambda l, j: (l, 0, j)),
            pl.BlockSpec((1, 1, tn), lambda l, j: (l, 0, j)),
        ],
        out_specs=pl.BlockSpec((1, rows, tn), lambda l, j: (l, 0, j)),
        out_shape=jax.ShapeDtypeStruct((depth, rows, n), F32),
        compiler_params=_cparams(("parallel", "parallel")),
        name="modulation",
    )(cc, w_mod, b_mod.reshape(depth, 1, n))


def _norm_mod(x, nw, shift, scale):
    ms = jnp.mean(x * x, axis=-1, keepdims=True)
    y = x * lax.rsqrt(ms + EPS) * nw
    return y * (1.0 + scale) + shift


def _rope_lanes(x, cos, sin):
    parts = []
    for j in range(x.shape[1] // LANES):
        xj = x[:, LANES * j:LANES * (j + 1)]
        nxt = pltpu.roll(xj, LANES - 8, axis=1)
        prv = pltpu.roll(xj, 8, axis=1)
        lane = lax.broadcasted_iota(jnp.int32, xj.shape, 1)
        parts.append(jnp.where((lane % 16) < 8, -nxt, prv))
    rot = jnp.concatenate(parts, axis=1)
    return x * cos + rot * sin


def _rope_rows(x, cos, sin):
    n = x.shape[0]
    nxt = pltpu.roll(x, n - 8, axis=0)
    prv = pltpu.roll(x, 8, axis=0)
    row = lax.broadcasted_iota(jnp.int32, x.shape, 0)
    rot = jnp.where((row % 16) < 8, -nxt, prv)
    return x * cos + rot * sin


def _inproj_kernel(x_ref, mod_ref, nw_ref, wm_ref, wk_ref, cos_ref, sin_ref, cost_ref, sint_ref,
                   qa_ref, kat_ref, va_ref, qkvb_ref, gate_ref, ab_ref, qc_ref, kct_ref, vc_ref,
                   *, a_w, b_w, c_w):
    mod = mod_ref[0]
    h = _norm_mod(x_ref[0], nw_ref[...], mod[0:1], mod[1:2]).astype(D_IN)
    acc = _dot(h, wm_ref[...])
    kt = _dot_nt(wk_ref[...], h)
    o = 0
    qa = _rope_lanes(acc[:, o:o + a_w], cos_ref[...], sin_ref[...])
    qa_ref[0] = (qa * (A_QK ** -0.5 * math.log2(math.e))).astype(D_ATT)
    o += a_w
    va_ref[0] = acc[:, o:o + a_w].astype(D_ATT)
    o += a_w
    qkvb_ref[0] = acc[:, o:o + 3 * b_w]
    o += 3 * b_w
    gate_ref[0] = acc[:, o:o + b_w]
    o += b_w
    qc_ref[0] = (acc[:, o:o + c_w] * (HEAD_DIM ** -0.5)).astype(D_NA)
    o += c_w
    vc_ref[0] = acc[:, o:o + c_w].astype(D_NA)
    o += c_w
    ab_ref[0] = acc[:, o:o + LANES]
    kat_ref[0, 0] = _rope_rows(kt[:a_w], cost_ref[...], sint_ref[...]).astype(D_ATT)
    kct_ref[0, 0] = kt[a_w:].astype(D_NA)


def _inproj(xs, mods, nw, wm, wk, cos, sin, cost, sint, *, n_batch, n_ctx_tiles, a_w, b_w, c_w):
    nb, tt, d = xs.shape
    nt = tt // TILE
    nm = wm.shape[1]
    ctx_row = n_batch

    def mod_idx(b, i):
        return (jnp.where(i < n_ctx_tiles, ctx_row, b), 0, 0)

    tok = lambda w: pl.BlockSpec((1, TILE, w), lambda b, i: (b, i, 0))
    tokt = lambda w: pl.BlockSpec((1, 1, w, TILE), lambda b, i: (b, i, 0, 0))
    out_shape = (
        jax.ShapeDtypeStruct((nb, tt, a_w), D_ATT),
        jax.ShapeDtypeStruct((nb, nt, a_w, TILE), D_ATT),
        jax.ShapeDtypeStruct((nb, tt, a_w), D_ATT),
        jax.ShapeDtypeStruct((nb, tt, 3 * b_w), F32),
        jax.ShapeDtypeStruct((nb, tt, b_w), F32),
        jax.ShapeDtypeStruct((nb, tt, LANES), F32),
        jax.ShapeDtypeStruct((nb, tt, c_w), D_NA),
        jax.ShapeDtypeStruct((nb, nt, c_w, TILE), D_NA),
        jax.ShapeDtypeStruct((nb, tt, c_w), D_NA),
    )
    return pl.pallas_call(
        functools.partial(_inproj_kernel, a_w=a_w, b_w=b_w, c_w=c_w),
        grid=(nb, nt),
        in_specs=[
            tok(d),
            pl.BlockSpec((1, 6, d), mod_idx),
            pl.BlockSpec((1, d), lambda b, i: (0, 0)),
            pl.BlockSpec((d, nm), lambda b, i: (0, 0)),
            pl.BlockSpec((a_w + c_w, d), lambda b, i: (0, 0)),
            pl.BlockSpec((TILE, a_w), lambda b, i: (i, 0)),
            pl.BlockSpec((TILE, a_w), lambda b, i: (i, 0)),
            pl.BlockSpec((a_w, TILE), lambda b, i: (0, i)),
            pl.BlockSpec((a_w, TILE), lambda b, i: (0, i)),
        ],
        out_specs=(tok(a_w), tokt(a_w), tok(a_w), tok(3 * b_w), tok(b_w), tok(LANES),
                   tok(c_w), tokt(c_w), tok(c_w)),
        out_shape=out_shape,
        compiler_params=_cparams(("parallel", "parallel")),
        name="inproj",
    )(xs, mods, nw, wm, wk, cos, sin, cost, sint)


def _diffattn_kernel(lam_ref, nw_ref, q_ref, kt_ref, v_ref, o_ref, mx_ref, acc_ref,
                     *, n_ctx_tiles, n_key_tiles, lam_init):
    i = pl.program_id(2)
    lv = lam_ref[...]
    lam = (jnp.exp(jnp.sum(lv[0:1] * lv[1:2], axis=-1, keepdims=True))
           - jnp.exp(jnp.sum(lv[2:3] * lv[3:4], axis=-1, keepdims=True)) + lam_init)
    q = q_ref[0]
    tq = q.shape[0]
    lane = lax.broadcasted_iota(jnp.int32, (tq, LANES), 1)
    lo = lane < HEAD_DIM
    zero = jnp.zeros_like(q)
    qm = [jnp.where((lane // A_QK) == m, q, zero) for m in range(LANES // A_QK)]
    n_maps = len(qm)
    n_lat = n_key_tiles - n_ctx_tiles
    group = max(u for u in (4, 2, 1) if n_lat % u == 0)
    ctx_tiles = list(range(n_ctx_tiles))
    lat_tiles = lambda g: [n_ctx_tiles + g * group + u for u in range(group)]

    mx_ref[...] = jnp.full(mx_ref.shape, NEG, F32)
    acc_ref[...] = jnp.zeros(acc_ref.shape, F32)
    lane_v = lax.broadcasted_iota(jnp.int32, (TILE, LANES), 1)

    def visit(tiles):
        ktiles = [kt_ref[0, j] for j in tiles]
        v_ext = []
        for j in tiles:
            start = j * TILE if isinstance(j, int) else pl.multiple_of(j * TILE, TILE)
            vtile = v_ref[0, pl.ds(start, TILE), :]
            ones = jnp.ones_like(vtile)
            v_ext.append((jnp.where(lane_v < HEAD_DIM, vtile, ones), jnp.where(lane_v < HEAD_DIM, ones, vtile)))
        scores = [[_dot(qm[m], ktile) for ktile in ktiles] for m in range(n_maps)]
        m_new = []
        for m in range(n_maps):
            top = scores[m][0]
            for s in scores[m][1:]:
                top = jnp.maximum(top, s)
            top = jnp.max(jnp.maximum(top[:, :LANES], top[:, LANES:]), axis=-1, keepdims=True)
            m_new.append(jnp.maximum(mx_ref[m], top))
        probs = [[jnp.exp2(s - m_new[m]).astype(D_ATT) for s in scores[m]] for m in range(n_maps)]
        for m in range(n_maps):
            acc = acc_ref[m] * jnp.exp2(mx_ref[m] - m_new[m])
            for p, ve in zip(probs[m], v_ext):
                acc = acc + _dot(p, ve[m // 2])
            mx_ref[m] = m_new[m]
            acc_ref[m] = acc

    @pl.when(i < n_ctx_tiles)
    def _():
        visit(ctx_tiles)

    @pl.when(i >= n_ctx_tiles)
    def _():
        visit(ctx_tiles + lat_tiles(0))
        lax.fori_loop(1, n_lat // group, lambda g, c: (visit(lat_tiles(g)), c)[1], 0)
    attn = []
    for m in range(n_maps):
        a = acc_ref[m]
        attn.append(a / pltpu.roll(a, HEAD_DIM, axis=1))
    o = jnp.where(lo, attn[0] - lam * attn[1], attn[2] - lam * attn[3])
    sq = o * o
    ms_lo = jnp.sum(jnp.where(lo, sq, 0.0), axis=-1, keepdims=True) * (1.0 / HEAD_DIM)
    ms_hi = jnp.sum(jnp.where(lo, 0.0, sq), axis=-1, keepdims=True) * (1.0 / HEAD_DIM)
    y = o * lax.rsqrt(jnp.where(lo, ms_lo, ms_hi) + EPS) * nw_ref[...]
    o_ref[0] = (y * (1.0 - lam_init)).astype(D_OUT)


def _diffattn(lam_vecs, nw2, qa, kat, va, *, n_ctx_tiles, lam_init):
    nb, tt, a_w = qa.shape
    nt = tt // TILE
    return pl.pallas_call(
        functools.partial(_diffattn_kernel, n_ctx_tiles=n_ctx_tiles, n_key_tiles=nt, lam_init=lam_init),
        grid=(nb, a_w // LANES, nt),
        in_specs=[
            pl.BlockSpec(lam_vecs.shape, lambda b, p, i: (0, 0)),
            pl.BlockSpec((1, LANES), lambda b, p, i: (0, 0)),
            pl.BlockSpec((1, TILE, LANES), lambda b, p, i: (b, i, p)),
            pl.BlockSpec((1, nt, LANES, TILE), lambda b, p, i: (b, 0, p, 0)),
            pl.BlockSpec((1, tt, LANES), lambda b, p, i: (b, 0, p)),
        ],
        out_specs=pl.BlockSpec((1, TILE, LANES), lambda b, p, i: (b, i, p)),
        out_shape=jax.ShapeDtypeStruct((nb, tt, a_w), D_OUT),
        scratch_shapes=[pltpu.VMEM((LANES // A_QK, TILE, 1), F32),
                        pltpu.VMEM((LANES // A_QK, TILE, LANES), F32)],
        compiler_params=_cparams(("parallel", "parallel", "parallel")),
        name="diffattn",
    )(lam_vecs, nw2, qa, kat, va)


def _na_kernel(q_ref, kt_ref, v_ref, bias_ref, o_ref, *, n_ctx_tiles, n_key_tiles):
    i = pl.program_id(2)
    n_lat = n_key_tiles - n_ctx_tiles
    w0 = n_ctx_tiles + jnp.clip(i - n_ctx_tiles - 1, 0, n_lat - NA_WIN_TILES)
    q = q_ref[0]
    tq = q.shape[0]
    lane = lax.broadcasted_iota(jnp.int32, (tq, LANES), 1)
    lo = lane < HEAD_DIM
    zero = jnp.zeros_like(q)
    outs = []
    for hd in range(2):
        qh = jnp.where(lo if hd == 0 else jnp.logical_not(lo), q, zero)
        tiles = [w0 + c for c in range(NA_WIN_TILES)] + list(range(n_ctx_tiles))
        scores = []
        for c, t in enumerate(tiles):
            s = _dot(qh, kt_ref[0, t])
            if c < NA_WIN_TILES:
                s = s + bias_ref[hd, 0, :, TILE * c:TILE * (c + 1)]
            scores.append(s)
        m = scores[0].max(axis=-1, keepdims=True)
        for s in scores[1:]:
            m = jnp.maximum(m, s.max(axis=-1, keepdims=True))
        l = jnp.zeros((tq, 1), F32)
        acc = jnp.zeros((tq, LANES), F32)
        for s, t in zip(scores, tiles):
            p = jnp.exp(s - m)
            l = l + jnp.sum(p, axis=-1, keepdims=True)
            vt = v_ref[0, pl.ds(pl.multiple_of(t * TILE, TILE), TILE), :]
            acc = acc + _dot(p.astype(D_NA), vt)
        outs.append(acc / l)
    o_ref[0] = jnp.where(lo, outs[0], outs[1]).astype(D_OUT)


def _na(qc, kct, vc, bias, *, n_ctx_tiles):
    nb, tt, c_w = qc.shape
    nt = tt // TILE
    n_lat = nt - n_ctx_tiles

    def bias_idx(b, p, i):
        il = i - n_ctx_tiles
        cls = jnp.where(il < 0, 3, jnp.where(il == 0, 0, jnp.where(il == n_lat - 1, 2, 1)))
        return (p, cls, 0, 0)

    return pl.pallas_call(
        functools.partial(_na_kernel, n_ctx_tiles=n_ctx_tiles, n_key_tiles=nt),
        grid=(nb, c_w // LANES, nt),
        in_specs=[
            pl.BlockSpec((1, TILE, LANES), lambda b, p, i: (b, i, p)),
            pl.BlockSpec((1, nt, LANES, TILE), lambda b, p, i: (b, 0, p, 0)),
            pl.BlockSpec((1, tt, LANES), lambda b, p, i: (b, 0, p)),
            pl.BlockSpec((2, 1, TILE, NA_WIN_TILES * TILE), bias_idx),
        ],
        out_specs=pl.BlockSpec((1, TILE, LANES), lambda b, p, i: (b, i, p)),
        out_shape=jax.ShapeDtypeStruct((nb, tt, c_w), D_OUT),
        compiler_params=_cparams(("parallel", "parallel", "parallel")),
        name="nbr_attn",
    )(qc, kct, vc, bias)


def _na_bias_table(rel_bias, seq):
    rows = seq // GRID_W
    rpt = TILE // GRID_W
    wrows = NA_WIN_TILES * rpt
    assert rows >= wrows and rows >= WIN_ROWS
    n_heads = rel_bias.shape[0]
    n_dr, n_dc = 2 * WIN_ROWS - 1, 2 * WIN_COLS - 1
    qcol, kcol = np.arange(GRID_W)[:, None], np.arange(GRID_W)[None, :]
    kc0 = np.clip(qcol - WIN_COLS // 2, 0, GRID_W - WIN_COLS)
    col_ok = (kcol >= kc0) & (kcol < kc0 + WIN_COLS)
    dcol = kcol - qcol + (WIN_COLS - 1)
    sel_c = ((dcol[None] == np.arange(n_dc)[:, None, None]) & col_ok[None]).astype(np.float32)
    sel_c = sel_c.reshape(n_dc, GRID_W * GRID_W)
    qr, kr = np.arange(rpt)[:, None], np.arange(wrows)[None, :]
    tables = []
    n_lat = rows // rpt
    for il in (0, min(1, n_lat - 1), n_lat - 1):
        r_abs = il * rpt + qr
        k_abs = int(np.clip(il - 1, 0, n_lat - NA_WIN_TILES)) * rpt + kr
        kr0 = np.clip(r_abs - WIN_ROWS // 2, 0, rows - WIN_ROWS)
        row_ok = (k_abs >= kr0) & (k_abs < kr0 + WIN_ROWS)
        drow = k_abs - r_abs + (WIN_ROWS - 1)
        sel_r = ((drow[:, :, None] == np.arange(n_dr)) & row_ok[:, :, None]).astype(np.float32)
        by_row = jnp.einsum("pd,hdc->hpc", sel_r.reshape(rpt * wrows, n_dr), rel_bias.astype(F32),
                            precision=lax.Precision.HIGHEST)
        vals = jnp.einsum("hpc,cx->hpx", by_row, sel_c, precision=lax.Precision.HIGHEST)
        vals = vals.reshape(n_heads, rpt, wrows, GRID_W, GRID_W).transpose(0, 1, 3, 2, 4)
        valid = (row_ok[:, None, :, None] & col_ok[None, :, None, :]).reshape(TILE, NA_WIN_TILES * TILE)
        tables.append(jnp.where(jnp.asarray(valid)[None], vals.reshape(n_heads, TILE, NA_WIN_TILES * TILE), NEG))
    tables.append(jnp.full_like(tables[0], NEG))
    return jnp.stack(tables, axis=1)


def _gdn_prep_kernel(x_ref, prev_ref, next_ref, ab_ref, cw_ref, alog_ref, dtb_ref,
                     q_ref, k_ref, v_ref, sc_ref, sct_ref, *, n_ctx_tiles, n_tiles, n_heads):
    i = pl.program_id(1)
    first = jnp.logical_or(i == 0, i == n_ctx_tiles)
    last = jnp.logical_or(i == n_ctx_tiles - 1, i == n_tiles - 1)
    prev = jnp.where(first, 0.0, prev_ref[0, 0])
    nxt = jnp.where(last, 0.0, next_ref[0, 0])
    ext = jnp.concatenate([prev, x_ref[0], nxt], axis=0)
    cw = cw_ref[...]
    y = None
    for j in range(CONV_W):
        o = SUBLANES - CONV_W // 2 + j
        term = ext[o:o + TILE] * cw[j:j + 1]
        y = term if y is None else y + term
    y = _silu(y)
    b_w = n_heads * HEAD_DIM
    for h in range(n_heads):
        qh = y[:, h * HEAD_DIM:(h + 1) * HEAD_DIM]
        kh = y[:, b_w + h * HEAD_DIM:b_w + (h + 1) * HEAD_DIM]
        vh = y[:, 2 * b_w + h * HEAD_DIM:2 * b_w + (h + 1) * HEAD_DIM]
        q_ref[0, h] = qh * lax.rsqrt(jnp.sum(qh * qh, axis=-1, keepdims=True) + EPS) * (HEAD_DIM ** -0.5)
        k_ref[0, h] = kh * lax.rsqrt(jnp.sum(kh * kh, axis=-1, keepdims=True) + EPS)
        v_ref[0, h] = vh

    ab = ab_ref[0]
    lane = lax.broadcasted_iota(jnp.int32, ab.shape, 1)
    z = ab + dtb_ref[...]
    softplus = jnp.maximum(z, 0.0) + jnp.log(1.0 + jnp.exp(-jnp.abs(z)))
    g = jnp.where(lane < 2 * n_heads, -jnp.exp(alog_ref[...]) * softplus, 0.0)
    beta = 1.0 / (1.0 + jnp.exp(-ab))
    r = lax.broadcasted_iota(jnp.int32, (TILE, TILE), 0)
    c = lax.broadcasted_iota(jnp.int32, (TILE, TILE), 1)
    same = (r // GDN_CHUNK) == (c // GDN_CHUNK)
    tri_f = jnp.where(jnp.logical_and(same, c <= r), 1.0, 0.0).astype(BF16)
    tri_b = jnp.where(jnp.logical_and(same, c >= r), 1.0, 0.0).astype(BF16)
    gc = jnp.where(lane < n_heads, _dot_sel_left(tri_f, g), _dot_sel_left(tri_b, g))
    rr = lax.broadcasted_iota(jnp.int32, (LANES, LANES), 0)
    cc = lax.broadcasted_iota(jnp.int32, (LANES, LANES), 1)
    hh, jj = cc // SUBLANES, cc % SUBLANES
    valid = hh < n_heads
    sel_gb = jnp.where(jnp.logical_and(valid, jnp.logical_and(jj < 4, rr == jj * n_heads + hh)), 1.0, 0.0).astype(BF16)
    sel_gc = jnp.where(jnp.logical_and(valid, jnp.logical_and(jnp.logical_and(jj >= 4, jj < 6),
                                                              rr == (jj - 4) * n_heads + hh)), 1.0, 0.0).astype(BF16)
    gb = jnp.where(lane < 2 * n_heads, g, beta)
    packed = _dot_sel_right(gb, sel_gb) + _dot_sel_right(gc, sel_gc)
    packed_t = packed.T
    for h in range(n_heads):
        sc_ref[0, h] = packed[:, SUBLANES * h:SUBLANES * (h + 1)]
        sct_ref[0, h] = packed_t[SUBLANES * h:SUBLANES * (h + 1), :]


def _gdn_prep(qkvb, ab, conv_w, alog_v, dtb_v, *, n_ctx_tiles, n_heads):
    nb, tt, w3 = qkvb.shape
    nt = tt // TILE
    rows8 = TILE // SUBLANES
    x8 = qkvb.reshape(nb, tt // SUBLANES, SUBLANES, w3)
    hd = lambda w: pl.BlockSpec((1, n_heads, TILE, w), lambda b, i: (b, 0, i, 0))
    return pl.pallas_call(
        functools.partial(_gdn_prep_kernel, n_ctx_tiles=n_ctx_tiles, n_tiles=nt, n_heads=n_heads),
        grid=(nb, nt),
        in_specs=[
            pl.BlockSpec((1, TILE, w3), lambda b, i: (b, i, 0)),
            pl.BlockSpec((1, 1, SUBLANES, w3), lambda b, i: (b, jnp.maximum(i * rows8 - 1, 0), 0, 0)),
            pl.BlockSpec((1, 1, SUBLANES, w3), lambda b, i: (b, jnp.minimum((i + 1) * rows8, nt * rows8 - 1), 0, 0)),
            pl.BlockSpec((1, TILE, LANES), lambda b, i: (b, i, 0)),
            pl.BlockSpec((CONV_W, w3), lambda b, i: (0, 0)),
            pl.BlockSpec((1, LANES), lambda b, i: (0, 0)),
            pl.BlockSpec((1, LANES), lambda b, i: (0, 0)),
        ],
        out_specs=(hd(HEAD_DIM), hd(HEAD_DIM), hd(HEAD_DIM), hd(SUBLANES),
                   pl.BlockSpec((1, n_heads, SUBLANES, TILE), lambda b, i: (b, 0, 0, i))),
        out_shape=(
            jax.ShapeDtypeStruct((nb, n_heads, tt, HEAD_DIM), F32),
            jax.ShapeDtypeStruct((nb, n_heads, tt, HEAD_DIM), F32),
            jax.ShapeDtypeStruct((nb, n_heads, tt, HEAD_DIM), F32),
            jax.ShapeDtypeStruct((nb, n_heads, tt, SUBLANES), F32),
            jax.ShapeDtypeStruct((nb, n_heads, SUBLANES, tt), F32),
        ),
        compiler_params=_cparams(("parallel", "parallel")),
        name="gdn_prep",
    )(qkvb, x8, x8, ab, conv_w, alog_v, dtb_v)


def _bmm(a, b):
    return jnp.einsum("nik,nkj->nij", a, b, preferred_element_type=F32)


def _bmm_nt(a, b):
    return jnp.einsum("nik,njk->nij", a, b, preferred_element_type=F32)


def _tri_inverse(lmat, ii, jj):
    cast = lambda z: z.astype(D_GI)
    blk = 16
    same = (ii // blk) == (jj // blk)
    dmat = jnp.where(same, lmat, 0.0)
    eye = jnp.where(ii == jj, 1.0, 0.0)
    db = cast(dmat)
    x = eye - dmat
    p = _bmm(db, db)
    n_sq = int(math.log2(blk)) - 1
    for step in range(n_sq):
        pb, xb = cast(p), cast(x)
        x = x + _bmm(pb, xb)
        if step < n_sq - 1:
            p = _bmm(pb, pb)
    while blk < lmat.shape[-1]:
        wider = (ii // (2 * blk)) == (jj // (2 * blk))
        off = jnp.where(jnp.logical_and(wider, jnp.logical_not(same)), lmat, 0.0)
        xb = cast(x)
        x = x - _bmm(xb, cast(_bmm(cast(off), xb)))
        same = wider
        blk *= 2
    return x


def _gdn_local_kernel(q_ref, k_ref, v_ref, sc_ref, sct_ref,
                      u_ref, w_ref, qh_ref, aqk_ref, ktt_ref, egl_ref):
    c = GDN_CHUNK
    cpt = TILE // c
    hps = q_ref.shape[1]
    n = hps * 2 * cpt

    def both(x):
        x4 = x.reshape(hps, cpt, c, x.shape[-1])
        return jnp.concatenate([x4, x4], axis=1).reshape(n, c, x.shape[-1])

    def per_dir(x, col):
        x4 = x.reshape(hps, cpt, c, x.shape[-1])
        return jnp.concatenate([x4[..., col:col + 1], x4[..., col + 1:col + 2]], axis=1).reshape(n, c, 1)

    q, k, v = both(q_ref[0]), both(k_ref[0]), both(v_ref[0])
    beta = per_dir(sc_ref[0], 2)
    gcc = per_dir(sc_ref[0], 4)
    sct = sct_ref[0]
    gcr = jnp.stack([sct[h, 4 + d:5 + d, ch * c:(ch + 1) * c]
                     for h in range(hps) for d in range(2) for ch in range(cpt)], axis=0)

    ii = lax.broadcasted_iota(jnp.int32, (n, c, c), 1)
    jj = lax.broadcasted_iota(jnp.int32, (n, c, c), 2)
    fwd = (lax.broadcasted_iota(jnp.int32, (n, c, c), 0) // cpt) % 2 == 0
    incl = jnp.logical_or(jnp.logical_and(fwd, ii >= jj), jnp.logical_and(jnp.logical_not(fwd), ii <= jj))
    strict = jnp.logical_and(incl, ii != jj)
    decay = jnp.where(incl, jnp.exp(jnp.where(incl, gcc - gcr, 0.0)), 0.0)
    kb = k * beta
    a = _bmm_nt(jnp.concatenate([kb, q], axis=1).astype(D_GA), k.astype(D_GA))
    lmat = jnp.where(strict, a[:, :c] * decay, 0.0)
    aqk = a[:, c:] * decay
    tinv = _tri_inverse(lmat, ii, jj).astype(D_GT)
    egc = jnp.exp(gcc)
    u = _bmm(tinv, (v * beta).astype(D_GT))
    w = _bmm(tinv, (kb * egc).astype(D_GT))
    fwd1 = (lax.broadcasted_iota(jnp.int32, (n, 1, 1), 0) // cpt) % 2 == 0
    glast = jnp.where(fwd1, gcc[:, c - 1:c, :], gcc[:, 0:1, :])
    ktail = (k * jnp.exp(glast - gcc)).astype(D_GDN)
    eye = jnp.where(ii == jj, 1.0, 0.0).astype(D_GDN)
    split = lambda z: z.reshape((hps, 2, cpt) + z.shape[1:])
    u_ref[0] = split(u)
    w_ref[0] = split(w.astype(D_GDN))
    qh_ref[0] = split((q * egc).astype(D_GDN))
    aqk_ref[0] = split(aqk.astype(D_GDN))
    ktt_ref[0] = split(_bmm_nt(eye, ktail).astype(D_GDN))
    egl_ref[0] = split(jnp.broadcast_to(jnp.exp(glast), (n, 1, HEAD_DIM)))


def _gdn_local(qn, kn, vv, sc, sct):
    nb, nh, tt, hd = qn.shape
    nt = tt // TILE
    nc = tt // GDN_CHUNK
    cpt = TILE // GDN_CHUNK
    hps = GDN_HEADS_PER_STEP if nh % GDN_HEADS_PER_STEP == 0 else 1
    tok = lambda w: pl.BlockSpec((1, hps, TILE, w), lambda b, h, i: (b, h, i, 0))
    chunked = lambda r, w: pl.BlockSpec((1, hps, 2, cpt, r, w), lambda b, h, i: (b, h, 0, i, 0, 0))
    shape = lambda r, w, dt: jax.ShapeDtypeStruct((nb, nh, 2, nc, r, w), dt)
    return pl.pallas_call(
        _gdn_local_kernel,
        grid=(nb, nh // hps, nt),
        in_specs=[tok(hd), tok(hd), tok(hd), tok(SUBLANES),
                  pl.BlockSpec((1, hps, SUBLANES, TILE), lambda b, h, i: (b, h, 0, i))],
        out_specs=(chunked(GDN_CHUNK, hd), chunked(GDN_CHUNK, hd), chunked(GDN_CHUNK, hd),
                   chunked(GDN_CHUNK, GDN_CHUNK), chunked(hd, GDN_CHUNK), chunked(1, hd)),
        out_shape=(
            shape(GDN_CHUNK, hd, F32),
            shape(GDN_CHUNK, hd, D_GDN),
            shape(GDN_CHUNK, hd, D_GDN),
            shape(GDN_CHUNK, GDN_CHUNK, D_GDN),
            shape(hd, GDN_CHUNK, D_GDN),
            shape(1, hd, F32),
        ),
        compiler_params=_cparams(("parallel", "parallel", "parallel")),
        name="gdn_local",
    )(qn, kn, vv, sc, sct)


def _gdn_scan_kernel(uf, ub, wf, wb, qf, qb, af, ab, kf, kb, ef, eb, of_ref, ob_ref, s_ref, *, cpt, n_heads):
    @pl.when(pl.program_id(1) == 0)
    def _():
        s_ref[...] = jnp.zeros_like(s_ref)

    s = s_ref[...]
    for t in range(cpt):
        tb = cpt - 1 - t
        pair = lambda f, b: jnp.concatenate([f[0, :, 0, t], b[0, :, 0, tb]], axis=0)
        sb = s.astype(D_GS)
        v_new = pair(uf, ub) - _bmm(pair(wf, wb), sb)
        vb = v_new.astype(D_GS)
        o = _bmm(pair(qf, qb), sb) + _bmm(pair(af, ab), vb)
        s = s * pair(ef, eb) + _bmm(pair(kf, kb), vb)
        of_ref[0, :, t] = o[:n_heads]
        ob_ref[0, :, tb] = o[n_heads:]
    s_ref[...] = s


def _gdn_scan(u, w, qh, aqk, ktt, egl, *, n_ctx_tiles):
    nb, nh, _, nc, c, hd = u.shape
    cpt = TILE // c
    nt = nc // cpt

    def bwd_tile(j):
        return jnp.where(j < n_ctx_tiles, n_ctx_tiles - 1 - j, nt - 1 + n_ctx_tiles - j)

    def specs(a):
        blk = (1, nh, 1, cpt) + a.shape[4:]
        return [pl.BlockSpec(blk, lambda b, j: (b, 0, 0, j, 0, 0)),
                pl.BlockSpec(blk, lambda b, j: (b, 0, 1, bwd_tile(j), 0, 0))]

    args, in_specs = [], []
    for a in (u, w, qh, aqk, ktt, egl):
        args += [a, a]
        in_specs += specs(a)
    oshape = jax.ShapeDtypeStruct((nb, nh, nc, c, hd), F32)
    return pl.pallas_call(
        functools.partial(_gdn_scan_kernel, cpt=cpt, n_heads=nh),
        grid=(nb, nt),
        in_specs=in_specs,
        out_specs=(pl.BlockSpec((1, nh, cpt, c, hd), lambda b, j: (b, 0, j, 0, 0)),
                   pl.BlockSpec((1, nh, cpt, c, hd), lambda b, j: (b, 0, bwd_tile(j), 0, 0))),
        out_shape=(oshape, oshape),
        scratch_shapes=[pltpu.VMEM((2 * nh, hd, hd), F32)],
        compiler_params=_cparams(("parallel", "arbitrary")),
        name="gdn_scan",
    )(*args)


def _outproj_kernel(x_ref, mod_ref, ya_ref, of_ref, ob_ref, gate_ref, gnw_ref, yc_ref, w_ref, o_ref,
                    *, a_w, b_w, n_heads):
    heads = []
    for h in range(n_heads):
        o = of_ref[0, h] + ob_ref[0, h]
        ms = jnp.mean(o * o, axis=-1, keepdims=True)
        heads.append(o * lax.rsqrt(ms + EPS) * gnw_ref[...])
    yb = (jnp.concatenate(heads, axis=1) * _silu(gate_ref[0])).astype(D_OUT)
    y = jnp.concatenate([ya_ref[0], yb, yc_ref[0]], axis=1)
    o_ref[0] = x_ref[0] + mod_ref[0][2:3] * _dot(y, w_ref[...])


def _outproj(xs, mods, ya, scan_f, scan_b, gate, gnw, yc, w, *, n_batch, n_ctx_tiles, n_heads):
    nb, tt, d = xs.shape
    nt = tt // TILE
    a_w, b_w, c_w = ya.shape[2], gate.shape[2], yc.shape[2]

    def mod_idx(b, i):
        return (jnp.where(i < n_ctx_tiles, n_batch, b), 0, 0)

    tok = lambda wd: pl.BlockSpec((1, TILE, wd), lambda b, i: (b, i, 0))
    heads = pl.BlockSpec((1, n_heads, TILE, HEAD_DIM), lambda b, i: (b, 0, i, 0))
    return pl.pallas_call(
        functools.partial(_outproj_kernel, a_w=a_w, b_w=b_w, n_heads=n_heads),
        grid=(nb, nt),
        in_specs=[
            tok(d),
            pl.BlockSpec((1, 6, d), mod_idx),
            tok(a_w),
            heads,
            heads,
            tok(b_w),
            pl.BlockSpec((1, HEAD_DIM), lambda b, i: (0, 0)),
            tok(c_w),
            pl.BlockSpec(w.shape, lambda b, i: (0, 0)),
        ],
        out_specs=tok(d),
        out_shape=jax.ShapeDtypeStruct((nb, tt, d), F32),
        compiler_params=_cparams(("parallel", "parallel")),
        name="outproj",
    )(xs, mods, ya, scan_f, scan_b, gate, gnw, yc, w)


def _ffn_kernel(x_ref, mod_ref, nw_ref, wi_ref, wo_ref, o_ref, *, d_ff):
    mod = mod_ref[0]
    x = x_ref[0]
    h = _norm_mod(x, nw_ref[...], mod[3:4], mod[4:5]).astype(D_FFN)
    gu = _dot(h, wi_ref[...])
    a = (_silu(gu[:, :d_ff]) * gu[:, d_ff:]).astype(D_FFN)
    o_ref[0] = x + mod[5:6] * _dot(a, wo_ref[...])


def _ffn(xs, mods, nw, wi, wo, *, n_batch, n_ctx_tiles):
    nb, tt, d = xs.shape
    nt = tt // TILE
    d_ff = wo.shape[0]

    def mod_idx(b, i):
        return (jnp.where(i < n_ctx_tiles, n_batch, b), 0, 0)

    tok = pl.BlockSpec((1, TILE, d), lambda b, i: (b, i, 0))
    return pl.pallas_call(
        functools.partial(_ffn_kernel, d_ff=d_ff),
        grid=(nb, nt),
        in_specs=[
            tok,
            pl.BlockSpec((1, 6, d), mod_idx),
            pl.BlockSpec((1, d), lambda b, i: (0, 0)),
            pl.BlockSpec(wi.shape, lambda b, i: (0, 0), pipeline_mode=pl.Buffered(1)),
            pl.BlockSpec(wo.shape, lambda b, i: (0, 0), pipeline_mode=pl.Buffered(1)),
        ],
        out_specs=tok,
        out_shape=jax.ShapeDtypeStruct((nb, tt, d), F32),
        compiler_params=_cparams(("parallel", "parallel")),
        name="ffn",
    )(xs, mods, nw, wi, wo)


def _final_kernel(x_ref, nw_ref, o_ref):
    x = x_ref[0]
    ms = jnp.mean(x * x, axis=-1, keepdims=True)
    o_ref[0] = x * lax.rsqrt(ms + EPS) * nw_ref[...]


def _final_norm(xs, nw, *, n_ctx_tiles):
    nb, tt, d = xs.shape
    n_lat = tt // TILE - n_ctx_tiles
    return pl.pallas_call(
        _final_kernel,
        grid=(nb, n_lat),
        in_specs=[pl.BlockSpec((1, TILE, d), lambda b, i: (b, i + n_ctx_tiles, 0)),
                  pl.BlockSpec((1, d), lambda b, i: (0, 0))],
        out_specs=pl.BlockSpec((1, TILE, d), lambda b, i: (b, i, 0)),
        out_shape=jax.ShapeDtypeStruct((nb, n_lat * TILE, d), F32),
        compiler_params=_cparams(("parallel", "parallel")),
        name="final_norm",
    )(xs, nw)


def _rope_tables(seq, ctx_len, n_rep):
    half = A_QK // 2
    inv_freq = 1.0 / (ROPE_THETA ** (jnp.arange(0, half, 2, dtype=F32) / half))
    t = jnp.arange(seq, dtype=jnp.int32)
    ang_r = (t // GRID_W).astype(F32)[:, None] * inv_freq
    ang_c = (t % GRID_W).astype(F32)[:, None] * inv_freq
    ang = jnp.concatenate([ang_r, ang_r, ang_c, ang_c], axis=-1)
    cos = jnp.concatenate([jnp.ones((ctx_len, A_QK), F32), jnp.cos(ang)], axis=0)
    sin = jnp.concatenate([jnp.zeros((ctx_len, A_QK), F32), jnp.sin(ang)], axis=0)
    return jnp.tile(cos, (1, n_rep)), jnp.tile(sin, (1, n_rep))


def kernel(x, c, ctx, c_ctx, w_mod, b_mod, norm1_w, norm2_w, w_in, w_out, lambda_q1, lambda_k1, lambda_q2,
           lambda_k2, diff_norm_w, conv_w, a_log, dt_bias, gdn_norm_w, na_bias, w_ffn_in, w_ffn_out,
           final_norm_w):
    nb, seq, d = x.shape
    ctx_len = ctx.shape[1]
    depth = w_mod.shape[0]
    b_heads = a_log.shape[-1]
    c_heads = na_bias.shape[1]
    b_w = b_heads * HEAD_DIM
    c_w = c_heads * HEAD_DIM
    a_w = w_out.shape[1] - b_w - c_w
    assert seq % TILE == 0 and ctx_len % TILE == 0 and seq % GRID_W == 0
    assert a_w % LANES == 0 and b_w % LANES == 0 and c_w % LANES == 0 and 4 * b_heads <= LANES
    n_ctx_tiles = ctx_len // TILE

    xs = jnp.concatenate([ctx, x], axis=1)

    n_rows = -(-(nb + 1) // SUBLANES) * SUBLANES
    cc = jnp.concatenate([c, c_ctx[None, :], jnp.zeros((n_rows - nb - 1, d), F32)], axis=0)
    mods_all = _modulation(cc, w_mod, b_mod).reshape(depth, n_rows, 6, d)

    cos, sin = _rope_tables(seq, ctx_len, a_w // A_QK)
    cost, sint = cos.T, sin.T

    sizes = (3 * a_w, 3 * b_w, b_w, 2 * b_heads, 2 * b_heads, 3 * c_w)
    offs = np.concatenate([[0], np.cumsum(sizes)])
    o_a, o_b, o_g, o_al, o_be, o_c = (int(v) for v in offs[:6])
    pad = LANES - 4 * b_heads

    for l in range(depth):
        wl = w_in[l]
        wm = jnp.concatenate([
            wl[:, o_a:o_a + a_w],
            wl[:, o_a + 2 * a_w:o_a + 3 * a_w],
            wl[:, o_b:o_b + 3 * b_w],
            wl[:, o_g:o_g + b_w],
            wl[:, o_c:o_c + c_w],
            wl[:, o_c + 2 * c_w:o_c + 3 * c_w],
            wl[:, o_al:o_al + 4 * b_heads],
            jnp.zeros((d, pad), F32),
        ], axis=1).astype(D_IN)
        wk = jnp.concatenate([wl[:, o_a + a_w:o_a + 2 * a_w], wl[:, o_c + c_w:o_c + 2 * c_w]], axis=1).T.astype(D_IN)
        mods = mods_all[l]
        lam_init = 0.8 - 0.6 * math.exp(-0.3 * l)

        qa, kat, va, qkvb, gate, ab, qc, kct, vc = _inproj(
            xs, mods, norm1_w[l][None, :], wm, wk, cos, sin, cost, sint,
            n_batch=nb, n_ctx_tiles=n_ctx_tiles, a_w=a_w, b_w=b_w, c_w=c_w)

        lam_vecs = jnp.stack([lambda_q1[l], lambda_k1[l], lambda_q2[l], lambda_k2[l]], axis=0).astype(F32)
        nw2 = jnp.tile(diff_norm_w[l], LANES // HEAD_DIM)[None, :]
        ya = _diffattn(lam_vecs, nw2, qa, kat, va, n_ctx_tiles=n_ctx_tiles, lam_init=lam_init)

        yc = _na(qc, kct, vc, _na_bias_table(na_bias[l], seq), n_ctx_tiles=n_ctx_tiles)

        alog_v = jnp.concatenate([a_log[l].reshape(-1), jnp.zeros((LANES - 2 * b_heads,), F32)])[None, :]
        dtb_v = jnp.concatenate([dt_bias[l].reshape(-1), jnp.zeros((LANES - 2 * b_heads,), F32)])[None, :]
        qn, kn, vv, sc, sct = _gdn_prep(qkvb, ab, conv_w[l], alog_v, dtb_v,
                                        n_ctx_tiles=n_ctx_tiles, n_heads=b_heads)
        u, w, qh, aqk, ktt, egl = _gdn_local(qn, kn, vv, sc, sct)
        scan_f, scan_b = _gdn_scan(u, w, qh, aqk, ktt, egl, n_ctx_tiles=n_ctx_tiles)
        scan_f = scan_f.reshape(nb, b_heads, ctx_len + seq, HEAD_DIM)
        scan_b = scan_b.reshape(nb, b_heads, ctx_len + seq, HEAD_DIM)

        xs = _outproj(xs, mods, ya, scan_f, scan_b, gate, gdn_norm_w[l][None, :], yc, w_out[l].astype(D_OUT),
                      n_batch=nb, n_ctx_tiles=n_ctx_tiles, n_heads=b_heads)
        xs = _ffn(xs, mods, norm2_w[l][None, :], w_ffn_in[l].astype(D_FFN), w_ffn_out[l].astype(D_FFN),
                  n_batch=nb, n_ctx_tiles=n_ctx_tiles)

    return _final_norm(xs, final_norm_w[None, :], n_ctx_tiles=n_ctx_tiles)
```

```python
import functools
import math

import jax
import jax.numpy as jnp
import numpy as np
from jax import lax
from jax.experimental import pallas as pl
from jax.experimental.pallas import tpu as pltpu

F32 = jnp.float32
BF16 = jnp.bfloat16
D_MOD = D_IN = D_OUT = D_FFN = D_ATT = D_NA = D_GDN = BF16
D_GA = D_GI = D_GT = D_GS = BF16

HEAD_DIM = 64
A_QK = HEAD_DIM // 2
GRID_W = 64
ROPE_THETA = 10000.0
CONV_W = 5
GDN_CHUNK = 64
WIN_ROWS = 8
WIN_COLS = 16
EPS = 1e-6

LANES = 128
SUBLANES = 8
TILE = 256
NA_WIN_TILES = 3
GDN_HEADS_PER_STEP = 6
NEG = -1e30
VMEM_LIMIT = 56 * 1024 * 1024


def _cparams(sem):
    return pltpu.CompilerParams(dimension_semantics=sem, vmem_limit_bytes=VMEM_LIMIT)


def _silu(x):
    return x * (1.0 / (1.0 + jnp.exp(-x)))


def _dot(a, b):
    return jnp.dot(a, b, preferred_element_type=F32)


def _dot_nt(a, b):
    return lax.dot_general(a, b, (((1,), (1,)), ((), ())), preferred_element_type=F32)


def _split3(x):
    hi = x.astype(BF16)
    r1 = x - hi.astype(F32)
    mid = r1.astype(BF16)
    lo = (r1 - mid.astype(F32)).astype(BF16)
    return hi, mid, lo


def _dot_sel_right(x, sel):
    hi, mid, lo = _split3(x)
    return _dot(hi, sel) + _dot(mid, sel) + _dot(lo, sel)


def _dot_sel_left(sel, x):
    hi, mid, lo = _split3(x)
    return _dot(sel, hi) + _dot(sel, mid) + _dot(sel, lo)


def _mod_kernel(c_ref, w_ref, b_ref, o_ref):
    s = _silu(c_ref[...])
    o_ref[0] = _dot(s.astype(D_MOD), w_ref[0].astype(D_MOD)) + b_ref[0]


def _modulation(cc, w_mod, b_mod):
    depth, d, n = w_mod.shape
    rows = cc.shape[0]
    tn = 1536
    assert n % tn == 0
    return pl.pallas_call(
        _mod_kernel,
        grid=(depth, n // tn),
        in_specs=[
            pl.BlockSpec((rows, d), lambda l, j: (0, 0)),
            pl.BlockSpec((1, d, tn), lambda l, j: (l, 0, j)),
            pl.BlockSpec((1, 1, tn), lambda l, j: (l, 0, j)),
        ],
        out_specs=pl.BlockSpec((1, rows, tn), lambda l, j: (l, 0, j)),
        out_shape=jax.ShapeDtypeStruct((depth, rows, n), F32),
        compiler_params=_cparams(("parallel", "parallel")),
        name="modulation",
    )(cc, w_mod, b_mod.reshape(depth, 1, n))


def _norm_mod(x, nw, shift, scale):
    ms = jnp.mean(x * x, axis=-1, keepdims=True)
    y = x * lax.rsqrt(ms + EPS) * nw
    return y * (1.0 + scale) + shift


def _rope_lanes(x, cos, sin):
    parts = []
    for j in range(x.shape[1] // LANES):
        xj = x[:, LANES * j:LANES * (j + 1)]
        nxt = pltpu.roll(xj, LANES - 8, axis=1)
        prv = pltpu.roll(xj, 8, axis=1)
        lane = lax.broadcasted_iota(jnp.int32, xj.shape, 1)
        parts.append(jnp.where((lane % 16) < 8, -nxt, prv))
    rot = jnp.concatenate(parts, axis=1)
    return x * cos + rot * sin


def _rope_rows(x, cos, sin):
    n = x.shape[0]
    nxt = pltpu.roll(x, n - 8, axis=0)
    prv = pltpu.roll(x, 8, axis=0)
    row = lax.broadcasted_iota(jnp.int32, x.shape, 0)
    rot = jnp.where((row % 16) < 8, -nxt, prv)
    return x * cos + rot * sin


def _inproj_kernel(x_ref, mod_ref, nw_ref, wm_ref, wk_ref, cos_ref, sin_ref, cost_ref, sint_ref,
                   qa_ref, kat_ref, va_ref, qkvb_ref, gate_ref, ab_ref, qc_ref, kct_ref, vc_ref,
                   *, a_w, b_w, c_w):
    mod = mod_ref[0]
    h = _norm_mod(x_ref[0], nw_ref[...], mod[0:1], mod[1:2]).astype(D_IN)
    acc = _dot(h, wm_ref[...])
    kt = _dot_nt(wk_ref[...], h)
    o = 0
    qa = _rope_lanes(acc[:, o:o + a_w], cos_ref[...], sin_ref[...])
    qa_ref[0] = (qa * (A_QK ** -0.5 * math.log2(math.e))).astype(D_ATT)
    o += a_w
    va_ref[0] = acc[:, o:o + a_w].astype(D_ATT)
    o += a_w
    qkvb_ref[0] = acc[:, o:o + 3 * b_w]
    o += 3 * b_w
    gate_ref[0] = acc[:, o:o + b_w]
    o += b_w
    qc_ref[0] = (acc[:, o:o + c_w] * (HEAD_DIM ** -0.5 * math.log2(math.e))).astype(D_NA)
    o += c_w
    vc_ref[0] = acc[:, o:o + c_w].astype(D_NA)
    o += c_w
    ab_ref[0] = acc[:, o:o + LANES]
    kat_ref[0, 0] = _rope_rows(kt[:a_w], cost_ref[...], sint_ref[...]).astype(D_ATT)
    kct_ref[0, 0] = kt[a_w:].astype(D_NA)


def _inproj(xs, mods, nw, wm, wk, cos, sin, cost, sint, *, n_batch, n_ctx_tiles, a_w, b_w, c_w):
    nb, tt, d = xs.shape
    nt = tt // TILE
    nm = wm.shape[1]
    ctx_row = n_batch

    def mod_idx(b, i):
        return (jnp.where(i < n_ctx_tiles, ctx_row, b), 0, 0)

    tok = lambda w: pl.BlockSpec((1, TILE, w), lambda b, i: (b, i, 0))
    tokt = lambda w: pl.BlockSpec((1, 1, w, TILE), lambda b, i: (b, i, 0, 0))
    out_shape = (
        jax.ShapeDtypeStruct((nb, tt, a_w), D_ATT),
        jax.ShapeDtypeStruct((nb, nt, a_w, TILE), D_ATT),
        jax.ShapeDtypeStruct((nb, tt, a_w), D_ATT),
        jax.ShapeDtypeStruct((nb, tt, 3 * b_w), F32),
        jax.ShapeDtypeStruct((nb, tt, b_w), F32),
        jax.ShapeDtypeStruct((nb, tt, LANES), F32),
        jax.ShapeDtypeStruct((nb, tt, c_w), D_NA),
        jax.ShapeDtypeStruct((nb, nt, c_w, TILE), D_NA),
        jax.ShapeDtypeStruct((nb, tt, c_w), D_NA),
    )
    return pl.pallas_call(
        functools.partial(_inproj_kernel, a_w=a_w, b_w=b_w, c_w=c_w),
        grid=(nb, nt),
        in_specs=[
            tok(d),
            pl.BlockSpec((1, 6, d), mod_idx),
            pl.BlockSpec((1, d), lambda b, i: (0, 0)),
            pl.BlockSpec((d, nm), lambda b, i: (0, 0)),
            pl.BlockSpec((a_w + c_w, d), lambda b, i: (0, 0)),
            pl.BlockSpec((TILE, a_w), lambda b, i: (i, 0)),
            pl.BlockSpec((TILE, a_w), lambda b, i: (i, 0)),
            pl.BlockSpec((a_w, TILE), lambda b, i: (0, i)),
            pl.BlockSpec((a_w, TILE), lambda b, i: (0, i)),
        ],
        out_specs=(tok(a_w), tokt(a_w), tok(a_w), tok(3 * b_w), tok(b_w), tok(LANES),
                   tok(c_w), tokt(c_w), tok(c_w)),
        out_shape=out_shape,
        compiler_params=_cparams(("parallel", "parallel")),
        name="inproj",
    )(xs, mods, nw, wm, wk, cos, sin, cost, sint)


def _diffattn_kernel(lam_ref, nw_ref, q_ref, kt_ref, v_ref, o_ref, mx_ref, acc_ref,
                     *, n_ctx_tiles, n_key_tiles, lam_init):
    i = pl.program_id(2)
    lv = lam_ref[...]
    lam = (jnp.exp(jnp.sum(lv[0:1] * lv[1:2], axis=-1, keepdims=True))
           - jnp.exp(jnp.sum(lv[2:3] * lv[3:4], axis=-1, keepdims=True)) + lam_init)
    q = q_ref[0]
    tq = q.shape[0]
    lane = lax.broadcasted_iota(jnp.int32, (tq, LANES), 1)
    lo = lane < HEAD_DIM
    zero = jnp.zeros_like(q)
    qm = [jnp.where((lane // A_QK) == m, q, zero) for m in range(LANES // A_QK)]
    n_maps = len(qm)
    n_lat = n_key_tiles - n_ctx_tiles
    group = max(u for u in (16, 8, 4, 2, 1) if n_lat % u == 0)
    ctx_tiles = list(range(n_ctx_tiles))
    lat_tiles = lambda g: [n_ctx_tiles + g * group + u for u in range(group)]

    mx_ref[...] = jnp.full(mx_ref.shape, NEG, F32)
    acc_ref[...] = jnp.zeros(acc_ref.shape, F32)
    lane_v = lax.broadcasted_iota(jnp.int32, (TILE, LANES), 1)

    def visit(tiles):
        ktiles = [kt_ref[0, j] for j in tiles]
        v_ext = []
        for j in tiles:
            start = j * TILE if isinstance(j, int) else pl.multiple_of(j * TILE, TILE)
            vtile = v_ref[0, pl.ds(start, TILE), :]
            ones = jnp.ones_like(vtile)
            v_ext.append((jnp.where(lane_v < HEAD_DIM, vtile, ones), jnp.where(lane_v < HEAD_DIM, ones, vtile)))
        scores = [[_dot(qm[m], ktile) for ktile in ktiles] for m in range(n_maps)]
        m_new = []
        for m in range(n_maps):
            top = scores[m][0]
            for s in scores[m][1:]:
                top = jnp.maximum(top, s)
            top = jnp.max(jnp.maximum(top[:, :LANES], top[:, LANES:]), axis=-1, keepdims=True)
            m_new.append(jnp.maximum(mx_ref[m], top))
        probs = [[jnp.exp2(s - m_new[m]).astype(D_ATT) for s in scores[m]] for m in range(n_maps)]
        for m in range(n_maps):
            acc = acc_ref[m] * jnp.exp2(mx_ref[m] - m_new[m])
            for p, ve in zip(probs[m], v_ext):
                acc = acc + _dot(p, ve[m // 2])
            mx_ref[m] = m_new[m]
            acc_ref[m] = acc

    @pl.when(i < n_ctx_tiles)
    def _():
        visit(ctx_tiles)

    @pl.when(i >= n_ctx_tiles)
    def _():
        visit(ctx_tiles + lat_tiles(0))
        lax.fori_loop(1, n_lat // group, lambda g, c: (visit(lat_tiles(g)), c)[1], 0)
    attn = []
    for m in range(n_maps):
        a = acc_ref[m]
        attn.append(a / pltpu.roll(a, HEAD_DIM, axis=1))
    o = jnp.where(lo, attn[0] - lam * attn[1], attn[2] - lam * attn[3])
    sq = o * o
    ms_lo = jnp.sum(jnp.where(lo, sq, 0.0), axis=-1, keepdims=True) * (1.0 / HEAD_DIM)
    ms_hi = jnp.sum(jnp.where(lo, 0.0, sq), axis=-1, keepdims=True) * (1.0 / HEAD_DIM)
    y = o * lax.rsqrt(jnp.where(lo, ms_lo, ms_hi) + EPS) * nw_ref[...]
    o_ref[0] = (y * (1.0 - lam_init)).astype(D_OUT)


def _diffattn(lam_vecs, nw2, qa, kat, va, *, n_ctx_tiles, lam_init):
    nb, tt, a_w = qa.shape
    nt = tt // TILE
    return pl.pallas_call(
        functools.partial(_diffattn_kernel, n_ctx_tiles=n_ctx_tiles, n_key_tiles=nt, lam_init=lam_init),
        grid=(nb, a_w // LANES, nt),
        in_specs=[
            pl.BlockSpec(lam_vecs.shape, lambda b, p, i: (0, 0)),
            pl.BlockSpec((1, LANES), lambda b, p, i: (0, 0)),
            pl.BlockSpec((1, TILE, LANES), lambda b, p, i: (b, i, p)),
            pl.BlockSpec((1, nt, LANES, TILE), lambda b, p, i: (b, 0, p, 0)),
            pl.BlockSpec((1, tt, LANES), lambda b, p, i: (b, 0, p)),
        ],
        out_specs=pl.BlockSpec((1, TILE, LANES), lambda b, p, i: (b, i, p)),
        out_shape=jax.ShapeDtypeStruct((nb, tt, a_w), D_OUT),
        scratch_shapes=[pltpu.VMEM((LANES // A_QK, TILE, 1), F32),
                        pltpu.VMEM((LANES // A_QK, TILE, LANES), F32)],
        compiler_params=_cparams(("parallel", "parallel", "parallel")),
        name="diffattn",
    )(lam_vecs, nw2, qa, kat, va)


def _na_kernel(q_ref, kt_ref, v_ref, bias_ref, o_ref, *, n_ctx_tiles, n_key_tiles):
    i = pl.program_id(2)
    n_lat = n_key_tiles - n_ctx_tiles
    w0 = n_ctx_tiles + jnp.clip(i - n_ctx_tiles - 1, 0, n_lat - NA_WIN_TILES)
    q = q_ref[0]
    tq = q.shape[0]
    lane = lax.broadcasted_iota(jnp.int32, (tq, LANES), 1)
    lo = lane < HEAD_DIM
    zero = jnp.zeros_like(q)
    qh = [jnp.where(lo, q, zero), jnp.where(lo, zero, q)]
    tiles = [w0 + c for c in range(NA_WIN_TILES)] + list(range(n_ctx_tiles))
    scores = []
    for hd in range(2):
        row = []
        for c, t in enumerate(tiles):
            s = _dot(qh[hd], kt_ref[0, t])
            if c < NA_WIN_TILES:
                s = s + bias_ref[hd, 0, :, TILE * c:TILE * (c + 1)]
            row.append(s)
        scores.append(row)
    tops = []
    for hd in range(2):
        top = scores[hd][0]
        for s in scores[hd][1:]:
            top = jnp.maximum(top, s)
        tops.append(jnp.max(jnp.maximum(top[:, :LANES], top[:, LANES:]), axis=-1, keepdims=True))
    probs = [[jnp.exp2(s - tops[hd]).astype(D_NA) for s in scores[hd]] for hd in range(2)]
    lane_v = lax.broadcasted_iota(jnp.int32, (TILE, LANES), 1)
    accs = [jnp.zeros((tq, LANES), F32), jnp.zeros((tq, LANES), F32)]
    for c, t in enumerate(tiles):
        start = t * TILE if isinstance(t, int) else pl.multiple_of(t * TILE, TILE)
        vt = v_ref[0, pl.ds(start, TILE), :]
        ones = jnp.ones_like(vt)
        v_ext = (jnp.where(lane_v < HEAD_DIM, vt, ones), jnp.where(lane_v < HEAD_DIM, ones, vt))
        for hd in range(2):
            accs[hd] = accs[hd] + _dot(probs[hd][c], v_ext[hd])
    outs = [a / pltpu.roll(a, HEAD_DIM, axis=1) for a in accs]
    o_ref[0] = jnp.where(lo, outs[0], outs[1]).astype(D_OUT)


def _na(qc, kct, vc, bias, *, n_ctx_tiles):
    nb, tt, c_w = qc.shape
    nt = tt // TILE
    n_lat = nt - n_ctx_tiles

    def bias_idx(b, p, i):
        il = i - n_ctx_tiles
        cls = jnp.where(il < 0, 3, jnp.where(il == 0, 0, jnp.where(il == n_lat - 1, 2, 1)))
        return (p, cls, 0, 0)

    return pl.pallas_call(
        functools.partial(_na_kernel, n_ctx_tiles=n_ctx_tiles, n_key_tiles=nt),
        grid=(nb, c_w // LANES, nt),
        in_specs=[
            pl.BlockSpec((1, TILE, LANES), lambda b, p, i: (b, i, p)),
            pl.BlockSpec((1, nt, LANES, TILE), lambda b, p, i: (b, 0, p, 0)),
            pl.BlockSpec((1, tt, LANES), lambda b, p, i: (b, 0, p)),
            pl.BlockSpec((2, 1, TILE, NA_WIN_TILES * TILE), bias_idx),
        ],
        out_specs=pl.BlockSpec((1, TILE, LANES), lambda b, p, i: (b, i, p)),
        out_shape=jax.ShapeDtypeStruct((nb, tt, c_w), D_OUT),
        compiler_params=_cparams(("parallel", "parallel", "parallel")),
        name="nbr_attn",
    )(qc, kct, vc, bias)


def _na_bias_table(rel_bias, seq):
    rows = seq // GRID_W
    rpt = TILE // GRID_W
    wrows = NA_WIN_TILES * rpt
    assert rows >= wrows and rows >= WIN_ROWS
    n_heads = rel_bias.shape[0]
    n_dr, n_dc = 2 * WIN_ROWS - 1, 2 * WIN_COLS - 1
    qcol, kcol = np.arange(GRID_W)[:, None], np.arange(GRID_W)[None, :]
    kc0 = np.clip(qcol - WIN_COLS // 2, 0, GRID_W - WIN_COLS)
    col_ok = (kcol >= kc0) & (kcol < kc0 + WIN_COLS)
    dcol = kcol - qcol + (WIN_COLS - 1)
    sel_c = ((dcol[None] == np.arange(n_dc)[:, None, None]) & col_ok[None]).astype(np.float32)
    sel_c = sel_c.reshape(n_dc, GRID_W * GRID_W)
    qr, kr = np.arange(rpt)[:, None], np.arange(wrows)[None, :]
    tables = []
    n_lat = rows // rpt
    for il in (0, min(1, n_lat - 1), n_lat - 1):
        r_abs = il * rpt + qr
        k_abs = int(np.clip(il - 1, 0, n_lat - NA_WIN_TILES)) * rpt + kr
        kr0 = np.clip(r_abs - WIN_ROWS // 2, 0, rows - WIN_ROWS)
        row_ok = (k_abs >= kr0) & (k_abs < kr0 + WIN_ROWS)
        drow = k_abs - r_abs + (WIN_ROWS - 1)
        sel_r = ((drow[:, :, None] == np.arange(n_dr)) & row_ok[:, :, None]).astype(np.float32)
        by_row = jnp.einsum("pd,hdc->hpc", sel_r.reshape(rpt * wrows, n_dr), rel_bias.astype(F32),
                            precision=lax.Precision.HIGHEST)
        vals = jnp.einsum("hpc,cx->hpx", by_row, sel_c, precision=lax.Precision.HIGHEST)
        vals = vals.reshape(n_heads, rpt, wrows, GRID_W, GRID_W).transpose(0, 1, 3, 2, 4)
        valid = (row_ok[:, None, :, None] & col_ok[None, :, None, :]).reshape(TILE, NA_WIN_TILES * TILE)
        vals = vals.reshape(n_heads, TILE, NA_WIN_TILES * TILE) * math.log2(math.e)
        tables.append(jnp.where(jnp.asarray(valid)[None], vals, NEG))
    tables.append(jnp.full_like(tables[0], NEG))
    return jnp.stack(tables, axis=1)


def _gdn_prep_kernel(x_ref, prev_ref, next_ref, ab_ref, cw_ref, alog_ref, dtb_ref,
                     q_ref, k_ref, v_ref, sc_ref, sct_ref, *, n_ctx_tiles, n_tiles, n_heads):
    i = pl.program_id(1)
    first = jnp.logical_or(i == 0, i == n_ctx_tiles)
    last = jnp.logical_or(i == n_ctx_tiles - 1, i == n_tiles - 1)
    prev = jnp.where(first, 0.0, prev_ref[0, 0])
    nxt = jnp.where(last, 0.0, next_ref[0, 0])
    ext = jnp.concatenate([prev, x_ref[0], nxt], axis=0)
    cw = cw_ref[...]
    y = None
    for j in range(CONV_W):
        o = SUBLANES - CONV_W // 2 + j
        term = ext[o:o + TILE] * cw[j:j + 1]
        y = term if y is None else y + term
    y = _silu(y)
    b_w = n_heads * HEAD_DIM
    for h in range(n_heads):
        qh = y[:, h * HEAD_DIM:(h + 1) * HEAD_DIM]
        kh = y[:, b_w + h * HEAD_DIM:b_w + (h + 1) * HEAD_DIM]
        vh = y[:, 2 * b_w + h * HEAD_DIM:2 * b_w + (h + 1) * HEAD_DIM]
        q_ref[0, h] = qh * lax.rsqrt(jnp.sum(qh * qh, axis=-1, keepdims=True) + EPS) * (HEAD_DIM ** -0.5)
        k_ref[0, h] = kh * lax.rsqrt(jnp.sum(kh * kh, axis=-1, keepdims=True) + EPS)
        v_ref[0, h] = vh

    ab = ab_ref[0]
    lane = lax.broadcasted_iota(jnp.int32, ab.shape, 1)
    z = ab + dtb_ref[...]
    softplus = jnp.maximum(z, 0.0) + jnp.log(1.0 + jnp.exp(-jnp.abs(z)))
    g = jnp.where(lane < 2 * n_heads, -jnp.exp(alog_ref[...]) * softplus, 0.0)
    beta = 1.0 / (1.0 + jnp.exp(-ab))
    r = lax.broadcasted_iota(jnp.int32, (TILE, TILE), 0)
    c = lax.broadcasted_iota(jnp.int32, (TILE, TILE), 1)
    same = (r // GDN_CHUNK) == (c // GDN_CHUNK)
    tri_f = jnp.where(jnp.logical_and(same, c <= r), 1.0, 0.0).astype(BF16)
    tri_b = jnp.where(jnp.logical_and(same, c >= r), 1.0, 0.0).astype(BF16)
    gc = jnp.where(lane < n_heads, _dot_sel_left(tri_f, g), _dot_sel_left(tri_b, g))
    rr = lax.broadcasted_iota(jnp.int32, (LANES, LANES), 0)
    cc = lax.broadcasted_iota(jnp.int32, (LANES, LANES), 1)
    hh, jj = cc // SUBLANES, cc % SUBLANES
    valid = hh < n_heads
    sel_gb = jnp.where(jnp.logical_and(valid, jnp.logical_and(jj < 4, rr == jj * n_heads + hh)), 1.0, 0.0).astype(BF16)
    sel_gc = jnp.where(jnp.logical_and(valid, jnp.logical_and(jnp.logical_and(jj >= 4, jj < 6),
                                                              rr == (jj - 4) * n_heads + hh)), 1.0, 0.0).astype(BF16)
    gb = jnp.where(lane < 2 * n_heads, g, beta)
    packed = _dot_sel_right(gb, sel_gb) + _dot_sel_right(gc, sel_gc)
    packed_t = packed.T
    for h in range(n_heads):
        sc_ref[0, h] = packed[:, SUBLANES * h:SUBLANES * (h + 1)]
        sct_ref[0, h] = packed_t[SUBLANES * h:SUBLANES * (h + 1), :]


def _gdn_prep(qkvb, ab, conv_w, alog_v, dtb_v, *, n_ctx_tiles, n_heads):
    nb, tt, w3 = qkvb.shape
    nt = tt // TILE
    rows8 = TILE // SUBLANES
    x8 = qkvb.reshape(nb, tt // SUBLANES, SUBLANES, w3)
    hd = lambda w: pl.BlockSpec((1, n_heads, TILE, w), lambda b, i: (b, 0, i, 0))
    return pl.pallas_call(
        functools.partial(_gdn_prep_kernel, n_ctx_tiles=n_ctx_tiles, n_tiles=nt, n_heads=n_heads),
        grid=(nb, nt),
        in_specs=[
            pl.BlockSpec((1, TILE, w3), lambda b, i: (b, i, 0)),
            pl.BlockSpec((1, 1, SUBLANES, w3), lambda b, i: (b, jnp.maximum(i * rows8 - 1, 0), 0, 0)),
            pl.BlockSpec((1, 1, SUBLANES, w3), lambda b, i: (b, jnp.minimum((i + 1) * rows8, nt * rows8 - 1), 0, 0)),
            pl.BlockSpec((1, TILE, LANES), lambda b, i: (b, i, 0)),
            pl.BlockSpec((CONV_W, w3), lambda b, i: (0, 0)),
            pl.BlockSpec((1, LANES), lambda b, i: (0, 0)),
            pl.BlockSpec((1, LANES), lambda b, i: (0, 0)),
        ],
        out_specs=(hd(HEAD_DIM), hd(HEAD_DIM), hd(HEAD_DIM), hd(SUBLANES),
                   pl.BlockSpec((1, n_heads, SUBLANES, TILE), lambda b, i: (b, 0, 0, i))),
        out_shape=(
            jax.ShapeDtypeStruct((nb, n_heads, tt, HEAD_DIM), F32),
            jax.ShapeDtypeStruct((nb, n_heads, tt, HEAD_DIM), F32),
            jax.ShapeDtypeStruct((nb, n_heads, tt, HEAD_DIM), F32),
            jax.ShapeDtypeStruct((nb, n_heads, tt, SUBLANES), F32),
            jax.ShapeDtypeStruct((nb, n_heads, SUBLANES, tt), F32),
        ),
        compiler_params=_cparams(("parallel", "parallel")),
        name="gdn_prep",
    )(qkvb, x8, x8, ab, conv_w, alog_v, dtb_v)


def _bmm(a, b):
    return jnp.einsum("nik,nkj->nij", a, b, preferred_element_type=F32)


def _bmm_nt(a, b):
    return jnp.einsum("nik,njk->nij", a, b, preferred_element_type=F32)


def _tri_inverse(lmat, ii, jj):
    cast = lambda z: z.astype(D_GI)
    blk = 16
    same = (ii // blk) == (jj // blk)
    dmat = jnp.where(same, lmat, 0.0)
    eye = jnp.where(ii == jj, 1.0, 0.0)
    db = cast(dmat)
    x = eye - dmat
    p = _bmm(db, db)
    n_sq = int(math.log2(blk)) - 1
    for step in range(n_sq):
        pb, xb = cast(p), cast(x)
        x = x + _bmm(pb, xb)
        if step < n_sq - 1:
            p = _bmm(pb, pb)
    while blk < lmat.shape[-1]:
        wider = (ii // (2 * blk)) == (jj // (2 * blk))
        off = jnp.where(jnp.logical_and(wider, jnp.logical_not(same)), lmat, 0.0)
        xb = cast(x)
        x = x - _bmm(xb, cast(_bmm(cast(off), xb)))
        same = wider
        blk *= 2
    return x


def _gdn_local_kernel(q_ref, k_ref, v_ref, sc_ref, sct_ref,
                      u_ref, w_ref, qh_ref, aqk_ref, ktt_ref, egl_ref):
    c = GDN_CHUNK
    cpt = TILE // c
    hps = q_ref.shape[1]
    n = hps * 2 * cpt

    def both(x):
        x4 = x.reshape(hps, cpt, c, x.shape[-1])
        return jnp.concatenate([x4, x4], axis=1).reshape(n, c, x.shape[-1])

    def per_dir(x, col):
        x4 = x.reshape(hps, cpt, c, x.shape[-1])
        return jnp.concatenate([x4[..., col:col + 1], x4[..., col + 1:col + 2]], axis=1).reshape(n, c, 1)

    q, k, v = both(q_ref[0]), both(k_ref[0]), both(v_ref[0])
    beta = per_dir(sc_ref[0], 2)
    gcc = per_dir(sc_ref[0], 4)
    sct = sct_ref[0]
    gcr = jnp.stack([sct[h, 4 + d:5 + d, ch * c:(ch + 1) * c]
                     for h in range(hps) for d in range(2) for ch in range(cpt)], axis=0)

    ii = lax.broadcasted_iota(jnp.int32, (n, c, c), 1)
    jj = lax.broadcasted_iota(jnp.int32, (n, c, c), 2)
    fwd = (lax.broadcasted_iota(jnp.int32, (n, c, c), 0) // cpt) % 2 == 0
    incl = jnp.logical_or(jnp.logical_and(fwd, ii >= jj), jnp.logical_and(jnp.logical_not(fwd), ii <= jj))
    strict = jnp.logical_and(incl, ii != jj)
    decay = jnp.where(incl, jnp.exp(jnp.where(incl, gcc - gcr, 0.0)), 0.0)
    kb = k * beta
    a = _bmm_nt(jnp.concatenate([kb, q], axis=1).astype(D_GA), k.astype(D_GA))
    lmat = jnp.where(strict, a[:, :c] * decay, 0.0)
    aqk = a[:, c:] * decay
    tinv = _tri_inverse(lmat, ii, jj).astype(D_GT)
    egc = jnp.exp(gcc)
    u = _bmm(tinv, (v * beta).astype(D_GT))
    w = _bmm(tinv, (kb * egc).astype(D_GT))
    fwd1 = (lax.broadcasted_iota(jnp.int32, (n, 1, 1), 0) // cpt) % 2 == 0
    glast = jnp.where(fwd1, gcc[:, c - 1:c, :], gcc[:, 0:1, :])
    ktail = (k * jnp.exp(glast - gcc)).astype(D_GDN)
    eye = jnp.where(ii == jj, 1.0, 0.0).astype(D_GDN)
    split = lambda z: z.reshape((hps, 2, cpt) + z.shape[1:])
    u_ref[0] = split(u)
    w_ref[0] = split(w.astype(D_GDN))
    qh_ref[0] = split((q * egc).astype(D_GDN))
    aqk_ref[0] = split(aqk.astype(D_GDN))
    ktt_ref[0] = split(_bmm_nt(eye, ktail).astype(D_GDN))
    egl_ref[0] = split(jnp.broadcast_to(jnp.exp(glast), (n, 1, HEAD_DIM)))


def _gdn_local(qn, kn, vv, sc, sct):
    nb, nh, tt, hd = qn.shape
    nt = tt // TILE
    nc = tt // GDN_CHUNK
    cpt = TILE // GDN_CHUNK
    hps = GDN_HEADS_PER_STEP if nh % GDN_HEADS_PER_STEP == 0 else 1
    tok = lambda w: pl.BlockSpec((1, hps, TILE, w), lambda b, h, i: (b, h, i, 0))
    chunked = lambda r, w: pl.BlockSpec((1, hps, 2, cpt, r, w), lambda b, h, i: (b, h, 0, i, 0, 0))
    shape = lambda r, w, dt: jax.ShapeDtypeStruct((nb, nh, 2, nc, r, w), dt)
    return pl.pallas_call(
        _gdn_local_kernel,
        grid=(nb, nh // hps, nt),
        in_specs=[tok(hd), tok(hd), tok(hd), tok(SUBLANES),
                  pl.BlockSpec((1, hps, SUBLANES, TILE), lambda b, h, i: (b, h, 0, i))],
        out_specs=(chunked(GDN_CHUNK, hd), chunked(GDN_CHUNK, hd), chunked(GDN_CHUNK, hd),
                   chunked(GDN_CHUNK, GDN_CHUNK), chunked(hd, GDN_CHUNK), chunked(1, hd)),
        out_shape=(
            shape(GDN_CHUNK, hd, F32),
            shape(GDN_CHUNK, hd, D_GDN),
            shape(GDN_CHUNK, hd, D_GDN),
            shape(GDN_CHUNK, GDN_CHUNK, D_GDN),
            shape(hd, GDN_CHUNK, D_GDN),
            shape(1, hd, F32),
        ),
        compiler_params=_cparams(("parallel", "parallel", "parallel")),
        name="gdn_local",
    )(qn, kn, vv, sc, sct)


def _gdn_scan_kernel(uf, ub, wf, wb, qf, qb, af, ab, kf, kb, ef, eb, of_ref, ob_ref, s_ref, *, cpt, n_heads):
    @pl.when(pl.program_id(1) == 0)
    def _():
        s_ref[...] = jnp.zeros_like(s_ref)

    s = s_ref[...]
    for t in range(cpt):
        tb = cpt - 1 - t
        pair = lambda f, b: jnp.concatenate([f[0, :, 0, t], b[0, :, 0, tb]], axis=0)
        sb = s.astype(D_GS)
        v_new = pair(uf, ub) - _bmm(pair(wf, wb), sb)
        vb = v_new.astype(D_GS)
        o = _bmm(pair(qf, qb), sb) + _bmm(pair(af, ab), vb)
        s = s * pair(ef, eb) + _bmm(pair(kf, kb), vb)
        of_ref[0, :, t] = o[:n_heads]
        ob_ref[0, :, tb] = o[n_heads:]
    s_ref[...] = s


def _gdn_scan(u, w, qh, aqk, ktt, egl, *, n_ctx_tiles):
    nb, nh, _, nc, c, hd = u.shape
    cpt = TILE // c
    nt = nc // cpt

    def bwd_tile(j):
        return jnp.where(j < n_ctx_tiles, n_ctx_tiles - 1 - j, nt - 1 + n_ctx_tiles - j)

    def specs(a):
        blk = (1, nh, 1, cpt) + a.shape[4:]
        return [pl.BlockSpec(blk, lambda b, j: (b, 0, 0, j, 0, 0)),
                pl.BlockSpec(blk, lambda b, j: (b, 0, 1, bwd_tile(j), 0, 0))]

    args, in_specs = [], []
    for a in (u, w, qh, aqk, ktt, egl):
        args += [a, a]
        in_specs += specs(a)
    oshape = jax.ShapeDtypeStruct((nb, nh, nc, c, hd), F32)
    return pl.pallas_call(
        functools.partial(_gdn_scan_kernel, cpt=cpt, n_heads=nh),
        grid=(nb, nt),
        in_specs=in_specs,
        out_specs=(pl.BlockSpec((1, nh, cpt, c, hd), lambda b, j: (b, 0, j, 0, 0)),
                   pl.BlockSpec((1, nh, cpt, c, hd), lambda b, j: (b, 0, bwd_tile(j), 0, 0))),
        out_shape=(oshape, oshape),
        scratch_shapes=[pltpu.VMEM((2 * nh, hd, hd), F32)],
        compiler_params=_cparams(("parallel", "arbitrary")),
        name="gdn_scan",
    )(*args)


def _outproj_kernel(x_ref, mod_ref, ya_ref, of_ref, ob_ref, gate_ref, gnw_ref, yc_ref, w_ref, o_ref,
                    *, a_w, b_w, n_heads):
    heads = []
    for h in range(n_heads):
        o = of_ref[0, h] + ob_ref[0, h]
        ms = jnp.mean(o * o, axis=-1, keepdims=True)
        heads.append(o * lax.rsqrt(ms + EPS) * gnw_ref[...])
    yb = (jnp.concatenate(heads, axis=1) * _silu(gate_ref[0])).astype(D_OUT)
    y = jnp.concatenate([ya_ref[0], yb, yc_ref[0]], axis=1)
    o_ref[0] = x_ref[0] + mod_ref[0][2:3] * _dot(y, w_ref[...])


def _outproj(xs, mods, ya, scan_f, scan_b, gate, gnw, yc, w, *, n_batch, n_ctx_tiles, n_heads):
    nb, tt, d = xs.shape
    nt = tt // TILE
    a_w, b_w, c_w = ya.shape[2], gate.shape[2], yc.shape[2]

    def mod_idx(b, i):
        return (jnp.where(i < n_ctx_tiles, n_batch, b), 0, 0)

    tok = lambda wd: pl.BlockSpec((1, TILE, wd), lambda b, i: (b, i, 0))
    heads = pl.BlockSpec((1, n_heads, TILE, HEAD_DIM), lambda b, i: (b, 0, i, 0))
    return pl.pallas_call(
        functools.partial(_outproj_kernel, a_w=a_w, b_w=b_w, n_heads=n_heads),
        grid=(nb, nt),
        in_specs=[
            tok(d),
            pl.BlockSpec((1, 6, d), mod_idx),
            tok(a_w),
            heads,
            heads,
            tok(b_w),
            pl.BlockSpec((1, HEAD_DIM), lambda b, i: (0, 0)),
            tok(c_w),
            pl.BlockSpec(w.shape, lambda b, i: (0, 0)),
        ],
        out_specs=tok(d),
        out_shape=jax.ShapeDtypeStruct((nb, tt, d), F32),
        compiler_params=_cparams(("parallel", "parallel")),
        name="outproj",
    )(xs, mods, ya, scan_f, scan_b, gate, gnw, yc, w)


def _ffn_kernel(x_ref, mod_ref, nw_ref, wi_ref, wo_ref, o_ref, *, d_ff):
    mod = mod_ref[0]
    x = x_ref[0]
    h = _norm_mod(x, nw_ref[...], mod[3:4], mod[4:5]).astype(D_FFN)
    gu = _dot(h, wi_ref[...])
    a = (_silu(gu[:, :d_ff]) * gu[:, d_ff:]).astype(D_FFN)
    o_ref[0] = x + mod[5:6] * _dot(a, wo_ref[...])


def _ffn(xs, mods, nw, wi, wo, *, n_batch, n_ctx_tiles):
    nb, tt, d = xs.shape
    nt = tt // TILE
    d_ff = wo.shape[0]

    def mod_idx(b, i):
        return (jnp.where(i < n_ctx_tiles, n_batch, b), 0, 0)

    tok = pl.BlockSpec((1, TILE, d), lambda b, i: (b, i, 0))
    return pl.pallas_call(
        functools.partial(_ffn_kernel, d_ff=d_ff),
        grid=(nb, nt),
        in_specs=[
            tok,
            pl.BlockSpec((1, 6, d), mod_idx),
            pl.BlockSpec((1, d), lambda b, i: (0, 0)),
            pl.BlockSpec(wi.shape, lambda b, i: (0, 0), pipeline_mode=pl.Buffered(1)),
            pl.BlockSpec(wo.shape, lambda b, i: (0, 0), pipeline_mode=pl.Buffered(1)),
        ],
        out_specs=tok,
        out_shape=jax.ShapeDtypeStruct((nb, tt, d), F32),
        compiler_params=_cparams(("parallel", "parallel")),
        name="ffn",
    )(xs, mods, nw, wi, wo)


def _final_kernel(x_ref, nw_ref, o_ref):
    x = x_ref[0]
    ms = jnp.mean(x * x, axis=-1, keepdims=True)
    o_ref[0] = x * lax.rsqrt(ms + EPS) * nw_ref[...]


def _final_norm(xs, nw, *, n_ctx_tiles):
    nb, tt, d = xs.shape
    n_lat = tt // TILE - n_ctx_tiles
    return pl.pallas_call(
        _final_kernel,
        grid=(nb, n_lat),
        in_specs=[pl.BlockSpec((1, TILE, d), lambda b, i: (b, i + n_ctx_tiles, 0)),
                  pl.BlockSpec((1, d), lambda b, i: (0, 0))],
        out_specs=pl.BlockSpec((1, TILE, d), lambda b, i: (b, i, 0)),
        out_shape=jax.ShapeDtypeStruct((nb, n_lat * TILE, d), F32),
        compiler_params=_cparams(("parallel", "parallel")),
        name="final_norm",
    )(xs, nw)


def _rope_tables(seq, ctx_len, n_rep):
    half = A_QK // 2
    inv_freq = 1.0 / (ROPE_THETA ** (jnp.arange(0, half, 2, dtype=F32) / half))
    t = jnp.arange(seq, dtype=jnp.int32)
    ang_r = (t // GRID_W).astype(F32)[:, None] * inv_freq
    ang_c = (t % GRID_W).astype(F32)[:, None] * inv_freq
    ang = jnp.concatenate([ang_r, ang_r, ang_c, ang_c], axis=-1)
    cos = jnp.concatenate([jnp.ones((ctx_len, A_QK), F32), jnp.cos(ang)], axis=0)
    sin = jnp.concatenate([jnp.zeros((ctx_len, A_QK), F32), jnp.sin(ang)], axis=0)
    return jnp.tile(cos, (1, n_rep)), jnp.tile(sin, (1, n_rep))


def kernel(x, c, ctx, c_ctx, w_mod, b_mod, norm1_w, norm2_w, w_in, w_out, lambda_q1, lambda_k1, lambda_q2,
           lambda_k2, diff_norm_w, conv_w, a_log, dt_bias, gdn_norm_w, na_bias, w_ffn_in, w_ffn_out,
           final_norm_w):
    nb, seq, d = x.shape
    ctx_len = ctx.shape[1]
    depth = w_mod.shape[0]
    b_heads = a_log.shape[-1]
    c_heads = na_bias.shape[1]
    b_w = b_heads * HEAD_DIM
    c_w = c_heads * HEAD_DIM
    a_w = w_out.shape[1] - b_w - c_w
    assert seq % TILE == 0 and ctx_len % TILE == 0 and seq % GRID_W == 0
    assert a_w % LANES == 0 and b_w % LANES == 0 and c_w % LANES == 0 and 4 * b_heads <= LANES
    n_ctx_tiles = ctx_len // TILE

    xs = jnp.concatenate([ctx, x], axis=1)

    n_rows = -(-(nb + 1) // SUBLANES) * SUBLANES
    cc = jnp.concatenate([c, c_ctx[None, :], jnp.zeros((n_rows - nb - 1, d), F32)], axis=0)
    mods_all = _modulation(cc, w_mod, b_mod).reshape(depth, n_rows, 6, d)

    cos, sin = _rope_tables(seq, ctx_len, a_w // A_QK)
    cost, sint = cos.T, sin.T

    sizes = (3 * a_w, 3 * b_w, b_w, 2 * b_heads, 2 * b_heads, 3 * c_w)
    offs = np.concatenate([[0], np.cumsum(sizes)])
    o_a, o_b, o_g, o_al, o_be, o_c = (int(v) for v in offs[:6])
    pad = LANES - 4 * b_heads

    for l in range(depth):
        wl = w_in[l]
        wm = jnp.concatenate([
            wl[:, o_a:o_a + a_w],
            wl[:, o_a + 2 * a_w:o_a + 3 * a_w],
            wl[:, o_b:o_b + 3 * b_w],
            wl[:, o_g:o_g + b_w],
            wl[:, o_c:o_c + c_w],
            wl[:, o_c + 2 * c_w:o_c + 3 * c_w],
            wl[:, o_al:o_al + 4 * b_heads],
            jnp.zeros((d, pad), F32),
        ], axis=1).astype(D_IN)
        wk = jnp.concatenate([wl[:, o_a + a_w:o_a + 2 * a_w], wl[:, o_c + c_w:o_c + 2 * c_w]], axis=1).T.astype(D_IN)
        mods = mods_all[l]
        lam_init = 0.8 - 0.6 * math.exp(-0.3 * l)

        qa, kat, va, qkvb, gate, ab, qc, kct, vc = _inproj(
            xs, mods, norm1_w[l][None, :], wm, wk, cos, sin, cost, sint,
            n_batch=nb, n_ctx_tiles=n_ctx_tiles, a_w=a_w, b_w=b_w, c_w=c_w)

        lam_vecs = jnp.stack([lambda_q1[l], lambda_k1[l], lambda_q2[l], lambda_k2[l]], axis=0).astype(F32)
        nw2 = jnp.tile(diff_norm_w[l], LANES // HEAD_DIM)[None, :]
        ya = _diffattn(lam_vecs, nw2, qa, kat, va, n_ctx_tiles=n_ctx_tiles, lam_init=lam_init)

        yc = _na(qc, kct, vc, _na_bias_table(na_bias[l], seq), n_ctx_tiles=n_ctx_tiles)

        alog_v = jnp.concatenate([a_log[l].reshape(-1), jnp.zeros((LANES - 2 * b_heads,), F32)])[None, :]
        dtb_v = jnp.concatenate([dt_bias[l].reshape(-1), jnp.zeros((LANES - 2 * b_heads,), F32)])[None, :]
        qn, kn, vv, sc, sct = _gdn_prep(qkvb, ab, conv_w[l], alog_v, dtb_v,
                                        n_ctx_tiles=n_ctx_tiles, n_heads=b_heads)
        u, w, qh, aqk, ktt, egl = _gdn_local(qn, kn, vv, sc, sct)
        scan_f, scan_b = _gdn_scan(u, w, qh, aqk, ktt, egl, n_ctx_tiles=n_ctx_tiles)
        scan_f = scan_f.reshape(nb, b_heads, ctx_len + seq, HEAD_DIM)
        scan_b = scan_b.reshape(nb, b_heads, ctx_len + seq, HEAD_DIM)

        xs = _outproj(xs, mods, ya, scan_f, scan_b, gate, gdn_norm_w[l][None, :], yc, w_out[l].astype(D_OUT),
                      n_batch=nb, n_ctx_tiles=n_ctx_tiles, n_heads=b_heads)
        xs = _ffn(xs, mods, norm2_w[l][None, :], w_ffn_in[l].astype(D_FFN), w_ffn_out[l].astype(D_FFN),
                  n_batch=nb, n_ctx_tiles=n_ctx_tiles)

    return _final_norm(xs, final_norm_w[None, :], n_ctx_tiles=n_ctx_tiles)
```

```python
import functools
import math

import jax
import jax.numpy as jnp
import numpy as np
from jax import lax
from jax.experimental import pallas as pl
from jax.experimental.pallas import tpu as pltpu

F32 = jnp.float32
BF16 = jnp.bfloat16
D_MOD = D_IN = D_OUT = D_FFN = D_ATT = D_NA = D_GDN = BF16
D_GA = D_GI = D_GT = D_GS = BF16

HEAD_DIM = 64
A_QK = HEAD_DIM // 2
GRID_W = 64
ROPE_THETA = 10000.0
CONV_W = 5
GDN_CHUNK = 64
WIN_ROWS = 8
WIN_COLS = 16
EPS = 1e-6

LANES = 128
SUBLANES = 8
TILE = 256
NA_WIN_TILES = 3
NEG = -1e30
VMEM_LIMIT = 56 * 1024 * 1024


def _cparams(sem):
    return pltpu.CompilerParams(dimension_semantics=sem, vmem_limit_bytes=VMEM_LIMIT)


def _silu(x):
    return x * (1.0 / (1.0 + jnp.exp(-x)))


def _dot(a, b):
    return jnp.dot(a, b, preferred_element_type=F32)


def _dot_nt(a, b):
    return lax.dot_general(a, b, (((1,), (1,)), ((), ())), preferred_element_type=F32)


def _split3(x):
    hi = x.astype(BF16)
    r1 = x - hi.astype(F32)
    mid = r1.astype(BF16)
    lo = (r1 - mid.astype(F32)).astype(BF16)
    return hi, mid, lo


def _dot_sel_right(x, sel):
    hi, mid, lo = _split3(x)
    return _dot(hi, sel) + _dot(mid, sel) + _dot(lo, sel)


def _dot_sel_left(sel, x):
    hi, mid, lo = _split3(x)
    return _dot(sel, hi) + _dot(sel, mid) + _dot(sel, lo)


def _mod_kernel(c_ref, w_ref, b_ref, o_ref):
    s = _silu(c_ref[...])
    o_ref[0] = _dot(s.astype(D_MOD), w_ref[0].astype(D_MOD)) + b_ref[0]


def _modulation(cc, w_mod, b_mod):
    depth, d, n = w_mod.shape
    rows = cc.shape[0]
    tn = 1536
    assert n % tn == 0
    return pl.pallas_call(
        _mod_kernel,
        grid=(depth, n // tn),
        in_specs=[
            pl.BlockSpec((rows, d), lambda l, j: (0, 0)),
            pl.BlockSpec((1, d, tn), lambda l, j: (l, 0, j)),
            pl.BlockSpec((1, 1, tn), lambda l, j: (l, 0, j)),
        ],
        out_specs=pl.BlockSpec((1, rows, tn), lambda l, j: (l, 0, j)),
        out_shape=jax.ShapeDtypeStruct((depth, rows, n), F32),
        compiler_params=_cparams(("parallel", "parallel")),
        name="modulation",
    )(cc, w_mod, b_mod.reshape(depth, 1, n))


def _norm_mod(x, nw, shift, scale):
    ms = jnp.mean(x * x, axis=-1, keepdims=True)
    y = x * lax.rsqrt(ms + EPS) * nw
    return y * (1.0 + scale) + shift


def _rope_lanes(x, cos, sin):
    parts = []
    for j in range(x.shape[1] // LANES):
        xj = x[:, LANES * j:LANES * (j + 1)]
        nxt = pltpu.roll(xj, LANES - 8, axis=1)
        prv = pltpu.roll(xj, 8, axis=1)
        lane = lax.broadcasted_iota(jnp.int32, xj.shape, 1)
        parts.append(jnp.where((lane % 16) < 8, -nxt, prv))
    rot = jnp.concatenate(parts, axis=1)
    return x * cos + rot * sin


def _rope_rows(x, cos, sin):
    n = x.shape[0]
    nxt = pltpu.roll(x, n - 8, axis=0)
    prv = pltpu.roll(x, 8, axis=0)
    row = lax.broadcasted_iota(jnp.int32, x.shape, 0)
    rot = jnp.where((row % 16) < 8, -nxt, prv)
    return x * cos + rot * sin


def _inproj_kernel(x_ref, mod_ref, nw_ref, wm_ref, wk_ref, cos_ref, sin_ref, cost_ref, sint_ref,
                   qa_ref, kat_ref, va_ref, qkvb_ref, gate_ref, ab_ref, qc_ref, kct_ref, vc_ref,
                   *, a_w, b_w, c_w):
    mod = mod_ref[0]
    h = _norm_mod(x_ref[0], nw_ref[...], mod[0:1], mod[1:2]).astype(D_IN)
    acc = _dot(h, wm_ref[...])
    kt = _dot_nt(wk_ref[...], h)
    o = 0
    qa = _rope_lanes(acc[:, o:o + a_w], cos_ref[...], sin_ref[...])
    qa_ref[0] = (qa * (A_QK ** -0.5 * math.log2(math.e))).astype(D_ATT)
    o += a_w
    va_ref[0] = acc[:, o:o + a_w].astype(D_ATT)
    o += a_w
    qkvb_ref[0] = acc[:, o:o + 3 * b_w]
    o += 3 * b_w
    gate_ref[0] = acc[:, o:o + b_w]
    o += b_w
    qc_ref[0] = (acc[:, o:o + c_w] * (HEAD_DIM ** -0.5 * math.log2(math.e))).astype(D_NA)
    o += c_w
    vc_ref[0] = acc[:, o:o + c_w].astype(D_NA)
    o += c_w
    ab_ref[0] = acc[:, o:o + LANES]
    kat_ref[0, 0] = _rope_rows(kt[:a_w], cost_ref[...], sint_ref[...]).astype(D_ATT)
    kct_ref[0, 0] = kt[a_w:].astype(D_NA)


def _inproj(xs, mods, nw, wm, wk, cos, sin, cost, sint, *, n_batch, n_ctx_tiles, a_w, b_w, c_w):
    nb, tt, d = xs.shape
    nt = tt // TILE
    nm = wm.shape[1]
    ctx_row = n_batch

    def mod_idx(b, i):
        return (jnp.where(i < n_ctx_tiles, ctx_row, b), 0, 0)

    tok = lambda w: pl.BlockSpec((1, TILE, w), lambda b, i: (b, i, 0))
    tokt = lambda w: pl.BlockSpec((1, 1, w, TILE), lambda b, i: (b, i, 0, 0))
    out_shape = (
        jax.ShapeDtypeStruct((nb, tt, a_w), D_ATT),
        jax.ShapeDtypeStruct((nb, nt, a_w, TILE), D_ATT),
        jax.ShapeDtypeStruct((nb, tt, a_w), D_ATT),
        jax.ShapeDtypeStruct((nb, tt, 3 * b_w), F32),
        jax.ShapeDtypeStruct((nb, tt, b_w), F32),
        jax.ShapeDtypeStruct((nb, tt, LANES), F32),
        jax.ShapeDtypeStruct((nb, tt, c_w), D_NA),
        jax.ShapeDtypeStruct((nb, nt, c_w, TILE), D_NA),
        jax.ShapeDtypeStruct((nb, tt, c_w), D_NA),
    )
    return pl.pallas_call(
        functools.partial(_inproj_kernel, a_w=a_w, b_w=b_w, c_w=c_w),
        grid=(nb, nt),
        in_specs=[
            tok(d),
            pl.BlockSpec((1, 6, d), mod_idx),
            pl.BlockSpec((1, d), lambda b, i: (0, 0)),
            pl.BlockSpec((d, nm), lambda b, i: (0, 0)),
            pl.BlockSpec((a_w + c_w, d), lambda b, i: (0, 0)),
            pl.BlockSpec((TILE, a_w), lambda b, i: (i, 0)),
            pl.BlockSpec((TILE, a_w), lambda b, i: (i, 0)),
            pl.BlockSpec((a_w, TILE), lambda b, i: (0, i)),
            pl.BlockSpec((a_w, TILE), lambda b, i: (0, i)),
        ],
        out_specs=(tok(a_w), tokt(a_w), tok(a_w), tok(3 * b_w), tok(b_w), tok(LANES),
                   tok(c_w), tokt(c_w), tok(c_w)),
        out_shape=out_shape,
        compiler_params=_cparams(("parallel", "parallel")),
        name="inproj",
    )(xs, mods, nw, wm, wk, cos, sin, cost, sint)


def _diffattn_kernel(lam_ref, nw_ref, q_ref, kt_ref, v_ref, o_ref, mx_ref, acc_ref,
                     *, n_ctx_tiles, n_key_tiles, lam_init):
    i = pl.program_id(2)
    lv = lam_ref[...]
    lam = (jnp.exp(jnp.sum(lv[0:1] * lv[1:2], axis=-1, keepdims=True))
           - jnp.exp(jnp.sum(lv[2:3] * lv[3:4], axis=-1, keepdims=True)) + lam_init)
    q = q_ref[0]
    tq = q.shape[0]
    lane = lax.broadcasted_iota(jnp.int32, (tq, LANES), 1)
    lo = lane < HEAD_DIM
    zero = jnp.zeros_like(q)
    qm = [jnp.where((lane // A_QK) == m, q, zero) for m in range(LANES // A_QK)]
    n_maps = len(qm)
    n_lat = n_key_tiles - n_ctx_tiles
    group = max(u for u in (16, 8, 4, 2, 1) if n_lat % u == 0)
    ctx_tiles = list(range(n_ctx_tiles))
    lat_tiles = lambda g: [n_ctx_tiles + g * group + u for u in range(group)]

    mx_ref[...] = jnp.full(mx_ref.shape, NEG, F32)
    acc_ref[...] = jnp.zeros(acc_ref.shape, F32)
    lane_v = lax.broadcasted_iota(jnp.int32, (TILE, LANES), 1)

    def visit(tiles):
        ktiles = [kt_ref[0, j] for j in tiles]
        v_ext = []
        for j in tiles:
            start = j * TILE if isinstance(j, int) else pl.multiple_of(j * TILE, TILE)
            vtile = v_ref[0, pl.ds(start, TILE), :]
            ones = jnp.ones_like(vtile)
            v_ext.append((jnp.where(lane_v < HEAD_DIM, vtile, ones), jnp.where(lane_v < HEAD_DIM, ones, vtile)))
        scores = [[_dot(qm[m], ktile) for ktile in ktiles] for m in range(n_maps)]
        m_new = []
        for m in range(n_maps):
            top = scores[m][0]
            for s in scores[m][1:]:
                top = jnp.maximum(top, s)
            top = jnp.max(jnp.maximum(top[:, :LANES], top[:, LANES:]), axis=-1, keepdims=True)
            m_new.append(jnp.maximum(mx_ref[m], top))
        probs = [[jnp.exp2(s - m_new[m]).astype(D_ATT) for s in scores[m]] for m in range(n_maps)]
        for m in range(n_maps):
            acc = acc_ref[m] * jnp.exp2(mx_ref[m] - m_new[m])
            for p, ve in zip(probs[m], v_ext):
                acc = acc + _dot(p, ve[m // 2])
            mx_ref[m] = m_new[m]
            acc_ref[m] = acc

    @pl.when(i < n_ctx_tiles)
    def _():
        visit(ctx_tiles)

    @pl.when(i >= n_ctx_tiles)
    def _():
        visit(ctx_tiles + lat_tiles(0))
        lax.fori_loop(1, n_lat // group, lambda g, c: (visit(lat_tiles(g)), c)[1], 0)
    attn = []
    for m in range(n_maps):
        a = acc_ref[m]
        attn.append(a / pltpu.roll(a, HEAD_DIM, axis=1))
    o = jnp.where(lo, attn[0] - lam * attn[1], attn[2] - lam * attn[3])
    sq = o * o
    ms_lo = jnp.sum(jnp.where(lo, sq, 0.0), axis=-1, keepdims=True) * (1.0 / HEAD_DIM)
    ms_hi = jnp.sum(jnp.where(lo, 0.0, sq), axis=-1, keepdims=True) * (1.0 / HEAD_DIM)
    y = o * lax.rsqrt(jnp.where(lo, ms_lo, ms_hi) + EPS) * nw_ref[...]
    o_ref[0] = (y * (1.0 - lam_init)).astype(D_OUT)


def _diffattn(lam_vecs, nw2, qa, kat, va, *, n_ctx_tiles, lam_init):
    nb, tt, a_w = qa.shape
    nt = tt // TILE
    return pl.pallas_call(
        functools.partial(_diffattn_kernel, n_ctx_tiles=n_ctx_tiles, n_key_tiles=nt, lam_init=lam_init),
        grid=(nb, a_w // LANES, nt),
        in_specs=[
            pl.BlockSpec(lam_vecs.shape, lambda b, p, i: (0, 0)),
            pl.BlockSpec((1, LANES), lambda b, p, i: (0, 0)),
            pl.BlockSpec((1, TILE, LANES), lambda b, p, i: (b, i, p)),
            pl.BlockSpec((1, nt, LANES, TILE), lambda b, p, i: (b, 0, p, 0)),
            pl.BlockSpec((1, tt, LANES), lambda b, p, i: (b, 0, p)),
        ],
        out_specs=pl.BlockSpec((1, TILE, LANES), lambda b, p, i: (b, i, p)),
        out_shape=jax.ShapeDtypeStruct((nb, tt, a_w), D_OUT),
        scratch_shapes=[pltpu.VMEM((LANES // A_QK, TILE, 1), F32),
                        pltpu.VMEM((LANES // A_QK, TILE, LANES), F32)],
        compiler_params=_cparams(("parallel", "parallel", "parallel")),
        name="diffattn",
    )(lam_vecs, nw2, qa, kat, va)


def _na_kernel(q_ref, kt_ref, v_ref, bias_ref, o_ref, *, n_ctx_tiles, n_key_tiles):
    i = pl.program_id(2)
    n_lat = n_key_tiles - n_ctx_tiles
    w0 = n_ctx_tiles + jnp.clip(i - n_ctx_tiles - 1, 0, n_lat - NA_WIN_TILES)
    q = q_ref[0]
    tq = q.shape[0]
    lane = lax.broadcasted_iota(jnp.int32, (tq, LANES), 1)
    lo = lane < HEAD_DIM
    zero = jnp.zeros_like(q)
    qh = [jnp.where(lo, q, zero), jnp.where(lo, zero, q)]
    tiles = [w0 + c for c in range(NA_WIN_TILES)] + list(range(n_ctx_tiles))
    scores = []
    for hd in range(2):
        row = []
        for c, t in enumerate(tiles):
            s = _dot(qh[hd], kt_ref[0, t])
            if c < NA_WIN_TILES:
                s = s + bias_ref[hd, 0, :, TILE * c:TILE * (c + 1)]
            row.append(s)
        scores.append(row)
    tops = []
    for hd in range(2):
        top = scores[hd][0]
        for s in scores[hd][1:]:
            top = jnp.maximum(top, s)
        tops.append(jnp.max(jnp.maximum(top[:, :LANES], top[:, LANES:]), axis=-1, keepdims=True))
    probs = [[jnp.exp2(s - tops[hd]).astype(D_NA) for s in scores[hd]] for hd in range(2)]
    lane_v = lax.broadcasted_iota(jnp.int32, (TILE, LANES), 1)
    accs = [jnp.zeros((tq, LANES), F32), jnp.zeros((tq, LANES), F32)]
    for c, t in enumerate(tiles):
        start = t * TILE if isinstance(t, int) else pl.multiple_of(t * TILE, TILE)
        vt = v_ref[0, pl.ds(start, TILE), :]
        ones = jnp.ones_like(vt)
        v_ext = (jnp.where(lane_v < HEAD_DIM, vt, ones), jnp.where(lane_v < HEAD_DIM, ones, vt))
        for hd in range(2):
            accs[hd] = accs[hd] + _dot(probs[hd][c], v_ext[hd])
    outs = [a / pltpu.roll(a, HEAD_DIM, axis=1) for a in accs]
    o_ref[0] = jnp.where(lo, outs[0], outs[1]).astype(D_OUT)


def _na(qc, kct, vc, bias, *, n_ctx_tiles):
    nb, tt, c_w = qc.shape
    nt = tt // TILE
    n_lat = nt - n_ctx_tiles

    def bias_idx(b, p, i):
        il = i - n_ctx_tiles
        cls = jnp.where(il < 0, 3, jnp.where(il == 0, 0, jnp.where(il == n_lat - 1, 2, 1)))
        return (p, cls, 0, 0)

    return pl.pallas_call(
        functools.partial(_na_kernel, n_ctx_tiles=n_ctx_tiles, n_key_tiles=nt),
        grid=(nb, c_w // LANES, nt),
        in_specs=[
            pl.BlockSpec((1, TILE, LANES), lambda b, p, i: (b, i, p)),
            pl.BlockSpec((1, nt, LANES, TILE), lambda b, p, i: (b, 0, p, 0)),
            pl.BlockSpec((1, tt, LANES), lambda b, p, i: (b, 0, p)),
            pl.BlockSpec((2, 1, TILE, NA_WIN_TILES * TILE), bias_idx),
        ],
        out_specs=pl.BlockSpec((1, TILE, LANES), lambda b, p, i: (b, i, p)),
        out_shape=jax.ShapeDtypeStruct((nb, tt, c_w), D_OUT),
        compiler_params=_cparams(("parallel", "parallel", "parallel")),
        name="nbr_attn",
    )(qc, kct, vc, bias)


def _na_bias_table(rel_bias, seq):
    rows = seq // GRID_W
    rpt = TILE // GRID_W
    wrows = NA_WIN_TILES * rpt
    assert rows >= wrows and rows >= WIN_ROWS
    n_heads = rel_bias.shape[0]
    n_dr, n_dc = 2 * WIN_ROWS - 1, 2 * WIN_COLS - 1
    qcol, kcol = np.arange(GRID_W)[:, None], np.arange(GRID_W)[None, :]
    kc0 = np.clip(qcol - WIN_COLS // 2, 0, GRID_W - WIN_COLS)
    col_ok = (kcol >= kc0) & (kcol < kc0 + WIN_COLS)
    dcol = kcol - qcol + (WIN_COLS - 1)
    sel_c = ((dcol[None] == np.arange(n_dc)[:, None, None]) & col_ok[None]).astype(np.float32)
    sel_c = sel_c.reshape(n_dc, GRID_W * GRID_W)
    qr, kr = np.arange(rpt)[:, None], np.arange(wrows)[None, :]
    tables = []
    n_lat = rows // rpt
    for il in (0, min(1, n_lat - 1), n_lat - 1):
        r_abs = il * rpt + qr
        k_abs = int(np.clip(il - 1, 0, n_lat - NA_WIN_TILES)) * rpt + kr
        kr0 = np.clip(r_abs - WIN_ROWS // 2, 0, rows - WIN_ROWS)
        row_ok = (k_abs >= kr0) & (k_abs < kr0 + WIN_ROWS)
        drow = k_abs - r_abs + (WIN_ROWS - 1)
        sel_r = ((drow[:, :, None] == np.arange(n_dr)) & row_ok[:, :, None]).astype(np.float32)
        by_row = jnp.einsum("pd,hdc->hpc", sel_r.reshape(rpt * wrows, n_dr), rel_bias.astype(F32),
                            precision=lax.Precision.HIGHEST)
        vals = jnp.einsum("hpc,cx->hpx", by_row, sel_c, precision=lax.Precision.HIGHEST)
        vals = vals.reshape(n_heads, rpt, wrows, GRID_W, GRID_W).transpose(0, 1, 3, 2, 4)
        valid = (row_ok[:, None, :, None] & col_ok[None, :, None, :]).reshape(TILE, NA_WIN_TILES * TILE)
        vals = vals.reshape(n_heads, TILE, NA_WIN_TILES * TILE) * math.log2(math.e)
        tables.append(jnp.where(jnp.asarray(valid)[None], vals, NEG))
    tables.append(jnp.full_like(tables[0], NEG))
    return jnp.stack(tables, axis=1)


def _gdn_prep_kernel(x_ref, prev_ref, next_ref, ab_ref, cw_ref, alog_ref, dtb_ref,
                     q_ref, k_ref, v_ref, sc_ref, sct_ref, *, n_ctx_tiles, n_tiles, n_heads):
    i = pl.program_id(1)
    first = jnp.logical_or(i == 0, i == n_ctx_tiles)
    last = jnp.logical_or(i == n_ctx_tiles - 1, i == n_tiles - 1)
    prev = jnp.where(first, 0.0, prev_ref[0, 0])
    nxt = jnp.where(last, 0.0, next_ref[0, 0])
    ext = jnp.concatenate([prev, x_ref[0], nxt], axis=0)
    cw = cw_ref[...]
    y = None
    for j in range(CONV_W):
        o = SUBLANES - CONV_W // 2 + j
        term = ext[o:o + TILE] * cw[j:j + 1]
        y = term if y is None else y + term
    y = _silu(y)
    b_w = n_heads * HEAD_DIM
    lo = lax.broadcasted_iota(jnp.int32, (TILE, LANES), 1) < HEAD_DIM

    def l2norm_heads(z):
        parts = []
        for p in range(z.shape[1] // LANES):
            zp = z[:, LANES * p:LANES * (p + 1)]
            sq = zp * zp
            s_lo = jnp.sum(jnp.where(lo, sq, 0.0), axis=-1, keepdims=True)
            s_hi = jnp.sum(jnp.where(lo, 0.0, sq), axis=-1, keepdims=True)
            parts.append(zp * lax.rsqrt(jnp.where(lo, s_lo, s_hi) + EPS))
        return jnp.concatenate(parts, axis=1)

    q_ref[0] = l2norm_heads(y[:, :b_w]) * (HEAD_DIM ** -0.5)
    k_ref[0] = l2norm_heads(y[:, b_w:2 * b_w])
    v_ref[0] = y[:, 2 * b_w:]

    ab = ab_ref[0]
    lane = lax.broadcasted_iota(jnp.int32, ab.shape, 1)
    z = ab + dtb_ref[...]
    softplus = jnp.maximum(z, 0.0) + jnp.log(1.0 + jnp.exp(-jnp.abs(z)))
    g = jnp.where(lane < 2 * n_heads, -jnp.exp(alog_ref[...]) * softplus, 0.0)
    beta = 1.0 / (1.0 + jnp.exp(-ab))
    r = lax.broadcasted_iota(jnp.int32, (TILE, TILE), 0)
    c = lax.broadcasted_iota(jnp.int32, (TILE, TILE), 1)
    same = (r // GDN_CHUNK) == (c // GDN_CHUNK)
    tri_f = jnp.where(jnp.logical_and(same, c <= r), 1.0, 0.0).astype(BF16)
    tri_b = jnp.where(jnp.logical_and(same, c >= r), 1.0, 0.0).astype(BF16)
    gc = jnp.where(lane < n_heads, _dot_sel_left(tri_f, g), _dot_sel_left(tri_b, g))
    rr = lax.broadcasted_iota(jnp.int32, (LANES, LANES), 0)
    cc = lax.broadcasted_iota(jnp.int32, (LANES, LANES), 1)
    hh, jj = cc // SUBLANES, cc % SUBLANES
    valid = hh < n_heads
    sel_gb = jnp.where(jnp.logical_and(valid, jnp.logical_and(jj < 4, rr == jj * n_heads + hh)), 1.0, 0.0).astype(BF16)
    sel_gc = jnp.where(jnp.logical_and(valid, jnp.logical_and(jnp.logical_and(jj >= 4, jj < 6),
                                                              rr == (jj - 4) * n_heads + hh)), 1.0, 0.0).astype(BF16)
    gb = jnp.where(lane < 2 * n_heads, g, beta)
    packed = _dot_sel_right(gb, sel_gb) + _dot_sel_right(gc, sel_gc)
    packed_t = packed.T
    for h in range(n_heads):
        sc_ref[0, h] = packed[:, SUBLANES * h:SUBLANES * (h + 1)]
        sct_ref[0, h] = packed_t[SUBLANES * h:SUBLANES * (h + 1), :]


def _gdn_prep(qkvb, ab, conv_w, alog_v, dtb_v, *, n_ctx_tiles, n_heads):
    nb, tt, w3 = qkvb.shape
    nt = tt // TILE
    rows8 = TILE // SUBLANES
    x8 = qkvb.reshape(nb, tt // SUBLANES, SUBLANES, w3)
    hd = lambda w: pl.BlockSpec((1, n_heads, TILE, w), lambda b, i: (b, 0, i, 0))
    tok = pl.BlockSpec((1, TILE, w3 // 3), lambda b, i: (b, i, 0))
    return pl.pallas_call(
        functools.partial(_gdn_prep_kernel, n_ctx_tiles=n_ctx_tiles, n_tiles=nt, n_heads=n_heads),
        grid=(nb, nt),
        in_specs=[
            pl.BlockSpec((1, TILE, w3), lambda b, i: (b, i, 0)),
            pl.BlockSpec((1, 1, SUBLANES, w3), lambda b, i: (b, jnp.maximum(i * rows8 - 1, 0), 0, 0)),
            pl.BlockSpec((1, 1, SUBLANES, w3), lambda b, i: (b, jnp.minimum((i + 1) * rows8, nt * rows8 - 1), 0, 0)),
            pl.BlockSpec((1, TILE, LANES), lambda b, i: (b, i, 0)),
            pl.BlockSpec((CONV_W, w3), lambda b, i: (0, 0)),
            pl.BlockSpec((1, LANES), lambda b, i: (0, 0)),
            pl.BlockSpec((1, LANES), lambda b, i: (0, 0)),
        ],
        out_specs=(tok, tok, tok, hd(SUBLANES),
                   pl.BlockSpec((1, n_heads, SUBLANES, TILE), lambda b, i: (b, 0, 0, i))),
        out_shape=(
            jax.ShapeDtypeStruct((nb, tt, w3 // 3), F32),
            jax.ShapeDtypeStruct((nb, tt, w3 // 3), F32),
            jax.ShapeDtypeStruct((nb, tt, w3 // 3), F32),
            jax.ShapeDtypeStruct((nb, n_heads, tt, SUBLANES), F32),
            jax.ShapeDtypeStruct((nb, n_heads, SUBLANES, tt), F32),
        ),
        compiler_params=_cparams(("parallel", "parallel")),
        name="gdn_prep",
    )(qkvb, x8, x8, ab, conv_w, alog_v, dtb_v)


def _bmm(a, b):
    return jnp.einsum("nik,nkj->nij", a, b, preferred_element_type=F32)


def _bmm_nt(a, b):
    return jnp.einsum("nik,njk->nij", a, b, preferred_element_type=F32)


def _pair_blockdiag(x, lo):
    zero = jnp.zeros_like(x)
    return jnp.concatenate([jnp.where(lo, x, zero), jnp.where(lo, zero, x)], axis=1)


def _pmm(p, x, lo, dtype):
    return _bmm(p.astype(dtype), _pair_blockdiag(x.astype(dtype), lo))


def _tri_inverse(lmat, ii, jj, lo):
    blk = 16
    same = (ii // blk) == (jj // blk)
    dmat = jnp.where(same, lmat, 0.0)
    x = jnp.where(ii == jj, 1.0, 0.0) - dmat
    p = _pmm(dmat, dmat, lo, D_GI)
    n_sq = int(math.log2(blk)) - 1
    for step in range(n_sq):
        x = x + _pmm(p, x, lo, D_GI)
        if step < n_sq - 1:
            p = _pmm(p, p, lo, D_GI)
    while blk < GDN_CHUNK:
        wider = (ii // (2 * blk)) == (jj // (2 * blk))
        off = jnp.where(jnp.logical_and(wider, jnp.logical_not(same)), lmat, 0.0)
        x = x - _pmm(x, _pmm(off, x, lo, D_GI), lo, D_GI)
        same = wider
        blk *= 2
    return x


def _gdn_local_kernel(q_ref, k_ref, v_ref, sc_ref, sct_ref,
                      u_ref, w_ref, qh_ref, aqk_ref, ktt_ref, egl_ref):
    c = GDN_CHUNK
    cpt = TILE // c
    n_pairs = q_ref.shape[2] // LANES
    n = n_pairs * 2 * cpt
    lo = lax.broadcasted_iota(jnp.int32, (1, 1, LANES), 2) < HEAD_DIM

    def both(x):
        parts = []
        for p in range(n_pairs):
            x3 = x[:, LANES * p:LANES * (p + 1)].reshape(cpt, c, LANES)
            parts += [x3, x3]
        return jnp.concatenate(parts, axis=0)

    sc = sc_ref[0]

    def per_dir(col):
        parts = []
        for p in range(n_pairs):
            a0 = sc[2 * p].reshape(cpt, c, SUBLANES)
            a1 = sc[2 * p + 1].reshape(cpt, c, SUBLANES)
            for d in range(2):
                parts.append(jnp.where(lo, a0[:, :, col + d:col + d + 1], a1[:, :, col + d:col + d + 1]))
        return jnp.concatenate(parts, axis=0)

    q, k, v = both(q_ref[0]), both(k_ref[0]), both(v_ref[0])
    beta = per_dir(2)
    gcc = per_dir(4)
    sct = sct_ref[0]
    gcr = jnp.stack([jnp.concatenate([sct[2 * p, 4 + d:5 + d, ch * c:(ch + 1) * c],
                                      sct[2 * p + 1, 4 + d:5 + d, ch * c:(ch + 1) * c]], axis=1)
                     for p in range(n_pairs) for d in range(2) for ch in range(cpt)], axis=0)

    ii = lax.broadcasted_iota(jnp.int32, (n, c, LANES), 1)
    jj = lax.broadcasted_iota(jnp.int32, (n, c, LANES), 2) % c
    fwd = (lax.broadcasted_iota(jnp.int32, (n, c, LANES), 0) // cpt) % 2 == 0
    incl = jnp.logical_or(jnp.logical_and(fwd, ii >= jj), jnp.logical_and(jnp.logical_not(fwd), ii <= jj))
    strict = jnp.logical_and(incl, ii != jj)
    decay = jnp.where(incl, jnp.exp(jnp.where(incl, gcc - gcr, 0.0)), 0.0)
    kb = k * beta
    a = _bmm_nt(jnp.concatenate([kb, q], axis=1).astype(D_GA), _pair_blockdiag(k.astype(D_GA), lo))
    lmat = jnp.where(strict, a[:, :c] * decay, 0.0)
    aqk = a[:, c:] * decay
    tinv = _tri_inverse(lmat, ii, jj, lo)
    egc = jnp.exp(gcc)
    u = _pmm(tinv, v * beta, lo, D_GT)
    w = _pmm(tinv, kb * egc, lo, D_GT)
    fwd1 = (lax.broadcasted_iota(jnp.int32, (n, 1, 1), 0) // cpt) % 2 == 0
    glast = jnp.where(fwd1, gcc[:, c - 1:c, :], gcc[:, 0:1, :])
    ktail = k * jnp.exp(glast - gcc)
    eye = jnp.where(ii == jj, 1.0, 0.0).astype(D_GDN)
    ktt = _bmm_nt(eye, _pair_blockdiag(ktail.astype(D_GDN), lo))
    split = lambda z: z.reshape((n_pairs, 2, cpt) + z.shape[1:])
    u_ref[0] = split(u)
    w_ref[0] = split(w.astype(D_GDN))
    qh_ref[0] = split((q * egc).astype(D_GDN))
    aqk_ref[0] = split(aqk.astype(D_GDN))
    ktt_ref[0] = split(ktt.astype(D_GDN))
    egl_ref[0] = split(jnp.exp(glast))


def _gdn_local(qn, kn, vv, sc, sct):
    nb, tt, b_w = qn.shape
    nh = sc.shape[1]
    n_pairs = b_w // LANES
    nt = tt // TILE
    nc = tt // GDN_CHUNK
    cpt = TILE // GDN_CHUNK
    tok = pl.BlockSpec((1, TILE, b_w), lambda b, i: (b, i, 0))
    chunked = lambda r: pl.BlockSpec((1, n_pairs, 2, cpt, r, LANES), lambda b, i: (b, 0, 0, i, 0, 0))
    shape = lambda r, dt: jax.ShapeDtypeStruct((nb, n_pairs, 2, nc, r, LANES), dt)
    return pl.pallas_call(
        _gdn_local_kernel,
        grid=(nb, nt),
        in_specs=[tok, tok, tok,
                  pl.BlockSpec((1, nh, TILE, SUBLANES), lambda b, i: (b, 0, i, 0)),
                  pl.BlockSpec((1, nh, SUBLANES, TILE), lambda b, i: (b, 0, 0, i))],
        out_specs=(chunked(GDN_CHUNK), chunked(GDN_CHUNK), chunked(GDN_CHUNK),
                   chunked(GDN_CHUNK), chunked(HEAD_DIM), chunked(1)),
        out_shape=(
            shape(GDN_CHUNK, F32),
            shape(GDN_CHUNK, D_GDN),
            shape(GDN_CHUNK, D_GDN),
            shape(GDN_CHUNK, D_GDN),
            shape(HEAD_DIM, D_GDN),
            shape(1, F32),
        ),
        compiler_params=_cparams(("parallel", "parallel")),
        name="gdn_local",
    )(qn, kn, vv, sc, sct)


def _gdn_scan_kernel(uf, ub, wf, wb, qf, qb, af, ab, kf, kb, ef, eb, of_ref, ob_ref, s_ref, *, cpt, n_pairs):
    @pl.when(pl.program_id(1) == 0)
    def _():
        s_ref[...] = jnp.zeros_like(s_ref)

    lo = lax.broadcasted_iota(jnp.int32, (1, 1, LANES), 2) < HEAD_DIM
    s = s_ref[...]
    for t in range(cpt):
        tb = cpt - 1 - t
        pair = lambda f, b: jnp.concatenate([f[0, :, 0, t], b[0, :, 0, tb]], axis=0)
        sd = _pair_blockdiag(s.astype(D_GS), lo)
        v_new = pair(uf, ub) - _bmm(pair(wf, wb), sd)
        vd = _pair_blockdiag(v_new.astype(D_GS), lo)
        o = _bmm(pair(qf, qb), sd) + _bmm(pair(af, ab), vd)
        s = s * pair(ef, eb) + _bmm(pair(kf, kb), vd)
        of_ref[0, :, t] = o[:n_pairs]
        ob_ref[0, :, tb] = o[n_pairs:]
    s_ref[...] = s


def _gdn_scan(u, w, qh, aqk, ktt, egl, *, n_ctx_tiles):
    nb, n_pairs, _, nc, c, _ = u.shape
    cpt = TILE // c
    nt = nc // cpt

    def bwd_tile(j):
        return jnp.where(j < n_ctx_tiles, n_ctx_tiles - 1 - j, nt - 1 + n_ctx_tiles - j)

    def specs(a):
        blk = (1, n_pairs, 1, cpt) + a.shape[4:]
        return [pl.BlockSpec(blk, lambda b, j: (b, 0, 0, j, 0, 0)),
                pl.BlockSpec(blk, lambda b, j: (b, 0, 1, bwd_tile(j), 0, 0))]

    args, in_specs = [], []
    for a in (u, w, qh, aqk, ktt, egl):
        args += [a, a]
        in_specs += specs(a)
    oshape = jax.ShapeDtypeStruct((nb, n_pairs, nc, c, LANES), F32)
    return pl.pallas_call(
        functools.partial(_gdn_scan_kernel, cpt=cpt, n_pairs=n_pairs),
        grid=(nb, nt),
        in_specs=in_specs,
        out_specs=(pl.BlockSpec((1, n_pairs, cpt, c, LANES), lambda b, j: (b, 0, j, 0, 0)),
                   pl.BlockSpec((1, n_pairs, cpt, c, LANES), lambda b, j: (b, 0, bwd_tile(j), 0, 0))),
        out_shape=(oshape, oshape),
        scratch_shapes=[pltpu.VMEM((2 * n_pairs, HEAD_DIM, LANES), F32)],
        compiler_params=_cparams(("parallel", "arbitrary")),
        name="gdn_scan",
    )(*args)


def _outproj_kernel(x_ref, mod_ref, ya_ref, of_ref, ob_ref, gate_ref, gnw_ref, yc_ref, w_ref, o_ref):
    lo = lax.broadcasted_iota(jnp.int32, (TILE, LANES), 1) < HEAD_DIM
    pairs = []
    for p in range(of_ref.shape[1]):
        o = of_ref[0, p] + ob_ref[0, p]
        sq = o * o
        ms_lo = jnp.sum(jnp.where(lo, sq, 0.0), axis=-1, keepdims=True) * (1.0 / HEAD_DIM)
        ms_hi = jnp.sum(jnp.where(lo, 0.0, sq), axis=-1, keepdims=True) * (1.0 / HEAD_DIM)
        pairs.append(o * lax.rsqrt(jnp.where(lo, ms_lo, ms_hi) + EPS) * gnw_ref[...])
    yb = (jnp.concatenate(pairs, axis=1) * _silu(gate_ref[0])).astype(D_OUT)
    y = jnp.concatenate([ya_ref[0], yb, yc_ref[0]], axis=1)
    o_ref[0] = x_ref[0] + mod_ref[0][2:3] * _dot(y, w_ref[...])


def _outproj(xs, mods, ya, scan_f, scan_b, gate, gnw2, yc, w, *, n_batch, n_ctx_tiles):
    nb, tt, d = xs.shape
    nt = tt // TILE
    a_w, b_w, c_w = ya.shape[2], gate.shape[2], yc.shape[2]
    n_pairs = scan_f.shape[1]

    def mod_idx(b, i):
        return (jnp.where(i < n_ctx_tiles, n_batch, b), 0, 0)

    tok = lambda wd: pl.BlockSpec((1, TILE, wd), lambda b, i: (b, i, 0))
    pairs = pl.BlockSpec((1, n_pairs, TILE, LANES), lambda b, i: (b, 0, i, 0))
    return pl.pallas_call(
        _outproj_kernel,
        grid=(nb, nt),
        in_specs=[
            tok(d),
            pl.BlockSpec((1, 6, d), mod_idx),
            tok(a_w),
            pairs,
            pairs,
            tok(b_w),
            pl.BlockSpec((1, LANES), lambda b, i: (0, 0)),
            tok(c_w),
            pl.BlockSpec(w.shape, lambda b, i: (0, 0)),
        ],
        out_specs=tok(d),
        out_shape=jax.ShapeDtypeStruct((nb, tt, d), F32),
        compiler_params=_cparams(("parallel", "parallel")),
        name="outproj",
    )(xs, mods, ya, scan_f, scan_b, gate, gnw2, yc, w)


def _ffn_kernel(x_ref, mod_ref, nw_ref, wi_ref, wo_ref, o_ref, *, d_ff):
    mod = mod_ref[0]
    x = x_ref[0]
    h = _norm_mod(x, nw_ref[...], mod[3:4], mod[4:5]).astype(D_FFN)
    gu = _dot(h, wi_ref[...])
    a = (_silu(gu[:, :d_ff]) * gu[:, d_ff:]).astype(D_FFN)
    o_ref[0] = x + mod[5:6] * _dot(a, wo_ref[...])


def _ffn(xs, mods, nw, wi, wo, *, n_batch, n_ctx_tiles):
    nb, tt, d = xs.shape
    nt = tt // TILE
    d_ff = wo.shape[0]

    def mod_idx(b, i):
        return (jnp.where(i < n_ctx_tiles, n_batch, b), 0, 0)

    tok = pl.BlockSpec((1, TILE, d), lambda b, i: (b, i, 0))
    return pl.pallas_call(
        functools.partial(_ffn_kernel, d_ff=d_ff),
        grid=(nb, nt),
        in_specs=[
            tok,
            pl.BlockSpec((1, 6, d), mod_idx),
            pl.BlockSpec((1, d), lambda b, i: (0, 0)),
            pl.BlockSpec(wi.shape, lambda b, i: (0, 0), pipeline_mode=pl.Buffered(1)),
            pl.BlockSpec(wo.shape, lambda b, i: (0, 0), pipeline_mode=pl.Buffered(1)),
        ],
        out_specs=tok,
        out_shape=jax.ShapeDtypeStruct((nb, tt, d), F32),
        compiler_params=_cparams(("parallel", "parallel")),
        name="ffn",
    )(xs, mods, nw, wi, wo)


def _final_kernel(x_ref, nw_ref, o_ref):
    x = x_ref[0]
    ms = jnp.mean(x * x, axis=-1, keepdims=True)
    o_ref[0] = x * lax.rsqrt(ms + EPS) * nw_ref[...]


def _final_norm(xs, nw, *, n_ctx_tiles):
    nb, tt, d = xs.shape
    n_lat = tt // TILE - n_ctx_tiles
    return pl.pallas_call(
        _final_kernel,
        grid=(nb, n_lat),
        in_specs=[pl.BlockSpec((1, TILE, d), lambda b, i: (b, i + n_ctx_tiles, 0)),
                  pl.BlockSpec((1, d), lambda b, i: (0, 0))],
        out_specs=pl.BlockSpec((1, TILE, d), lambda b, i: (b, i, 0)),
        out_shape=jax.ShapeDtypeStruct((nb, n_lat * TILE, d), F32),
        compiler_params=_cparams(("parallel", "parallel")),
        name="final_norm",
    )(xs, nw)


def _rope_tables(seq, ctx_len, n_rep):
    half = A_QK // 2
    inv_freq = 1.0 / (ROPE_THETA ** (jnp.arange(0, half, 2, dtype=F32) / half))
    t = jnp.arange(seq, dtype=jnp.int32)
    ang_r = (t // GRID_W).astype(F32)[:, None] * inv_freq
    ang_c = (t % GRID_W).astype(F32)[:, None] * inv_freq
    ang = jnp.concatenate([ang_r, ang_r, ang_c, ang_c], axis=-1)
    cos = jnp.concatenate([jnp.ones((ctx_len, A_QK), F32), jnp.cos(ang)], axis=0)
    sin = jnp.concatenate([jnp.zeros((ctx_len, A_QK), F32), jnp.sin(ang)], axis=0)
    return jnp.tile(cos, (1, n_rep)), jnp.tile(sin, (1, n_rep))


def kernel(x, c, ctx, c_ctx, w_mod, b_mod, norm1_w, norm2_w, w_in, w_out, lambda_q1, lambda_k1, lambda_q2,
           lambda_k2, diff_norm_w, conv_w, a_log, dt_bias, gdn_norm_w, na_bias, w_ffn_in, w_ffn_out,
           final_norm_w):
    nb, seq, d = x.shape
    ctx_len = ctx.shape[1]
    depth = w_mod.shape[0]
    b_heads = a_log.shape[-1]
    c_heads = na_bias.shape[1]
    b_w = b_heads * HEAD_DIM
    c_w = c_heads * HEAD_DIM
    a_w = w_out.shape[1] - b_w - c_w
    assert seq % TILE == 0 and ctx_len % TILE == 0 and seq % GRID_W == 0
    assert a_w % LANES == 0 and b_w % LANES == 0 and c_w % LANES == 0 and 4 * b_heads <= LANES
    n_ctx_tiles = ctx_len // TILE

    xs = jnp.concatenate([ctx, x], axis=1)

    n_rows = -(-(nb + 1) // SUBLANES) * SUBLANES
    cc = jnp.concatenate([c, c_ctx[None, :], jnp.zeros((n_rows - nb - 1, d), F32)], axis=0)
    mods_all = _modulation(cc, w_mod, b_mod).reshape(depth, n_rows, 6, d)

    cos, sin = _rope_tables(seq, ctx_len, a_w // A_QK)
    cost, sint = cos.T, sin.T

    sizes = (3 * a_w, 3 * b_w, b_w, 2 * b_heads, 2 * b_heads, 3 * c_w)
    offs = np.concatenate([[0], np.cumsum(sizes)])
    o_a, o_b, o_g, o_al, o_be, o_c = (int(v) for v in offs[:6])
    pad = LANES - 4 * b_heads

    for l in range(depth):
        wl = w_in[l]
        wm = jnp.concatenate([
            wl[:, o_a:o_a + a_w],
            wl[:, o_a + 2 * a_w:o_a + 3 * a_w],
            wl[:, o_b:o_b + 3 * b_w],
            wl[:, o_g:o_g + b_w],
            wl[:, o_c:o_c + c_w],
            wl[:, o_c + 2 * c_w:o_c + 3 * c_w],
            wl[:, o_al:o_al + 4 * b_heads],
            jnp.zeros((d, pad), F32),
        ], axis=1).astype(D_IN)
        wk = jnp.concatenate([wl[:, o_a + a_w:o_a + 2 * a_w], wl[:, o_c + c_w:o_c + 2 * c_w]], axis=1).T.astype(D_IN)
        mods = mods_all[l]
        lam_init = 0.8 - 0.6 * math.exp(-0.3 * l)

        qa, kat, va, qkvb, gate, ab, qc, kct, vc = _inproj(
            xs, mods, norm1_w[l][None, :], wm, wk, cos, sin, cost, sint,
            n_batch=nb, n_ctx_tiles=n_ctx_tiles, a_w=a_w, b_w=b_w, c_w=c_w)

        lam_vecs = jnp.stack([lambda_q1[l], lambda_k1[l], lambda_q2[l], lambda_k2[l]], axis=0).astype(F32)
        nw2 = jnp.tile(diff_norm_w[l], LANES // HEAD_DIM)[None, :]
        ya = _diffattn(lam_vecs, nw2, qa, kat, va, n_ctx_tiles=n_ctx_tiles, lam_init=lam_init)

        yc = _na(qc, kct, vc, _na_bias_table(na_bias[l], seq), n_ctx_tiles=n_ctx_tiles)

        alog_v = jnp.concatenate([a_log[l].reshape(-1), jnp.zeros((LANES - 2 * b_heads,), F32)])[None, :]
        dtb_v = jnp.concatenate([dt_bias[l].reshape(-1), jnp.zeros((LANES - 2 * b_heads,), F32)])[None, :]
        qn, kn, vv, sc, sct = _gdn_prep(qkvb, ab, conv_w[l], alog_v, dtb_v,
                                        n_ctx_tiles=n_ctx_tiles, n_heads=b_heads)
        u, w, qh, aqk, ktt, egl = _gdn_local(qn, kn, vv, sc, sct)
        scan_f, scan_b = _gdn_scan(u, w, qh, aqk, ktt, egl, n_ctx_tiles=n_ctx_tiles)
        scan_f = scan_f.reshape(nb, b_w // LANES, ctx_len + seq, LANES)
        scan_b = scan_b.reshape(nb, b_w // LANES, ctx_len + seq, LANES)

        gnw2 = jnp.tile(gdn_norm_w[l], LANES // HEAD_DIM)[None, :]
        xs = _outproj(xs, mods, ya, scan_f, scan_b, gate, gnw2, yc, w_out[l].astype(D_OUT),
                      n_batch=nb, n_ctx_tiles=n_ctx_tiles)
        xs = _ffn(xs, mods, norm2_w[l][None, :], w_ffn_in[l].astype(D_FFN), w_ffn_out[l].astype(D_FFN),
                  n_batch=nb, n_ctx_tiles=n_ctx_tiles)

    return _final_norm(xs, final_norm_w[None, :], n_ctx_tiles=n_ctx_tiles)
```

```python
import functools
import math

import jax
import jax.numpy as jnp
import numpy as np
from jax import lax
from jax.experimental import pallas as pl
from jax.experimental.pallas import tpu as pltpu

F32 = jnp.float32
BF16 = jnp.bfloat16
D_MOD = D_IN = D_OUT = D_FFN = D_ATT = D_NA = D_GDN = BF16
D_GA = D_GI = D_GT = D_GS = BF16

HEAD_DIM = 64
A_QK = HEAD_DIM // 2
GRID_W = 64
ROPE_THETA = 10000.0
CONV_W = 5
GDN_CHUNK = 64
WIN_ROWS = 8
WIN_COLS = 16
EPS = 1e-6

LANES = 128
SUBLANES = 8
TILE = 256
NA_WIN_TILES = 3
NEG = -1e30
VMEM_LIMIT = 56 * 1024 * 1024


def _cparams(sem):
    return pltpu.CompilerParams(dimension_semantics=sem, vmem_limit_bytes=VMEM_LIMIT)


def _silu(x):
    return x * (1.0 / (1.0 + jnp.exp(-x)))


def _dot(a, b):
    return jnp.dot(a, b, preferred_element_type=F32)


def _dot_nt(a, b):
    return lax.dot_general(a, b, (((1,), (1,)), ((), ())), preferred_element_type=F32)


def _split3(x):
    hi = x.astype(BF16)
    r1 = x - hi.astype(F32)
    mid = r1.astype(BF16)
    lo = (r1 - mid.astype(F32)).astype(BF16)
    return hi, mid, lo


def _dot_sel_right(x, sel):
    hi, mid, lo = _split3(x)
    return _dot(hi, sel) + _dot(mid, sel) + _dot(lo, sel)


def _dot_sel_left(sel, x):
    hi, mid, lo = _split3(x)
    return _dot(sel, hi) + _dot(sel, mid) + _dot(sel, lo)


def _mod_kernel(c_ref, w_ref, b_ref, o_ref):
    s = _silu(c_ref[...])
    o_ref[0] = _dot(s.astype(D_MOD), w_ref[0].astype(D_MOD)) + b_ref[0]


def _modulation(cc, w_mod, b_mod):
    depth, d, n = w_mod.shape
    rows = cc.shape[0]
    tn = 1536
    assert n % tn == 0
    return pl.pallas_call(
        _mod_kernel,
        grid=(depth, n // tn),
        in_specs=[
            pl.BlockSpec((rows, d), lambda l, j: (0, 0)),
            pl.BlockSpec((1, d, tn), lambda l, j: (l, 0, j)),
            pl.BlockSpec((1, 1, tn), lambda l, j: (l, 0, j)),
        ],
        out_specs=pl.BlockSpec((1, rows, tn), lambda l, j: (l, 0, j)),
        out_shape=jax.ShapeDtypeStruct((depth, rows, n), F32),
        compiler_params=_cparams(("parallel", "parallel")),
        name="modulation",
    )(cc, w_mod, b_mod.reshape(depth, 1, n))


def _norm_mod(x, nw, shift, scale):
    ms = jnp.mean(x * x, axis=-1, keepdims=True)
    y = x * lax.rsqrt(ms + EPS) * nw
    return y * (1.0 + scale) + shift


def _rope_lanes(x, cos, sin):
    parts = []
    for j in range(x.shape[1] // LANES):
        xj = x[:, LANES * j:LANES * (j + 1)]
        nxt = pltpu.roll(xj, LANES - 8, axis=1)
        prv = pltpu.roll(xj, 8, axis=1)
        lane = lax.broadcasted_iota(jnp.int32, xj.shape, 1)
        parts.append(jnp.where((lane % 16) < 8, -nxt, prv))
    rot = jnp.concatenate(parts, axis=1)
    return x * cos + rot * sin


def _rope_rows(x, cos, sin):
    n = x.shape[0]
    nxt = pltpu.roll(x, n - 8, axis=0)
    prv = pltpu.roll(x, 8, axis=0)
    row = lax.broadcasted_iota(jnp.int32, x.shape, 0)
    rot = jnp.where((row % 16) < 8, -nxt, prv)
    return x * cos + rot * sin


def _inproj_kernel(x_ref, mod_ref, nw_ref, wm_ref, wk_ref, cos_ref, sin_ref, cost_ref, sint_ref,
                   qa_ref, kat_ref, va_ref, qkvb_ref, gate_ref, ab_ref, qc_ref, kct_ref, vc_ref,
                   *, a_w, b_w, c_w):
    mod = mod_ref[0]
    h = _norm_mod(x_ref[0], nw_ref[...], mod[0:1], mod[1:2]).astype(D_IN)
    acc = _dot(h, wm_ref[0])
    kt = _dot_nt(wk_ref[0], h)
    o = 0
    qa = _rope_lanes(acc[:, o:o + a_w], cos_ref[...], sin_ref[...])
    qa_ref[0] = (qa * (A_QK ** -0.5 * math.log2(math.e))).astype(D_ATT)
    o += a_w
    va_ref[0] = acc[:, o:o + a_w].astype(D_ATT)
    o += a_w
    qkvb_ref[0] = acc[:, o:o + 3 * b_w]
    o += 3 * b_w
    gate_ref[0] = acc[:, o:o + b_w]
    o += b_w
    qc_ref[0] = (acc[:, o:o + c_w] * (HEAD_DIM ** -0.5 * math.log2(math.e))).astype(D_NA)
    o += c_w
    vc_ref[0] = acc[:, o:o + c_w].astype(D_NA)
    o += c_w
    ab_ref[0] = acc[:, o:o + LANES]
    kat_ref[0, 0] = _rope_rows(kt[:a_w], cost_ref[...], sint_ref[...]).astype(D_ATT)
    kct_ref[0, 0] = kt[a_w:].astype(D_NA)


def _inproj(xs, mods, nw, wm, wk, cos, sin, cost, sint, *, layer, n_batch, n_ctx_tiles, a_w, b_w, c_w):
    nb, tt, d = xs.shape
    nt = tt // TILE
    ctx_row = n_batch

    def mod_idx(b, i):
        return (jnp.where(i < n_ctx_tiles, ctx_row, b), 0, 0)

    tok = lambda w: pl.BlockSpec((1, TILE, w), lambda b, i: (b, i, 0))
    tokt = lambda w: pl.BlockSpec((1, 1, w, TILE), lambda b, i: (b, i, 0, 0))
    out_shape = (
        jax.ShapeDtypeStruct((nb, tt, a_w), D_ATT),
        jax.ShapeDtypeStruct((nb, nt, a_w, TILE), D_ATT),
        jax.ShapeDtypeStruct((nb, tt, a_w), D_ATT),
        jax.ShapeDtypeStruct((nb, tt, 3 * b_w), F32),
        jax.ShapeDtypeStruct((nb, tt, b_w), F32),
        jax.ShapeDtypeStruct((nb, tt, LANES), F32),
        jax.ShapeDtypeStruct((nb, tt, c_w), D_NA),
        jax.ShapeDtypeStruct((nb, nt, c_w, TILE), D_NA),
        jax.ShapeDtypeStruct((nb, tt, c_w), D_NA),
    )
    return pl.pallas_call(
        functools.partial(_inproj_kernel, a_w=a_w, b_w=b_w, c_w=c_w),
        grid=(nb, nt),
        in_specs=[
            tok(d),
            pl.BlockSpec((1, 6, d), mod_idx),
            pl.BlockSpec((1, d), lambda b, i: (0, 0)),
            pl.BlockSpec((1,) + wm.shape[1:], lambda b, i: (layer, 0, 0)),
            pl.BlockSpec((1,) + wk.shape[1:], lambda b, i: (layer, 0, 0)),
            pl.BlockSpec((TILE, a_w), lambda b, i: (i, 0)),
            pl.BlockSpec((TILE, a_w), lambda b, i: (i, 0)),
            pl.BlockSpec((a_w, TILE), lambda b, i: (0, i)),
            pl.BlockSpec((a_w, TILE), lambda b, i: (0, i)),
        ],
        out_specs=(tok(a_w), tokt(a_w), tok(a_w), tok(3 * b_w), tok(b_w), tok(LANES),
                   tok(c_w), tokt(c_w), tok(c_w)),
        out_shape=out_shape,
        compiler_params=_cparams(("parallel", "parallel")),
        name="inproj",
    )(xs, mods, nw, wm, wk, cos, sin, cost, sint)


def _diffattn_kernel(lam_ref, nw_ref, q_ref, kt_ref, v_ref, o_ref, mx_ref, acc_ref,
                     *, n_ctx_tiles, n_key_tiles, lam_init):
    i = pl.program_id(2)
    lv = lam_ref[...]
    lam = (jnp.exp(jnp.sum(lv[0:1] * lv[1:2], axis=-1, keepdims=True))
           - jnp.exp(jnp.sum(lv[2:3] * lv[3:4], axis=-1, keepdims=True)) + lam_init)
    q = q_ref[0]
    tq = q.shape[0]
    lane = lax.broadcasted_iota(jnp.int32, (tq, LANES), 1)
    lo = lane < HEAD_DIM
    zero = jnp.zeros_like(q)
    qm = [jnp.where((lane // A_QK) == m, q, zero) for m in range(LANES // A_QK)]
    n_maps = len(qm)
    n_lat = n_key_tiles - n_ctx_tiles
    group = max(u for u in (16, 8, 4, 2, 1) if n_lat % u == 0)
    ctx_tiles = list(range(n_ctx_tiles))
    lat_tiles = lambda g: [n_ctx_tiles + g * group + u for u in range(group)]

    mx_ref[...] = jnp.full(mx_ref.shape, NEG, F32)
    acc_ref[...] = jnp.zeros(acc_ref.shape, F32)
    lane_v = lax.broadcasted_iota(jnp.int32, (TILE, LANES), 1)

    def visit(tiles):
        ktiles = [kt_ref[0, j] for j in tiles]
        v_ext = []
        for j in tiles:
            start = j * TILE if isinstance(j, int) else pl.multiple_of(j * TILE, TILE)
            vtile = v_ref[0, pl.ds(start, TILE), :]
            ones = jnp.ones_like(vtile)
            v_ext.append((jnp.where(lane_v < HEAD_DIM, vtile, ones), jnp.where(lane_v < HEAD_DIM, ones, vtile)))
        scores = [[_dot(qm[m], ktile) for ktile in ktiles] for m in range(n_maps)]
        m_new = []
        for m in range(n_maps):
            top = scores[m][0]
            for s in scores[m][1:]:
                top = jnp.maximum(top, s)
            top = jnp.max(jnp.maximum(top[:, :LANES], top[:, LANES:]), axis=-1, keepdims=True)
            m_new.append(jnp.maximum(mx_ref[m], top))
        probs = [[jnp.exp2(s - m_new[m]).astype(D_ATT) for s in scores[m]] for m in range(n_maps)]
        for m in range(n_maps):
            acc = acc_ref[m] * jnp.exp2(mx_ref[m] - m_new[m])
            for p, ve in zip(probs[m], v_ext):
                acc = acc + _dot(p, ve[m // 2])
            mx_ref[m] = m_new[m]
            acc_ref[m] = acc

    @pl.when(i < n_ctx_tiles)
    def _():
        visit(ctx_tiles)

    @pl.when(i >= n_ctx_tiles)
    def _():
        visit(ctx_tiles + lat_tiles(0))
        lax.fori_loop(1, n_lat // group, lambda g, c: (visit(lat_tiles(g)), c)[1], 0)
    attn = []
    for m in range(n_maps):
        a = acc_ref[m]
        attn.append(a / pltpu.roll(a, HEAD_DIM, axis=1))
    o = jnp.where(lo, attn[0] - lam * attn[1], attn[2] - lam * attn[3])
    sq = o * o
    ms_lo = jnp.sum(jnp.where(lo, sq, 0.0), axis=-1, keepdims=True) * (1.0 / HEAD_DIM)
    ms_hi = jnp.sum(jnp.where(lo, 0.0, sq), axis=-1, keepdims=True) * (1.0 / HEAD_DIM)
    y = o * lax.rsqrt(jnp.where(lo, ms_lo, ms_hi) + EPS) * nw_ref[...]
    o_ref[0] = (y * (1.0 - lam_init)).astype(D_OUT)


def _diffattn(lam_vecs, nw2, qa, kat, va, *, n_ctx_tiles, lam_init):
    nb, tt, a_w = qa.shape
    nt = tt // TILE
    return pl.pallas_call(
        functools.partial(_diffattn_kernel, n_ctx_tiles=n_ctx_tiles, n_key_tiles=nt, lam_init=lam_init),
        grid=(nb, a_w // LANES, nt),
        in_specs=[
            pl.BlockSpec(lam_vecs.shape, lambda b, p, i: (0, 0)),
            pl.BlockSpec((1, LANES), lambda b, p, i: (0, 0)),
            pl.BlockSpec((1, TILE, LANES), lambda b, p, i: (b, i, p)),
            pl.BlockSpec((1, nt, LANES, TILE), lambda b, p, i: (b, 0, p, 0)),
            pl.BlockSpec((1, tt, LANES), lambda b, p, i: (b, 0, p)),
        ],
        out_specs=pl.BlockSpec((1, TILE, LANES), lambda b, p, i: (b, i, p)),
        out_shape=jax.ShapeDtypeStruct((nb, tt, a_w), D_OUT),
        scratch_shapes=[pltpu.VMEM((LANES // A_QK, TILE, 1), F32),
                        pltpu.VMEM((LANES // A_QK, TILE, LANES), F32)],
        compiler_params=_cparams(("parallel", "parallel", "parallel")),
        name="diffattn",
    )(lam_vecs, nw2, qa, kat, va)


def _na_kernel(q_ref, kt_ref, v_ref, bias_ref, o_ref, *, n_ctx_tiles, n_key_tiles):
    i = pl.program_id(2)
    n_lat = n_key_tiles - n_ctx_tiles
    w0 = n_ctx_tiles + jnp.clip(i - n_ctx_tiles - 1, 0, n_lat - NA_WIN_TILES)
    q = q_ref[0]
    tq = q.shape[0]
    lane = lax.broadcasted_iota(jnp.int32, (tq, LANES), 1)
    lo = lane < HEAD_DIM
    zero = jnp.zeros_like(q)
    qh = [jnp.where(lo, q, zero), jnp.where(lo, zero, q)]
    tiles = [w0 + c for c in range(NA_WIN_TILES)] + list(range(n_ctx_tiles))
    scores = []
    for hd in range(2):
        row = []
        for c, t in enumerate(tiles):
            s = _dot(qh[hd], kt_ref[0, t])
            if c < NA_WIN_TILES:
                s = s + bias_ref[0, hd, 0, :, TILE * c:TILE * (c + 1)]
            row.append(s)
        scores.append(row)
    tops = []
    for hd in range(2):
        top = scores[hd][0]
        for s in scores[hd][1:]:
            top = jnp.maximum(top, s)
        tops.append(jnp.max(jnp.maximum(top[:, :LANES], top[:, LANES:]), axis=-1, keepdims=True))
    probs = [[jnp.exp2(s - tops[hd]).astype(D_NA) for s in scores[hd]] for hd in range(2)]
    lane_v = lax.broadcasted_iota(jnp.int32, (TILE, LANES), 1)
    accs = [jnp.zeros((tq, LANES), F32), jnp.zeros((tq, LANES), F32)]
    for c, t in enumerate(tiles):
        start = t * TILE if isinstance(t, int) else pl.multiple_of(t * TILE, TILE)
        vt = v_ref[0, pl.ds(start, TILE), :]
        ones = jnp.ones_like(vt)
        v_ext = (jnp.where(lane_v < HEAD_DIM, vt, ones), jnp.where(lane_v < HEAD_DIM, ones, vt))
        for hd in range(2):
            accs[hd] = accs[hd] + _dot(probs[hd][c], v_ext[hd])
    outs = [a / pltpu.roll(a, HEAD_DIM, axis=1) for a in accs]
    o_ref[0] = jnp.where(lo, outs[0], outs[1]).astype(D_OUT)


def _na(qc, kct, vc, bias, *, layer, n_ctx_tiles):
    nb, tt, c_w = qc.shape
    nt = tt // TILE
    n_lat = nt - n_ctx_tiles

    def bias_idx(b, p, i):
        il = i - n_ctx_tiles
        cls = jnp.where(il < 0, 3, jnp.where(il == 0, 0, jnp.where(il == n_lat - 1, 2, 1)))
        return (layer, p, cls, 0, 0)

    return pl.pallas_call(
        functools.partial(_na_kernel, n_ctx_tiles=n_ctx_tiles, n_key_tiles=nt),
        grid=(nb, c_w // LANES, nt),
        in_specs=[
            pl.BlockSpec((1, TILE, LANES), lambda b, p, i: (b, i, p)),
            pl.BlockSpec((1, nt, LANES, TILE), lambda b, p, i: (b, 0, p, 0)),
            pl.BlockSpec((1, tt, LANES), lambda b, p, i: (b, 0, p)),
            pl.BlockSpec((1, 2, 1, TILE, NA_WIN_TILES * TILE), bias_idx),
        ],
        out_specs=pl.BlockSpec((1, TILE, LANES), lambda b, p, i: (b, i, p)),
        out_shape=jax.ShapeDtypeStruct((nb, tt, c_w), D_OUT),
        compiler_params=_cparams(("parallel", "parallel", "parallel")),
        name="nbr_attn",
    )(qc, kct, vc, bias)


def _na_bias_table(rel_bias, seq):
    rows = seq // GRID_W
    rpt = TILE // GRID_W
    wrows = NA_WIN_TILES * rpt
    assert rows >= wrows and rows >= WIN_ROWS
    n_layers, n_heads = rel_bias.shape[:2]
    n_dr, n_dc = 2 * WIN_ROWS - 1, 2 * WIN_COLS - 1
    qcol, kcol = np.arange(GRID_W)[:, None], np.arange(GRID_W)[None, :]
    kc0 = np.clip(qcol - WIN_COLS // 2, 0, GRID_W - WIN_COLS)
    col_ok = (kcol >= kc0) & (kcol < kc0 + WIN_COLS)
    dcol = kcol - qcol + (WIN_COLS - 1)
    sel_c = ((dcol[None] == np.arange(n_dc)[:, None, None]) & col_ok[None]).astype(np.float32)
    sel_c = sel_c.reshape(n_dc, GRID_W * GRID_W)
    qr, kr = np.arange(rpt)[:, None], np.arange(wrows)[None, :]
    n_lat = rows // rpt
    sel_r, valid = [], []
    for il in (0, min(1, n_lat - 1), n_lat - 1):
        r_abs = il * rpt + qr
        k_abs = int(np.clip(il - 1, 0, n_lat - NA_WIN_TILES)) * rpt + kr
        kr0 = np.clip(r_abs - WIN_ROWS // 2, 0, rows - WIN_ROWS)
        row_ok = (k_abs >= kr0) & (k_abs < kr0 + WIN_ROWS)
        drow = k_abs - r_abs + (WIN_ROWS - 1)
        sel_r.append(((drow[:, :, None] == np.arange(n_dr)) & row_ok[:, :, None]).reshape(rpt * wrows, n_dr))
        valid.append((row_ok[:, None, :, None] & col_ok[None, :, None, :]).reshape(TILE, NA_WIN_TILES * TILE))
    sel_r = np.stack(sel_r).astype(np.float32)
    by_row = jnp.einsum("kpd,lhdc->lhkpc", sel_r, rel_bias.astype(F32) * math.log2(math.e),
                        precision=lax.Precision.HIGHEST)
    vals = jnp.einsum("lhkpc,cx->lhkpx", by_row, sel_c, precision=lax.Precision.HIGHEST)
    vals = vals.reshape(n_layers, n_heads, 3, rpt, wrows, GRID_W, GRID_W).transpose(0, 1, 2, 3, 5, 4, 6)
    vals = vals.reshape(n_layers, n_heads, 3, TILE, NA_WIN_TILES * TILE)
    tab = jnp.where(jnp.asarray(np.stack(valid))[None, None], vals, NEG)
    return jnp.concatenate([tab, jnp.full_like(tab[:, :, :1], NEG)], axis=2)


def _gdn_prep_kernel(x_ref, prev_ref, next_ref, ab_ref, cw_ref, alog_ref, dtb_ref,
                     q_ref, k_ref, v_ref, sc_ref, sct_ref, *, n_ctx_tiles, n_tiles, n_heads):
    i = pl.program_id(1)
    first = jnp.logical_or(i == 0, i == n_ctx_tiles)
    last = jnp.logical_or(i == n_ctx_tiles - 1, i == n_tiles - 1)
    prev = jnp.where(first, 0.0, prev_ref[0, 0])
    nxt = jnp.where(last, 0.0, next_ref[0, 0])
    ext = jnp.concatenate([prev, x_ref[0], nxt], axis=0)
    cw = cw_ref[...]
    y = None
    for j in range(CONV_W):
        o = SUBLANES - CONV_W // 2 + j
        term = ext[o:o + TILE] * cw[j:j + 1]
        y = term if y is None else y + term
    y = _silu(y)
    b_w = n_heads * HEAD_DIM
    lo = lax.broadcasted_iota(jnp.int32, (TILE, LANES), 1) < HEAD_DIM

    def l2norm_heads(z):
        parts = []
        for p in range(z.shape[1] // LANES):
            zp = z[:, LANES * p:LANES * (p + 1)]
            sq = zp * zp
            s_lo = jnp.sum(jnp.where(lo, sq, 0.0), axis=-1, keepdims=True)
            s_hi = jnp.sum(jnp.where(lo, 0.0, sq), axis=-1, keepdims=True)
            parts.append(zp * lax.rsqrt(jnp.where(lo, s_lo, s_hi) + EPS))
        return jnp.concatenate(parts, axis=1)

    q_ref[0] = l2norm_heads(y[:, :b_w]) * (HEAD_DIM ** -0.5)
    k_ref[0] = l2norm_heads(y[:, b_w:2 * b_w])
    v_ref[0] = y[:, 2 * b_w:]

    ab = ab_ref[0]
    lane = lax.broadcasted_iota(jnp.int32, ab.shape, 1)
    z = ab + dtb_ref[...]
    softplus = jnp.maximum(z, 0.0) + jnp.log(1.0 + jnp.exp(-jnp.abs(z)))
    g = jnp.where(lane < 2 * n_heads, -jnp.exp(alog_ref[...]) * softplus, 0.0)
    beta = 1.0 / (1.0 + jnp.exp(-ab))
    r = lax.broadcasted_iota(jnp.int32, (TILE, TILE), 0)
    c = lax.broadcasted_iota(jnp.int32, (TILE, TILE), 1)
    same = (r // GDN_CHUNK) == (c // GDN_CHUNK)
    tri_f = jnp.where(jnp.logical_and(same, c <= r), 1.0, 0.0).astype(BF16)
    tri_b = jnp.where(jnp.logical_and(same, c >= r), 1.0, 0.0).astype(BF16)
    gc = jnp.where(lane < n_heads, _dot_sel_left(tri_f, g), _dot_sel_left(tri_b, g))
    rr = lax.broadcasted_iota(jnp.int32, (LANES, LANES), 0)
    cc = lax.broadcasted_iota(jnp.int32, (LANES, LANES), 1)
    hh, jj = cc // SUBLANES, cc % SUBLANES
    valid = hh < n_heads
    sel_gb = jnp.where(jnp.logical_and(valid, jnp.logical_and(jj < 4, rr == jj * n_heads + hh)), 1.0, 0.0).astype(BF16)
    sel_gc = jnp.where(jnp.logical_and(valid, jnp.logical_and(jnp.logical_and(jj >= 4, jj < 6),
                                                              rr == (jj - 4) * n_heads + hh)), 1.0, 0.0).astype(BF16)
    gb = jnp.where(lane < 2 * n_heads, g, beta)
    packed = _dot_sel_right(gb, sel_gb) + _dot_sel_right(gc, sel_gc)
    packed_t = packed.T
    for h in range(n_heads):
        sc_ref[0, h] = packed[:, SUBLANES * h:SUBLANES * (h + 1)]
        sct_ref[0, h] = packed_t[SUBLANES * h:SUBLANES * (h + 1), :]


def _gdn_prep(qkvb, ab, conv_w, alog_v, dtb_v, *, n_ctx_tiles, n_heads):
    nb, tt, w3 = qkvb.shape
    nt = tt // TILE
    rows8 = TILE // SUBLANES
    x8 = qkvb.reshape(nb, tt // SUBLANES, SUBLANES, w3)
    hd = lambda w: pl.BlockSpec((1, n_heads, TILE, w), lambda b, i: (b, 0, i, 0))
    tok = pl.BlockSpec((1, TILE, w3 // 3), lambda b, i: (b, i, 0))
    return pl.pallas_call(
        functools.partial(_gdn_prep_kernel, n_ctx_tiles=n_ctx_tiles, n_tiles=nt, n_heads=n_heads),
        grid=(nb, nt),
        in_specs=[
            pl.BlockSpec((1, TILE, w3), lambda b, i: (b, i, 0)),
            pl.BlockSpec((1, 1, SUBLANES, w3), lambda b, i: (b, jnp.maximum(i * rows8 - 1, 0), 0, 0)),
            pl.BlockSpec((1, 1, SUBLANES, w3), lambda b, i: (b, jnp.minimum((i + 1) * rows8, nt * rows8 - 1), 0, 0)),
            pl.BlockSpec((1, TILE, LANES), lambda b, i: (b, i, 0)),
            pl.BlockSpec((CONV_W, w3), lambda b, i: (0, 0)),
            pl.BlockSpec((1, LANES), lambda b, i: (0, 0)),
            pl.BlockSpec((1, LANES), lambda b, i: (0, 0)),
        ],
        out_specs=(tok, tok, tok, hd(SUBLANES),
                   pl.BlockSpec((1, n_heads, SUBLANES, TILE), lambda b, i: (b, 0, 0, i))),
        out_shape=(
            jax.ShapeDtypeStruct((nb, tt, w3 // 3), F32),
            jax.ShapeDtypeStruct((nb, tt, w3 // 3), F32),
            jax.ShapeDtypeStruct((nb, tt, w3 // 3), F32),
            jax.ShapeDtypeStruct((nb, n_heads, tt, SUBLANES), F32),
            jax.ShapeDtypeStruct((nb, n_heads, SUBLANES, tt), F32),
        ),
        compiler_params=_cparams(("parallel", "parallel")),
        name="gdn_prep",
    )(qkvb, x8, x8, ab, conv_w, alog_v, dtb_v)


def _bmm(a, b):
    return jnp.einsum("nik,nkj->nij", a, b, preferred_element_type=F32)


def _bmm_nt(a, b):
    return jnp.einsum("nik,njk->nij", a, b, preferred_element_type=F32)


def _pair_blockdiag(x, lo):
    zero = jnp.zeros_like(x)
    return jnp.concatenate([jnp.where(lo, x, zero), jnp.where(lo, zero, x)], axis=1)


def _pmm(p, x, lo, dtype):
    return _bmm(p.astype(dtype), _pair_blockdiag(x.astype(dtype), lo))


def _tri_inverse(lmat, ii, jj, lo):
    blk = 16
    same = (ii // blk) == (jj // blk)
    dmat = jnp.where(same, lmat, 0.0)
    x = jnp.where(ii == jj, 1.0, 0.0) - dmat
    p = _pmm(dmat, dmat, lo, D_GI)
    n_sq = int(math.log2(blk)) - 1
    for step in range(n_sq):
        x = x + _pmm(p, x, lo, D_GI)
        if step < n_sq - 1:
            p = _pmm(p, p, lo, D_GI)
    while blk < GDN_CHUNK:
        wider = (ii // (2 * blk)) == (jj // (2 * blk))
        off = jnp.where(jnp.logical_and(wider, jnp.logical_not(same)), lmat, 0.0)
        x = x - _pmm(x, _pmm(off, x, lo, D_GI), lo, D_GI)
        same = wider
        blk *= 2
    return x


def _gdn_local_kernel(q_ref, k_ref, v_ref, sc_ref, sct_ref,
                      u_ref, w_ref, qh_ref, aqk_ref, ktt_ref, egl_ref):
    c = GDN_CHUNK
    cpt = TILE // c
    n_pairs = q_ref.shape[2] // LANES
    n = n_pairs * 2 * cpt
    lo = lax.broadcasted_iota(jnp.int32, (1, 1, LANES), 2) < HEAD_DIM

    def both(x):
        parts = []
        for p in range(n_pairs):
            x3 = x[:, LANES * p:LANES * (p + 1)].reshape(cpt, c, LANES)
            parts += [x3, x3]
        return jnp.concatenate(parts, axis=0)

    sc = sc_ref[0]

    def per_dir(col):
        parts = []
        for p in range(n_pairs):
            a0 = sc[2 * p].reshape(cpt, c, SUBLANES)
            a1 = sc[2 * p + 1].reshape(cpt, c, SUBLANES)
            for d in range(2):
                parts.append(jnp.where(lo, a0[:, :, col + d:col + d + 1], a1[:, :, col + d:col + d + 1]))
        return jnp.concatenate(parts, axis=0)

    q, k, v = both(q_ref[0]), both(k_ref[0]), both(v_ref[0])
    beta = per_dir(2)
    gcc = per_dir(4)
    sct = sct_ref[0]
    gcr = jnp.stack([jnp.concatenate([sct[2 * p, 4 + d:5 + d, ch * c:(ch + 1) * c],
                                      sct[2 * p + 1, 4 + d:5 + d, ch * c:(ch + 1) * c]], axis=1)
                     for p in range(n_pairs) for d in range(2) for ch in range(cpt)], axis=0)

    ii = lax.broadcasted_iota(jnp.int32, (n, c, LANES), 1)
    jj = lax.broadcasted_iota(jnp.int32, (n, c, LANES), 2) % c
    fwd = (lax.broadcasted_iota(jnp.int32, (n, c, LANES), 0) // cpt) % 2 == 0
    incl = jnp.logical_or(jnp.logical_and(fwd, ii >= jj), jnp.logical_and(jnp.logical_not(fwd), ii <= jj))
    strict = jnp.logical_and(incl, ii != jj)
    decay = jnp.where(incl, jnp.exp(jnp.where(incl, gcc - gcr, 0.0)), 0.0)
    kb = k * beta
    a = _bmm_nt(jnp.concatenate([kb, q], axis=1).astype(D_GA), _pair_blockdiag(k.astype(D_GA), lo))
    lmat = jnp.where(strict, a[:, :c] * decay, 0.0)
    aqk = a[:, c:] * decay
    tinv = _tri_inverse(lmat, ii, jj, lo)
    egc = jnp.exp(gcc)
    u = _pmm(tinv, v * beta, lo, D_GT)
    w = _pmm(tinv, kb * egc, lo, D_GT)
    fwd1 = (lax.broadcasted_iota(jnp.int32, (n, 1, 1), 0) // cpt) % 2 == 0
    glast = jnp.where(fwd1, gcc[:, c - 1:c, :], gcc[:, 0:1, :])
    ktail = k * jnp.exp(glast - gcc)
    eye = jnp.where(ii == jj, 1.0, 0.0).astype(D_GDN)
    ktt = _bmm_nt(eye, _pair_blockdiag(ktail.astype(D_GDN), lo))
    split = lambda z: z.reshape((n_pairs, 2, cpt) + z.shape[1:])
    u_ref[0] = split(u)
    w_ref[0] = split(w.astype(D_GDN))
    qh_ref[0] = split((q * egc).astype(D_GDN))
    aqk_ref[0] = split(aqk.astype(D_GDN))
    ktt_ref[0] = split(ktt.astype(D_GDN))
    egl_ref[0] = split(jnp.exp(glast))


def _gdn_local(qn, kn, vv, sc, sct):
    nb, tt, b_w = qn.shape
    nh = sc.shape[1]
    n_pairs = b_w // LANES
    nt = tt // TILE
    nc = tt // GDN_CHUNK
    cpt = TILE // GDN_CHUNK
    tok = pl.BlockSpec((1, TILE, b_w), lambda b, i: (b, i, 0))
    chunked = lambda r: pl.BlockSpec((1, n_pairs, 2, cpt, r, LANES), lambda b, i: (b, 0, 0, i, 0, 0))
    shape = lambda r, dt: jax.ShapeDtypeStruct((nb, n_pairs, 2, nc, r, LANES), dt)
    return pl.pallas_call(
        _gdn_local_kernel,
        grid=(nb, nt),
        in_specs=[tok, tok, tok,
                  pl.BlockSpec((1, nh, TILE, SUBLANES), lambda b, i: (b, 0, i, 0)),
                  pl.BlockSpec((1, nh, SUBLANES, TILE), lambda b, i: (b, 0, 0, i))],
        out_specs=(chunked(GDN_CHUNK), chunked(GDN_CHUNK), chunked(GDN_CHUNK),
                   chunked(GDN_CHUNK), chunked(HEAD_DIM), chunked(1)),
        out_shape=(
            shape(GDN_CHUNK, F32),
            shape(GDN_CHUNK, D_GDN),
            shape(GDN_CHUNK, D_GDN),
            shape(GDN_CHUNK, D_GDN),
            shape(HEAD_DIM, D_GDN),
            shape(1, F32),
        ),
        compiler_params=_cparams(("parallel", "parallel")),
        name="gdn_local",
    )(qn, kn, vv, sc, sct)


def _gdn_scan_kernel(uf, ub, wf, wb, qf, qb, af, ab, kf, kb, ef, eb, of_ref, ob_ref, s_ref, *, cpt, n_pairs):
    @pl.when(pl.program_id(1) == 0)
    def _():
        s_ref[...] = jnp.zeros_like(s_ref)

    lo = lax.broadcasted_iota(jnp.int32, (1, 1, LANES), 2) < HEAD_DIM
    s = s_ref[...]
    for t in range(cpt):
        tb = cpt - 1 - t
        pair = lambda f, b: jnp.concatenate([f[0, :, 0, t], b[0, :, 0, tb]], axis=0)
        sd = _pair_blockdiag(s.astype(D_GS), lo)
        v_new = pair(uf, ub) - _bmm(pair(wf, wb), sd)
        vd = _pair_blockdiag(v_new.astype(D_GS), lo)
        o = _bmm(pair(qf, qb), sd) + _bmm(pair(af, ab), vd)
        s = s * pair(ef, eb) + _bmm(pair(kf, kb), vd)
        of_ref[0, :, t] = o[:n_pairs]
        ob_ref[0, :, tb] = o[n_pairs:]
    s_ref[...] = s


def _gdn_scan(u, w, qh, aqk, ktt, egl, *, n_ctx_tiles):
    nb, n_pairs, _, nc, c, _ = u.shape
    cpt = TILE // c
    nt = nc // cpt

    def bwd_tile(j):
        return jnp.where(j < n_ctx_tiles, n_ctx_tiles - 1 - j, nt - 1 + n_ctx_tiles - j)

    def specs(a):
        blk = (1, n_pairs, 1, cpt) + a.shape[4:]
        return [pl.BlockSpec(blk, lambda b, j: (b, 0, 0, j, 0, 0)),
                pl.BlockSpec(blk, lambda b, j: (b, 0, 1, bwd_tile(j), 0, 0))]

    args, in_specs = [], []
    for a in (u, w, qh, aqk, ktt, egl):
        args += [a, a]
        in_specs += specs(a)
    oshape = jax.ShapeDtypeStruct((nb, n_pairs, nc, c, LANES), F32)
    return pl.pallas_call(
        functools.partial(_gdn_scan_kernel, cpt=cpt, n_pairs=n_pairs),
        grid=(nb, nt),
        in_specs=in_specs,
        out_specs=(pl.BlockSpec((1, n_pairs, cpt, c, LANES), lambda b, j: (b, 0, j, 0, 0)),
                   pl.BlockSpec((1, n_pairs, cpt, c, LANES), lambda b, j: (b, 0, bwd_tile(j), 0, 0))),
        out_shape=(oshape, oshape),
        scratch_shapes=[pltpu.VMEM((2 * n_pairs, HEAD_DIM, LANES), F32)],
        compiler_params=_cparams(("parallel", "arbitrary")),
        name="gdn_scan",
    )(*args)


def _outproj_kernel(x_ref, mod_ref, ya_ref, of_ref, ob_ref, gate_ref, gnw_ref, yc_ref, w_ref, o_ref):
    lo = lax.broadcasted_iota(jnp.int32, (TILE, LANES), 1) < HEAD_DIM
    pairs = []
    for p in range(of_ref.shape[1]):
        o = of_ref[0, p] + ob_ref[0, p]
        sq = o * o
        ms_lo = jnp.sum(jnp.where(lo, sq, 0.0), axis=-1, keepdims=True) * (1.0 / HEAD_DIM)
        ms_hi = jnp.sum(jnp.where(lo, 0.0, sq), axis=-1, keepdims=True) * (1.0 / HEAD_DIM)
        pairs.append(o * lax.rsqrt(jnp.where(lo, ms_lo, ms_hi) + EPS) * gnw_ref[...])
    yb = (jnp.concatenate(pairs, axis=1) * _silu(gate_ref[0])).astype(D_OUT)
    y = jnp.concatenate([ya_ref[0], yb, yc_ref[0]], axis=1)
    o_ref[0] = x_ref[0] + mod_ref[0][2:3] * _dot(y, w_ref[0])


def _outproj(xs, mods, ya, scan_f, scan_b, gate, gnw2, yc, w, *, layer, tile0, n_batch, n_ctx_tiles):
    nb, tt, d = xs.shape
    nt = tt // TILE - tile0
    a_w, b_w, c_w = ya.shape[2], gate.shape[2], yc.shape[2]
    n_pairs = scan_f.shape[1]

    def mod_idx(b, i):
        return (jnp.where(i + tile0 < n_ctx_tiles, n_batch, b), 0, 0)

    tok = lambda wd: pl.BlockSpec((1, TILE, wd), lambda b, i: (b, i + tile0, 0))
    pairs = pl.BlockSpec((1, n_pairs, TILE, LANES), lambda b, i: (b, 0, i + tile0, 0))
    return pl.pallas_call(
        _outproj_kernel,
        grid=(nb, nt),
        in_specs=[
            tok(d),
            pl.BlockSpec((1, 6, d), mod_idx),
            tok(a_w),
            pairs,
            pairs,
            tok(b_w),
            pl.BlockSpec((1, LANES), lambda b, i: (0, 0)),
            tok(c_w),
            pl.BlockSpec((1,) + w.shape[1:], lambda b, i: (layer, 0, 0)),
        ],
        out_specs=pl.BlockSpec((1, TILE, d), lambda b, i: (b, i, 0)),
        out_shape=jax.ShapeDtypeStruct((nb, nt * TILE, d), F32),
        compiler_params=_cparams(("parallel", "parallel")),
        name="outproj",
    )(xs, mods, ya, scan_f, scan_b, gate, gnw2, yc, w)


def _ffn_kernel(x_ref, mod_ref, nw_ref, wi_ref, wo_ref, fnw_ref, o_ref, *, d_ff, final):
    mod = mod_ref[0]
    x = x_ref[0]
    h = _norm_mod(x, nw_ref[...], mod[3:4], mod[4:5]).astype(D_FFN)
    gu = _dot(h, wi_ref[0])
    a = (_silu(gu[:, :d_ff]) * gu[:, d_ff:]).astype(D_FFN)
    y = x + mod[5:6] * _dot(a, wo_ref[0])
    if final:
        y = y * lax.rsqrt(jnp.mean(y * y, axis=-1, keepdims=True) + EPS) * fnw_ref[...]
    o_ref[0] = y


def _ffn(xs, mods, nw, wi, wo, fnw, *, layer, final, n_batch, n_ctx_tiles):
    nb, tt, d = xs.shape
    nt = tt // TILE
    d_ff = wo.shape[1]

    def mod_idx(b, i):
        return (jnp.where(i < n_ctx_tiles, n_batch, b), 0, 0)

    tok = pl.BlockSpec((1, TILE, d), lambda b, i: (b, i, 0))
    return pl.pallas_call(
        functools.partial(_ffn_kernel, d_ff=d_ff, final=final),
        grid=(nb, nt),
        in_specs=[
            tok,
            pl.BlockSpec((1, 6, d), mod_idx),
            pl.BlockSpec((1, d), lambda b, i: (0, 0)),
            pl.BlockSpec((1,) + wi.shape[1:], lambda b, i: (layer, 0, 0), pipeline_mode=pl.Buffered(1)),
            pl.BlockSpec((1,) + wo.shape[1:], lambda b, i: (layer, 0, 0), pipeline_mode=pl.Buffered(1)),
            pl.BlockSpec((1, d), lambda b, i: (0, 0)),
        ],
        out_specs=tok,
        out_shape=jax.ShapeDtypeStruct((nb, tt, d), F32),
        compiler_params=_cparams(("parallel", "parallel")),
        name="ffn",
    )(xs, mods, nw, wi, wo, fnw)


def _rope_tables(seq, ctx_len, n_rep):
    half = A_QK // 2
    inv_freq = 1.0 / (ROPE_THETA ** (jnp.arange(0, half, 2, dtype=F32) / half))
    t = jnp.arange(seq, dtype=jnp.int32)
    ang_r = (t // GRID_W).astype(F32)[:, None] * inv_freq
    ang_c = (t % GRID_W).astype(F32)[:, None] * inv_freq
    ang = jnp.concatenate([ang_r, ang_r, ang_c, ang_c], axis=-1)
    cos = jnp.concatenate([jnp.ones((ctx_len, A_QK), F32), jnp.cos(ang)], axis=0)
    sin = jnp.concatenate([jnp.zeros((ctx_len, A_QK), F32), jnp.sin(ang)], axis=0)
    return jnp.tile(cos, (1, n_rep)), jnp.tile(sin, (1, n_rep))


def kernel(x, c, ctx, c_ctx, w_mod, b_mod, norm1_w, norm2_w, w_in, w_out, lambda_q1, lambda_k1, lambda_q2,
           lambda_k2, diff_norm_w, conv_w, a_log, dt_bias, gdn_norm_w, na_bias, w_ffn_in, w_ffn_out,
           final_norm_w):
    nb, seq, d = x.shape
    ctx_len = ctx.shape[1]
    depth = w_mod.shape[0]
    b_heads = a_log.shape[-1]
    c_heads = na_bias.shape[1]
    b_w = b_heads * HEAD_DIM
    c_w = c_heads * HEAD_DIM
    a_w = w_out.shape[1] - b_w - c_w
    assert seq % TILE == 0 and ctx_len % TILE == 0 and seq % GRID_W == 0
    assert a_w % LANES == 0 and b_w % LANES == 0 and c_w % LANES == 0 and 4 * b_heads <= LANES
    n_ctx_tiles = ctx_len // TILE

    xs = jnp.concatenate([ctx, x], axis=1)

    n_rows = -(-(nb + 1) // SUBLANES) * SUBLANES
    cc = jnp.concatenate([c, c_ctx[None, :], jnp.zeros((n_rows - nb - 1, d), F32)], axis=0)
    mods_all = _modulation(cc, w_mod, b_mod).reshape(depth, n_rows, 6, d)

    cos, sin = _rope_tables(seq, ctx_len, a_w // A_QK)
    cost, sint = cos.T, sin.T

    sizes = (3 * a_w, 3 * b_w, b_w, 2 * b_heads, 2 * b_heads, 3 * c_w)
    offs = np.concatenate([[0], np.cumsum(sizes)])
    o_a, o_b, o_g, o_al, o_be, o_c = (int(v) for v in offs[:6])
    pad = LANES - 4 * b_heads

    wm_all = jnp.concatenate([
        w_in[:, :, o_a:o_a + a_w],
        w_in[:, :, o_a + 2 * a_w:o_a + 3 * a_w],
        w_in[:, :, o_b:o_b + 3 * b_w],
        w_in[:, :, o_g:o_g + b_w],
        w_in[:, :, o_c:o_c + c_w],
        w_in[:, :, o_c + 2 * c_w:o_c + 3 * c_w],
        w_in[:, :, o_al:o_al + 4 * b_heads],
        jnp.zeros((depth, d, pad), F32),
    ], axis=2).astype(D_IN)
    wk_all = jnp.swapaxes(jnp.concatenate([w_in[:, :, o_a + a_w:o_a + 2 * a_w],
                                           w_in[:, :, o_c + c_w:o_c + 2 * c_w]], axis=2), 1, 2).astype(D_IN)
    w_out_all = w_out.astype(D_OUT)
    w_ffn_in_all = w_ffn_in.astype(D_FFN)
    w_ffn_out_all = w_ffn_out.astype(D_FFN)
    bias_all = _na_bias_table(na_bias, seq)
    nw2_all = jnp.tile(diff_norm_w, (1, LANES // HEAD_DIM))
    gnw2_all = jnp.tile(gdn_norm_w, (1, LANES // HEAD_DIM))
    zpad = jnp.zeros((depth, LANES - 2 * b_heads), F32)
    alog_all = jnp.concatenate([a_log.reshape(depth, -1), zpad], axis=1)
    dtb_all = jnp.concatenate([dt_bias.reshape(depth, -1), zpad], axis=1)
    fnw = final_norm_w[None, :]

    for l in range(depth):
        last = l == depth - 1
        mods = mods_all[l]
        lam_init = 0.8 - 0.6 * math.exp(-0.3 * l)

        qa, kat, va, qkvb, gate, ab, qc, kct, vc = _inproj(
            xs, mods, norm1_w[l][None, :], wm_all, wk_all, cos, sin, cost, sint,
            layer=l, n_batch=nb, n_ctx_tiles=n_ctx_tiles, a_w=a_w, b_w=b_w, c_w=c_w)

        lam_vecs = jnp.stack([lambda_q1[l], lambda_k1[l], lambda_q2[l], lambda_k2[l]], axis=0).astype(F32)
        ya = _diffattn(lam_vecs, nw2_all[l][None, :], qa, kat, va, n_ctx_tiles=n_ctx_tiles, lam_init=lam_init)

        yc = _na(qc, kct, vc, bias_all, layer=l, n_ctx_tiles=n_ctx_tiles)

        qn, kn, vv, sc, sct = _gdn_prep(qkvb, ab, conv_w[l], alog_all[l][None, :], dtb_all[l][None, :],
                                        n_ctx_tiles=n_ctx_tiles, n_heads=b_heads)
        u, w, qh, aqk, ktt, egl = _gdn_local(qn, kn, vv, sc, sct)
        scan_f, scan_b = _gdn_scan(u, w, qh, aqk, ktt, egl, n_ctx_tiles=n_ctx_tiles)
        scan_f = scan_f.reshape(nb, b_w // LANES, ctx_len + seq, LANES)
        scan_b = scan_b.reshape(nb, b_w // LANES, ctx_len + seq, LANES)

        tile0 = n_ctx_tiles if last else 0
        xs = _outproj(xs, mods, ya, scan_f, scan_b, gate, gnw2_all[l][None, :], yc, w_out_all,
                      layer=l, tile0=tile0, n_batch=nb, n_ctx_tiles=n_ctx_tiles)
        xs = _ffn(xs, mods, norm2_w[l][None, :], w_ffn_in_all, w_ffn_out_all, fnw,
                  layer=l, final=last, n_batch=nb, n_ctx_tiles=n_ctx_tiles - tile0)

    return xs
```

```python
import functools
import math

import jax
import jax.numpy as jnp
import numpy as np
from jax import lax
from jax.experimental import pallas as pl
from jax.experimental.pallas import tpu as pltpu

F32 = jnp.float32
BF16 = jnp.bfloat16
D_MOD = D_IN = D_OUT = D_FFN = D_ATT = D_NA = D_GDN = BF16
D_GA = D_GI = D_GT = D_GS = BF16

HEAD_DIM = 64
A_QK = HEAD_DIM // 2
GRID_W = 64
ROPE_THETA = 10000.0
CONV_W = 5
GDN_CHUNK = 64
WIN_ROWS = 8
WIN_COLS = 16
EPS = 1e-6

LANES = 128
SUBLANES = 8
TILE = 256
NA_WIN_TILES = 3
NEG = -1e30
VMEM_LIMIT = 56 * 1024 * 1024


def _cparams(sem):
    return pltpu.CompilerParams(dimension_semantics=sem, vmem_limit_bytes=VMEM_LIMIT)


def _silu(x):
    return x * (1.0 / (1.0 + jnp.exp(-x)))


def _dot(a, b):
    return jnp.dot(a, b, preferred_element_type=F32)


def _dot_nt(a, b):
    return lax.dot_general(a, b, (((1,), (1,)), ((), ())), preferred_element_type=F32)


def _split3(x):
    hi = x.astype(BF16)
    r1 = x - hi.astype(F32)
    mid = r1.astype(BF16)
    lo = (r1 - mid.astype(F32)).astype(BF16)
    return hi, mid, lo


def _dot_sel_right(x, sel):
    hi, mid, lo = _split3(x)
    return _dot(hi, sel) + _dot(mid, sel) + _dot(lo, sel)


def _dot_sel_left(sel, x):
    hi, mid, lo = _split3(x)
    return _dot(sel, hi) + _dot(sel, mid) + _dot(sel, lo)


def _mod_kernel(c_ref, w_ref, b_ref, o_ref):
    s = _silu(c_ref[...])
    o_ref[0] = _dot(s.astype(D_MOD), w_ref[0].astype(D_MOD)) + b_ref[0]


def _modulation(cc, w_mod, b_mod):
    depth, d, n = w_mod.shape
    rows = cc.shape[0]
    tn = 1536
    assert n % tn == 0
    return pl.pallas_call(
        _mod_kernel,
        grid=(depth, n // tn),
        in_specs=[
            pl.BlockSpec((rows, d), lambda l, j: (0, 0)),
            pl.BlockSpec((1, d, tn), lambda l, j: (l, 0, j)),
            pl.BlockSpec((1, 1, tn), lambda l, j: (l, 0, j)),
        ],
        out_specs=pl.BlockSpec((1, rows, tn), lambda l, j: (l, 0, j)),
        out_shape=jax.ShapeDtypeStruct((depth, rows, n), F32),
        compiler_params=_cparams(("parallel", "parallel")),
        name="modulation",
    )(cc, w_mod, b_mod.reshape(depth, 1, n))


def _norm_mod(x, nw, shift, scale):
    ms = jnp.mean(x * x, axis=-1, keepdims=True)
    y = x * lax.rsqrt(ms + EPS) * nw
    return y * (1.0 + scale) + shift


def _rope_lanes(x, cos, sin):
    parts = []
    for j in range(x.shape[1] // LANES):
        xj = x[:, LANES * j:LANES * (j + 1)]
        nxt = pltpu.roll(xj, LANES - 8, axis=1)
        prv = pltpu.roll(xj, 8, axis=1)
        lane = lax.broadcasted_iota(jnp.int32, xj.shape, 1)
        parts.append(jnp.where((lane % 16) < 8, -nxt, prv))
    rot = jnp.concatenate(parts, axis=1)
    return x * cos + rot * sin


def _rope_rows(x, cos, sin):
    n = x.shape[0]
    nxt = pltpu.roll(x, n - 8, axis=0)
    prv = pltpu.roll(x, 8, axis=0)
    row = lax.broadcasted_iota(jnp.int32, x.shape, 0)
    rot = jnp.where((row % 16) < 8, -nxt, prv)
    return x * cos + rot * sin


def _inproj_kernel(x_ref, mod_ref, nw_ref, wm_ref, wk_ref, cos_ref, sin_ref, cost_ref, sint_ref,
                   qa_ref, kat_ref, va_ref, qkvb_ref, gate_ref, ab_ref, qc_ref, kct_ref, vc_ref,
                   *, a_w, b_w, c_w):
    mod = mod_ref[0]
    h = _norm_mod(x_ref[0], nw_ref[...], mod[0:1], mod[1:2]).astype(D_IN)
    acc = _dot(h, wm_ref[0])
    kt = _dot_nt(wk_ref[0], h)
    o = 0
    qa = _rope_lanes(acc[:, o:o + a_w], cos_ref[...], sin_ref[...])
    qa_ref[0] = (qa * (A_QK ** -0.5 * math.log2(math.e))).astype(D_ATT)
    o += a_w
    va_ref[0] = acc[:, o:o + a_w].astype(D_ATT)
    o += a_w
    qkvb_ref[0] = acc[:, o:o + 3 * b_w]
    o += 3 * b_w
    gate_ref[0] = acc[:, o:o + b_w]
    o += b_w
    qc_ref[0] = (acc[:, o:o + c_w] * (HEAD_DIM ** -0.5 * math.log2(math.e))).astype(D_NA)
    o += c_w
    vc_ref[0] = acc[:, o:o + c_w].astype(D_NA)
    o += c_w
    ab_ref[0] = acc[:, o:o + LANES]
    kat_ref[0, 0] = _rope_rows(kt[:a_w], cost_ref[...], sint_ref[...]).astype(D_ATT)
    kct_ref[0, 0] = kt[a_w:].astype(D_NA)


def _inproj(xs, mods, nw, wm, wk, cos, sin, cost, sint, *, layer, n_batch, n_ctx_tiles, a_w, b_w, c_w):
    nb, tt, d = xs.shape
    nt = tt // TILE
    ctx_row = n_batch

    def mod_idx(b, i):
        return (jnp.where(i < n_ctx_tiles, ctx_row, b), 0, 0)

    tok = lambda w: pl.BlockSpec((1, TILE, w), lambda b, i: (b, i, 0))
    tokt = lambda w: pl.BlockSpec((1, 1, w, TILE), lambda b, i: (b, i, 0, 0))
    out_shape = (
        jax.ShapeDtypeStruct((nb, tt, a_w), D_ATT),
        jax.ShapeDtypeStruct((nb, nt, a_w, TILE), D_ATT),
        jax.ShapeDtypeStruct((nb, tt, a_w), D_ATT),
        jax.ShapeDtypeStruct((nb, tt, 3 * b_w), F32),
        jax.ShapeDtypeStruct((nb, tt, b_w), F32),
        jax.ShapeDtypeStruct((nb, tt, LANES), F32),
        jax.ShapeDtypeStruct((nb, tt, c_w), D_NA),
        jax.ShapeDtypeStruct((nb, nt, c_w, TILE), D_NA),
        jax.ShapeDtypeStruct((nb, tt, c_w), D_NA),
    )
    return pl.pallas_call(
        functools.partial(_inproj_kernel, a_w=a_w, b_w=b_w, c_w=c_w),
        grid=(nb, nt),
        in_specs=[
            tok(d),
            pl.BlockSpec((1, 6, d), mod_idx),
            pl.BlockSpec((1, d), lambda b, i: (0, 0)),
            pl.BlockSpec((1,) + wm.shape[1:], lambda b, i: (layer, 0, 0)),
            pl.BlockSpec((1,) + wk.shape[1:], lambda b, i: (layer, 0, 0)),
            pl.BlockSpec((TILE, a_w), lambda b, i: (i, 0)),
            pl.BlockSpec((TILE, a_w), lambda b, i: (i, 0)),
            pl.BlockSpec((a_w, TILE), lambda b, i: (0, i)),
            pl.BlockSpec((a_w, TILE), lambda b, i: (0, i)),
        ],
        out_specs=(tok(a_w), tokt(a_w), tok(a_w), tok(3 * b_w), tok(b_w), tok(LANES),
                   tok(c_w), tokt(c_w), tok(c_w)),
        out_shape=out_shape,
        compiler_params=_cparams(("parallel", "parallel")),
        name="inproj",
    )(xs, mods, nw, wm, wk, cos, sin, cost, sint)


def _diffattn_kernel(lam_ref, nw_ref, q_ref, kt_ref, v_ref, o_ref, mx_ref, acc_ref,
                     *, n_ctx_tiles, n_key_tiles, lam_init):
    i = pl.program_id(2)
    lv = lam_ref[...]
    lam = (jnp.exp(jnp.sum(lv[0:1] * lv[1:2], axis=-1, keepdims=True))
           - jnp.exp(jnp.sum(lv[2:3] * lv[3:4], axis=-1, keepdims=True)) + lam_init)
    q = q_ref[0]
    tq = q.shape[0]
    lane = lax.broadcasted_iota(jnp.int32, (tq, LANES), 1)
    lo = lane < HEAD_DIM
    zero = jnp.zeros_like(q)
    qm = [jnp.where((lane // A_QK) == m, q, zero) for m in range(LANES // A_QK)]
    n_maps = len(qm)
    n_lat = n_key_tiles - n_ctx_tiles
    group = max(u for u in (16, 8, 4, 2, 1) if n_lat % u == 0)
    ctx_tiles = list(range(n_ctx_tiles))
    lat_tiles = lambda g: [n_ctx_tiles + g * group + u for u in range(group)]

    mx_ref[...] = jnp.full(mx_ref.shape, NEG, F32)
    acc_ref[...] = jnp.zeros(acc_ref.shape, F32)
    lane_v = lax.broadcasted_iota(jnp.int32, (TILE, LANES), 1)

    def visit(tiles):
        ktiles = [kt_ref[0, j] for j in tiles]
        v_ext = []
        for j in tiles:
            start = j * TILE if isinstance(j, int) else pl.multiple_of(j * TILE, TILE)
            vtile = v_ref[0, pl.ds(start, TILE), :]
            ones = jnp.ones_like(vtile)
            v_ext.append((jnp.where(lane_v < HEAD_DIM, vtile, ones), jnp.where(lane_v < HEAD_DIM, ones, vtile)))
        scores = [[_dot(qm[m], ktile) for ktile in ktiles] for m in range(n_maps)]
        m_new = []
        for m in range(n_maps):
            top = scores[m][0]
            for s in scores[m][1:]:
                top = jnp.maximum(top, s)
            top = jnp.max(jnp.maximum(top[:, :LANES], top[:, LANES:]), axis=-1, keepdims=True)
            m_new.append(jnp.maximum(mx_ref[m], top))
        probs = [[jnp.exp2(s - m_new[m]).astype(D_ATT) for s in scores[m]] for m in range(n_maps)]
        for m in range(n_maps):
            acc = acc_ref[m] * jnp.exp2(mx_ref[m] - m_new[m])
            for p, ve in zip(probs[m], v_ext):
                acc = acc + _dot(p, ve[m // 2])
            mx_ref[m] = m_new[m]
            acc_ref[m] = acc

    @pl.when(i < n_ctx_tiles)
    def _():
        visit(ctx_tiles)

    @pl.when(i >= n_ctx_tiles)
    def _():
        visit(ctx_tiles + lat_tiles(0))
        lax.fori_loop(1, n_lat // group, lambda g, c: (visit(lat_tiles(g)), c)[1], 0)
    attn = []
    for m in range(n_maps):
        a = acc_ref[m]
        attn.append(a / pltpu.roll(a, HEAD_DIM, axis=1))
    o = jnp.where(lo, attn[0] - lam * attn[1], attn[2] - lam * attn[3])
    sq = o * o
    ms_lo = jnp.sum(jnp.where(lo, sq, 0.0), axis=-1, keepdims=True) * (1.0 / HEAD_DIM)
    ms_hi = jnp.sum(jnp.where(lo, 0.0, sq), axis=-1, keepdims=True) * (1.0 / HEAD_DIM)
    y = o * lax.rsqrt(jnp.where(lo, ms_lo, ms_hi) + EPS) * nw_ref[...]
    o_ref[0] = (y * (1.0 - lam_init)).astype(D_OUT)


def _diffattn(lam_vecs, nw2, qa, kat, va, *, n_ctx_tiles, lam_init):
    nb, tt, a_w = qa.shape
    nt = tt // TILE
    return pl.pallas_call(
        functools.partial(_diffattn_kernel, n_ctx_tiles=n_ctx_tiles, n_key_tiles=nt, lam_init=lam_init),
        grid=(nb, a_w // LANES, nt),
        in_specs=[
            pl.BlockSpec(lam_vecs.shape, lambda b, p, i: (0, 0)),
            pl.BlockSpec((1, LANES), lambda b, p, i: (0, 0)),
            pl.BlockSpec((1, TILE, LANES), lambda b, p, i: (b, i, p)),
            pl.BlockSpec((1, nt, LANES, TILE), lambda b, p, i: (b, 0, p, 0)),
            pl.BlockSpec((1, tt, LANES), lambda b, p, i: (b, 0, p)),
        ],
        out_specs=pl.BlockSpec((1, TILE, LANES), lambda b, p, i: (b, i, p)),
        out_shape=jax.ShapeDtypeStruct((nb, tt, a_w), D_OUT),
        scratch_shapes=[pltpu.VMEM((LANES // A_QK, TILE, 1), F32),
                        pltpu.VMEM((LANES // A_QK, TILE, LANES), F32)],
        compiler_params=_cparams(("parallel", "parallel", "parallel")),
        name="diffattn",
    )(lam_vecs, nw2, qa, kat, va)


def _na_kernel(q_ref, kt_ref, v_ref, bias_ref, o_ref, *, n_ctx_tiles, n_key_tiles):
    i = pl.program_id(1)
    n_lat = n_key_tiles - n_ctx_tiles
    w0 = n_ctx_tiles + jnp.clip(i - n_ctx_tiles - 1, 0, n_lat - NA_WIN_TILES)
    tq = q_ref.shape[1]
    n_pairs = q_ref.shape[2] // LANES
    lane = lax.broadcasted_iota(jnp.int32, (tq, LANES), 1)
    lo = lane < HEAD_DIM
    heads = [(p, hd) for p in range(n_pairs) for hd in range(2)]
    qh = []
    for p in range(n_pairs):
        q = q_ref[0, :, LANES * p:LANES * (p + 1)]
        zero = jnp.zeros_like(q)
        qh += [jnp.where(lo, q, zero), jnp.where(lo, zero, q)]
    tiles = [w0 + c for c in range(NA_WIN_TILES)] + list(range(n_ctx_tiles))
    scores = []
    for h, (p, hd) in enumerate(heads):
        row = []
        for c, t in enumerate(tiles):
            s = _dot(qh[h], kt_ref[0, t, LANES * p:LANES * (p + 1), :])
            if c < NA_WIN_TILES:
                s = s + bias_ref[0, h, 0, :, TILE * c:TILE * (c + 1)]
            row.append(s)
        scores.append(row)
    tops = []
    for row in scores:
        top = row[0]
        for s in row[1:]:
            top = jnp.maximum(top, s)
        tops.append(jnp.max(jnp.maximum(top[:, :LANES], top[:, LANES:]), axis=-1, keepdims=True))
    probs = [[jnp.exp2(s - top).astype(D_NA) for s in row] for row, top in zip(scores, tops)]
    lane_v = lax.broadcasted_iota(jnp.int32, (TILE, LANES), 1)
    accs = [jnp.zeros((tq, LANES), F32) for _ in heads]
    for c, t in enumerate(tiles):
        start = t * TILE if isinstance(t, int) else pl.multiple_of(t * TILE, TILE)
        for p in range(n_pairs):
            vt = v_ref[0, pl.ds(start, TILE), LANES * p:LANES * (p + 1)]
            ones = jnp.ones_like(vt)
            v_ext = (jnp.where(lane_v < HEAD_DIM, vt, ones), jnp.where(lane_v < HEAD_DIM, ones, vt))
            for hd in range(2):
                h = 2 * p + hd
                accs[h] = accs[h] + _dot(probs[h][c], v_ext[hd])
    outs = [a / pltpu.roll(a, HEAD_DIM, axis=1) for a in accs]
    o_ref[0] = jnp.concatenate([jnp.where(lo, outs[2 * p], outs[2 * p + 1]) for p in range(n_pairs)],
                               axis=1).astype(D_OUT)


def _na(qc, kct, vc, bias, *, layer, n_ctx_tiles):
    nb, tt, c_w = qc.shape
    nt = tt // TILE
    n_lat = nt - n_ctx_tiles

    def bias_idx(b, i):
        il = i - n_ctx_tiles
        cls = jnp.where(il < 0, 3, jnp.where(il == 0, 0, jnp.where(il == n_lat - 1, 2, 1)))
        return (layer, 0, cls, 0, 0)

    return pl.pallas_call(
        functools.partial(_na_kernel, n_ctx_tiles=n_ctx_tiles, n_key_tiles=nt),
        grid=(nb, nt),
        in_specs=[
            pl.BlockSpec((1, TILE, c_w), lambda b, i: (b, i, 0)),
            pl.BlockSpec((1, nt, c_w, TILE), lambda b, i: (b, 0, 0, 0)),
            pl.BlockSpec((1, tt, c_w), lambda b, i: (b, 0, 0)),
            pl.BlockSpec((1, bias.shape[1], 1, TILE, NA_WIN_TILES * TILE), bias_idx),
        ],
        out_specs=pl.BlockSpec((1, TILE, c_w), lambda b, i: (b, i, 0)),
        out_shape=jax.ShapeDtypeStruct((nb, tt, c_w), D_OUT),
        compiler_params=_cparams(("parallel", "parallel")),
        name="nbr_attn",
    )(qc, kct, vc, bias)


def _na_bias_table(rel_bias, seq):
    rows = seq // GRID_W
    rpt = TILE // GRID_W
    wrows = NA_WIN_TILES * rpt
    assert rows >= wrows and rows >= WIN_ROWS
    n_layers, n_heads = rel_bias.shape[:2]
    n_dr, n_dc = 2 * WIN_ROWS - 1, 2 * WIN_COLS - 1
    qcol, kcol = np.arange(GRID_W)[:, None], np.arange(GRID_W)[None, :]
    kc0 = np.clip(qcol - WIN_COLS // 2, 0, GRID_W - WIN_COLS)
    col_ok = (kcol >= kc0) & (kcol < kc0 + WIN_COLS)
    dcol = kcol - qcol + (WIN_COLS - 1)
    sel_c = ((dcol[None] == np.arange(n_dc)[:, None, None]) & col_ok[None]).astype(np.float32)
    sel_c = sel_c.reshape(n_dc, GRID_W * GRID_W)
    qr, kr = np.arange(rpt)[:, None], np.arange(wrows)[None, :]
    n_lat = rows // rpt
    sel_r, valid = [], []
    for il in (0, min(1, n_lat - 1), n_lat - 1):
        r_abs = il * rpt + qr
        k_abs = int(np.clip(il - 1, 0, n_lat - NA_WIN_TILES)) * rpt + kr
        kr0 = np.clip(r_abs - WIN_ROWS // 2, 0, rows - WIN_ROWS)
        row_ok = (k_abs >= kr0) & (k_abs < kr0 + WIN_ROWS)
        drow = k_abs - r_abs + (WIN_ROWS - 1)
        sel_r.append(((drow[:, :, None] == np.arange(n_dr)) & row_ok[:, :, None]).reshape(rpt * wrows, n_dr))
        valid.append((row_ok[:, None, :, None] & col_ok[None, :, None, :]).reshape(TILE, NA_WIN_TILES * TILE))
    sel_r = np.stack(sel_r).astype(np.float32)
    by_row = jnp.einsum("kpd,lhdc->lhkpc", sel_r, rel_bias.astype(F32) * math.log2(math.e),
                        precision=lax.Precision.HIGHEST)
    vals = jnp.einsum("lhkpc,cx->lhkpx", by_row, sel_c, precision=lax.Precision.HIGHEST)
    vals = vals.reshape(n_layers, n_heads, 3, rpt, wrows, GRID_W, GRID_W).transpose(0, 1, 2, 3, 5, 4, 6)
    vals = vals.reshape(n_layers, n_heads, 3, TILE, NA_WIN_TILES * TILE)
    tab = jnp.where(jnp.asarray(np.stack(valid))[None, None], vals, NEG)
    return jnp.concatenate([tab, jnp.full_like(tab[:, :, :1], NEG)], axis=2)


def _gdn_prep_kernel(x_ref, prev_ref, next_ref, ab_ref, cw_ref, alog_ref, dtb_ref,
                     q_ref, k_ref, v_ref, sc_ref, sct_ref, *, n_ctx_tiles, n_tiles, n_heads):
    i = pl.program_id(1)
    first = jnp.logical_or(i == 0, i == n_ctx_tiles)
    last = jnp.logical_or(i == n_ctx_tiles - 1, i == n_tiles - 1)
    prev = jnp.where(first, 0.0, prev_ref[0, 0])
    nxt = jnp.where(last, 0.0, next_ref[0, 0])
    ext = jnp.concatenate([prev, x_ref[0], nxt], axis=0)
    cw = cw_ref[...]
    y = None
    for j in range(CONV_W):
        o = SUBLANES - CONV_W // 2 + j
        term = ext[o:o + TILE] * cw[j:j + 1]
        y = term if y is None else y + term
    y = _silu(y)
    b_w = n_heads * HEAD_DIM
    lo = lax.broadcasted_iota(jnp.int32, (TILE, LANES), 1) < HEAD_DIM

    def l2norm_heads(z):
        parts = []
        for p in range(z.shape[1] // LANES):
            zp = z[:, LANES * p:LANES * (p + 1)]
            sq = zp * zp
            s_lo = jnp.sum(jnp.where(lo, sq, 0.0), axis=-1, keepdims=True)
            s_hi = jnp.sum(jnp.where(lo, 0.0, sq), axis=-1, keepdims=True)
            parts.append(zp * lax.rsqrt(jnp.where(lo, s_lo, s_hi) + EPS))
        return jnp.concatenate(parts, axis=1)

    q_ref[0] = l2norm_heads(y[:, :b_w]) * (HEAD_DIM ** -0.5)
    k_ref[0] = l2norm_heads(y[:, b_w:2 * b_w])
    v_ref[0] = y[:, 2 * b_w:]

    ab = ab_ref[0]
    lane = lax.broadcasted_iota(jnp.int32, ab.shape, 1)
    z = ab + dtb_ref[...]
    softplus = jnp.maximum(z, 0.0) + jnp.log(1.0 + jnp.exp(-jnp.abs(z)))
    g = jnp.where(lane < 2 * n_heads, -jnp.exp(alog_ref[...]) * softplus, 0.0)
    beta = 1.0 / (1.0 + jnp.exp(-ab))
    r = lax.broadcasted_iota(jnp.int32, (TILE, TILE), 0)
    c = lax.broadcasted_iota(jnp.int32, (TILE, TILE), 1)
    same = (r // GDN_CHUNK) == (c // GDN_CHUNK)
    tri_f = jnp.where(jnp.logical_and(same, c <= r), 1.0, 0.0).astype(BF16)
    tri_b = jnp.where(jnp.logical_and(same, c >= r), 1.0, 0.0).astype(BF16)
    gc = jnp.where(lane < n_heads, _dot_sel_left(tri_f, g), _dot_sel_left(tri_b, g))
    rr = lax.broadcasted_iota(jnp.int32, (LANES, LANES), 0)
    cc = lax.broadcasted_iota(jnp.int32, (LANES, LANES), 1)
    hh, jj = cc // SUBLANES, cc % SUBLANES
    valid = hh < n_heads
    sel_gb = jnp.where(jnp.logical_and(valid, jnp.logical_and(jj < 4, rr == jj * n_heads + hh)), 1.0, 0.0).astype(BF16)
    sel_gc = jnp.where(jnp.logical_and(valid, jnp.logical_and(jnp.logical_and(jj >= 4, jj < 6),
                                                              rr == (jj - 4) * n_heads + hh)), 1.0, 0.0).astype(BF16)
    gb = jnp.where(lane < 2 * n_heads, g, beta)
    packed = _dot_sel_right(gb, sel_gb) + _dot_sel_right(gc, sel_gc)
    packed_t = packed.T
    for h in range(n_heads):
        sc_ref[0, h] = packed[:, SUBLANES * h:SUBLANES * (h + 1)]
        sct_ref[0, h] = packed_t[SUBLANES * h:SUBLANES * (h + 1), :]


def _gdn_prep(qkvb, ab, conv_w, alog_v, dtb_v, *, n_ctx_tiles, n_heads):
    nb, tt, w3 = qkvb.shape
    nt = tt // TILE
    rows8 = TILE // SUBLANES
    x8 = qkvb.reshape(nb, tt // SUBLANES, SUBLANES, w3)
    hd = lambda w: pl.BlockSpec((1, n_heads, TILE, w), lambda b, i: (b, 0, i, 0))
    tok = pl.BlockSpec((1, TILE, w3 // 3), lambda b, i: (b, i, 0))
    return pl.pallas_call(
        functools.partial(_gdn_prep_kernel, n_ctx_tiles=n_ctx_tiles, n_tiles=nt, n_heads=n_heads),
        grid=(nb, nt),
        in_specs=[
            pl.BlockSpec((1, TILE, w3), lambda b, i: (b, i, 0)),
            pl.BlockSpec((1, 1, SUBLANES, w3), lambda b, i: (b, jnp.maximum(i * rows8 - 1, 0), 0, 0)),
            pl.BlockSpec((1, 1, SUBLANES, w3), lambda b, i: (b, jnp.minimum((i + 1) * rows8, nt * rows8 - 1), 0, 0)),
            pl.BlockSpec((1, TILE, LANES), lambda b, i: (b, i, 0)),
            pl.BlockSpec((CONV_W, w3), lambda b, i: (0, 0)),
            pl.BlockSpec((1, LANES), lambda b, i: (0, 0)),
            pl.BlockSpec((1, LANES), lambda b, i: (0, 0)),
        ],
        out_specs=(tok, tok, tok, hd(SUBLANES),
                   pl.BlockSpec((1, n_heads, SUBLANES, TILE), lambda b, i: (b, 0, 0, i))),
        out_shape=(
            jax.ShapeDtypeStruct((nb, tt, w3 // 3), F32),
            jax.ShapeDtypeStruct((nb, tt, w3 // 3), F32),
            jax.ShapeDtypeStruct((nb, tt, w3 // 3), F32),
            jax.ShapeDtypeStruct((nb, n_heads, tt, SUBLANES), F32),
            jax.ShapeDtypeStruct((nb, n_heads, SUBLANES, tt), F32),
        ),
        compiler_params=_cparams(("parallel", "parallel")),
        name="gdn_prep",
    )(qkvb, x8, x8, ab, conv_w, alog_v, dtb_v)


def _bmm(a, b):
    return jnp.einsum("nik,nkj->nij", a, b, preferred_element_type=F32)


def _bmm_nt(a, b):
    return jnp.einsum("nik,njk->nij", a, b, preferred_element_type=F32)


def _pair_blockdiag(x, lo):
    zero = jnp.zeros_like(x)
    return jnp.concatenate([jnp.where(lo, x, zero), jnp.where(lo, zero, x)], axis=1)


def _pmm(p, x, lo, dtype):
    return _bmm(p.astype(dtype), _pair_blockdiag(x.astype(dtype), lo))


def _tri_inverse(lmat, ii, jj, lo):
    blk = 16
    same = (ii // blk) == (jj // blk)
    dmat = jnp.where(same, lmat, 0.0)
    x = jnp.where(ii == jj, 1.0, 0.0) - dmat
    p = _pmm(dmat, dmat, lo, D_GI)
    n_sq = int(math.log2(blk)) - 1
    for step in range(n_sq):
        x = x + _pmm(p, x, lo, D_GI)
        if step < n_sq - 1:
            p = _pmm(p, p, lo, D_GI)
    while blk < GDN_CHUNK:
        wider = (ii // (2 * blk)) == (jj // (2 * blk))
        off = jnp.where(jnp.logical_and(wider, jnp.logical_not(same)), lmat, 0.0)
        x = x - _pmm(x, _pmm(off, x, lo, D_GI), lo, D_GI)
        same = wider
        blk *= 2
    return x


def _gdn_local_kernel(q_ref, k_ref, v_ref, sc_ref, sct_ref,
                      u_ref, w_ref, qh_ref, aqk_ref, ktt_ref, egl_ref):
    c = GDN_CHUNK
    cpt = TILE // c
    n_pairs = q_ref.shape[2] // LANES
    n = n_pairs * 2 * cpt
    lo = lax.broadcasted_iota(jnp.int32, (1, 1, LANES), 2) < HEAD_DIM

    def both(x):
        parts = []
        for p in range(n_pairs):
            x3 = x[:, LANES * p:LANES * (p + 1)].reshape(cpt, c, LANES)
            parts += [x3, x3]
        return jnp.concatenate(parts, axis=0)

    sc = sc_ref[0]

    def per_dir(col):
        parts = []
        for p in range(n_pairs):
            a0 = sc[2 * p].reshape(cpt, c, SUBLANES)
            a1 = sc[2 * p + 1].reshape(cpt, c, SUBLANES)
            for d in range(2):
                parts.append(jnp.where(lo, a0[:, :, col + d:col + d + 1], a1[:, :, col + d:col + d + 1]))
        return jnp.concatenate(parts, axis=0)

    q, k, v = both(q_ref[0]), both(k_ref[0]), both(v_ref[0])
    beta = per_dir(2)
    gcc = per_dir(4)
    sct = sct_ref[0]
    gcr = jnp.stack([jnp.concatenate([sct[2 * p, 4 + d:5 + d, ch * c:(ch + 1) * c],
                                      sct[2 * p + 1, 4 + d:5 + d, ch * c:(ch + 1) * c]], axis=1)
                     for p in range(n_pairs) for d in range(2) for ch in range(cpt)], axis=0)

    ii = lax.broadcasted_iota(jnp.int32, (n, c, LANES), 1)
    jj = lax.broadcasted_iota(jnp.int32, (n, c, LANES), 2) % c
    fwd = (lax.broadcasted_iota(jnp.int32, (n, c, LANES), 0) // cpt) % 2 == 0
    incl = jnp.logical_or(jnp.logical_and(fwd, ii >= jj), jnp.logical_and(jnp.logical_not(fwd), ii <= jj))
    strict = jnp.logical_and(incl, ii != jj)
    decay = jnp.where(incl, jnp.exp(jnp.where(incl, gcc - gcr, 0.0)), 0.0)
    kb = k * beta
    a = _bmm_nt(jnp.concatenate([kb, q], axis=1).astype(D_GA), _pair_blockdiag(k.astype(D_GA), lo))
    lmat = jnp.where(strict, a[:, :c] * decay, 0.0)
    aqk = a[:, c:] * decay
    tinv = _tri_inverse(lmat, ii, jj, lo)
    egc = jnp.exp(gcc)
    u = _pmm(tinv, v * beta, lo, D_GT)
    w = _pmm(tinv, kb * egc, lo, D_GT)
    fwd1 = (lax.broadcasted_iota(jnp.int32, (n, 1, 1), 0) // cpt) % 2 == 0
    glast = jnp.where(fwd1, gcc[:, c - 1:c, :], gcc[:, 0:1, :])
    ktail = k * jnp.exp(glast - gcc)
    eye = jnp.where(ii == jj, 1.0, 0.0).astype(D_GDN)
    ktt = _bmm_nt(eye, _pair_blockdiag(ktail.astype(D_GDN), lo))
    split = lambda z: z.reshape((n_pairs, 2, cpt) + z.shape[1:])
    u_ref[0] = split(u)
    w_ref[0] = split(w.astype(D_GDN))
    qh_ref[0] = split((q * egc).astype(D_GDN))
    aqk_ref[0] = split(aqk.astype(D_GDN))
    ktt_ref[0] = split(ktt.astype(D_GDN))
    egl_ref[0] = split(jnp.exp(glast))


def _gdn_local(qn, kn, vv, sc, sct):
    nb, tt, b_w = qn.shape
    nh = sc.shape[1]
    n_pairs = b_w // LANES
    nt = tt // TILE
    nc = tt // GDN_CHUNK
    cpt = TILE // GDN_CHUNK
    tok = pl.BlockSpec((1, TILE, b_w), lambda b, i: (b, i, 0))
    chunked = lambda r: pl.BlockSpec((1, n_pairs, 2, cpt, r, LANES), lambda b, i: (b, 0, 0, i, 0, 0))
    shape = lambda r, dt: jax.ShapeDtypeStruct((nb, n_pairs, 2, nc, r, LANES), dt)
    return pl.pallas_call(
        _gdn_local_kernel,
        grid=(nb, nt),
        in_specs=[tok, tok, tok,
                  pl.BlockSpec((1, nh, TILE, SUBLANES), lambda b, i: (b, 0, i, 0)),
                  pl.BlockSpec((1, nh, SUBLANES, TILE), lambda b, i: (b, 0, 0, i))],
        out_specs=(chunked(GDN_CHUNK), chunked(GDN_CHUNK), chunked(GDN_CHUNK),
                   chunked(GDN_CHUNK), chunked(HEAD_DIM), chunked(1)),
        out_shape=(
            shape(GDN_CHUNK, F32),
            shape(GDN_CHUNK, D_GDN),
            shape(GDN_CHUNK, D_GDN),
            shape(GDN_CHUNK, D_GDN),
            shape(HEAD_DIM, D_GDN),
            shape(1, F32),
        ),
        compiler_params=_cparams(("parallel", "parallel")),
        name="gdn_local",
    )(qn, kn, vv, sc, sct)


def _gdn_scan_kernel(uf, ub, wf, wb, qf, qb, af, ab, kf, kb, ef, eb, of_ref, ob_ref, s_ref, *, cpt):
    @pl.when(pl.program_id(0) == 0)
    def _():
        s_ref[...] = jnp.zeros_like(s_ref)

    nb, n_pairs = uf.shape[:2]
    n = nb * n_pairs
    lo = lax.broadcasted_iota(jnp.int32, (1, 1, LANES), 2) < HEAD_DIM
    s = s_ref[...]
    for t in range(cpt):
        tb = cpt - 1 - t
        flat = lambda z: z.reshape((n,) + z.shape[2:])
        pair = lambda f, b: jnp.concatenate([flat(f[:, :, 0, t]), flat(b[:, :, 0, tb])], axis=0)
        sd = _pair_blockdiag(s.astype(D_GS), lo)
        v_new = pair(uf, ub) - _bmm(pair(wf, wb), sd)
        vd = _pair_blockdiag(v_new.astype(D_GS), lo)
        o = _bmm(pair(qf, qb), sd) + _bmm(pair(af, ab), vd)
        s = s * pair(ef, eb) + _bmm(pair(kf, kb), vd)
        of_ref[:, :, t] = o[:n].reshape((nb, n_pairs) + o.shape[1:])
        ob_ref[:, :, tb] = o[n:].reshape((nb, n_pairs) + o.shape[1:])
    s_ref[...] = s


def _gdn_scan(u, w, qh, aqk, ktt, egl, *, n_ctx_tiles):
    nb, n_pairs, _, nc, c, _ = u.shape
    cpt = TILE // c
    nt = nc // cpt

    def bwd_tile(j):
        return jnp.where(j < n_ctx_tiles, n_ctx_tiles - 1 - j, nt - 1 + n_ctx_tiles - j)

    def specs(a):
        blk = (nb, n_pairs, 1, cpt) + a.shape[4:]
        return [pl.BlockSpec(blk, lambda j: (0, 0, 0, j, 0, 0)),
                pl.BlockSpec(blk, lambda j: (0, 0, 1, bwd_tile(j), 0, 0))]

    args, in_specs = [], []
    for a in (u, w, qh, aqk, ktt, egl):
        args += [a, a]
        in_specs += specs(a)
    oshape = jax.ShapeDtypeStruct((nb, n_pairs, nc, c, LANES), F32)
    return pl.pallas_call(
        functools.partial(_gdn_scan_kernel, cpt=cpt),
        grid=(nt,),
        in_specs=in_specs,
        out_specs=(pl.BlockSpec((nb, n_pairs, cpt, c, LANES), lambda j: (0, 0, j, 0, 0)),
                   pl.BlockSpec((nb, n_pairs, cpt, c, LANES), lambda j: (0, 0, bwd_tile(j), 0, 0))),
        out_shape=(oshape, oshape),
        scratch_shapes=[pltpu.VMEM((2 * nb * n_pairs, HEAD_DIM, LANES), F32)],
        compiler_params=_cparams(("arbitrary",)),
        name="gdn_scan",
    )(*args)


def _mix_ffn_kernel(x_ref, mod_ref, ya_ref, of_ref, ob_ref, gate_ref, gnw_ref, yc_ref, w_ref,
                    nw_ref, wi_ref, wo_ref, fnw_ref, o_ref, *, d_ff, final):
    mod = mod_ref[0]
    lo = lax.broadcasted_iota(jnp.int32, (TILE, LANES), 1) < HEAD_DIM
    pairs = []
    for p in range(of_ref.shape[1]):
        o = of_ref[0, p] + ob_ref[0, p]
        sq = o * o
        ms_lo = jnp.sum(jnp.where(lo, sq, 0.0), axis=-1, keepdims=True) * (1.0 / HEAD_DIM)
        ms_hi = jnp.sum(jnp.where(lo, 0.0, sq), axis=-1, keepdims=True) * (1.0 / HEAD_DIM)
        pairs.append(o * lax.rsqrt(jnp.where(lo, ms_lo, ms_hi) + EPS) * gnw_ref[...])
    yb = (jnp.concatenate(pairs, axis=1) * _silu(gate_ref[0])).astype(D_OUT)
    y = jnp.concatenate([ya_ref[0], yb, yc_ref[0]], axis=1)
    x = x_ref[0] + mod[2:3] * _dot(y, w_ref[0])

    h = _norm_mod(x, nw_ref[...], mod[3:4], mod[4:5]).astype(D_FFN)
    gu = _dot(h, wi_ref[0])
    a = (_silu(gu[:, :d_ff]) * gu[:, d_ff:]).astype(D_FFN)
    out = x + mod[5:6] * _dot(a, wo_ref[0])
    if final:
        out = out * lax.rsqrt(jnp.mean(out * out, axis=-1, keepdims=True) + EPS) * fnw_ref[...]
    o_ref[0] = out


def _mix_ffn(xs, mods, ya, scan_f, scan_b, gate, gnw2, yc, w, nw, wi, wo, fnw,
             *, layer, tile0, final, n_batch, n_ctx_tiles):
    nb, tt, d = xs.shape
    nt = tt // TILE - tile0
    a_w, b_w, c_w = ya.shape[2], gate.shape[2], yc.shape[2]
    n_pairs = scan_f.shape[1]
    d_ff = wo.shape[1]

    def mod_idx(b, i):
        return (jnp.where(i + tile0 < n_ctx_tiles, n_batch, b), 0, 0)

    tok = lambda wd: pl.BlockSpec((1, TILE, wd), lambda b, i: (b, i + tile0, 0))
    pairs = pl.BlockSpec((1, n_pairs, TILE, LANES), lambda b, i: (b, 0, i + tile0, 0))
    vec = lambda wd: pl.BlockSpec((1, wd), lambda b, i: (0, 0))
    weight = lambda arr: pl.BlockSpec((1,) + arr.shape[1:], lambda b, i: (layer, 0, 0), pipeline_mode=pl.Buffered(1))
    return pl.pallas_call(
        functools.partial(_mix_ffn_kernel, d_ff=d_ff, final=final),
        grid=(nb, nt),
        in_specs=[tok(d), pl.BlockSpec((1, 6, d), mod_idx), tok(a_w), pairs, pairs, tok(b_w), vec(LANES), tok(c_w),
                  weight(w), vec(d), weight(wi), weight(wo), vec(d)],
        out_specs=pl.BlockSpec((1, TILE, d), lambda b, i: (b, i, 0)),
        out_shape=jax.ShapeDtypeStruct((nb, nt * TILE, d), F32),
        compiler_params=_cparams(("parallel", "parallel")),
        name="mix_ffn",
    )(xs, mods, ya, scan_f, scan_b, gate, gnw2, yc, w, nw, wi, wo, fnw)


def _rope_tables(seq, ctx_len, n_rep):
    half = A_QK // 2
    inv_freq = 1.0 / (ROPE_THETA ** (jnp.arange(0, half, 2, dtype=F32) / half))
    t = jnp.arange(seq, dtype=jnp.int32)
    ang_r = (t // GRID_W).astype(F32)[:, None] * inv_freq
    ang_c = (t % GRID_W).astype(F32)[:, None] * inv_freq
    ang = jnp.concatenate([ang_r, ang_r, ang_c, ang_c], axis=-1)
    cos = jnp.concatenate([jnp.ones((ctx_len, A_QK), F32), jnp.cos(ang)], axis=0)
    sin = jnp.concatenate([jnp.zeros((ctx_len, A_QK), F32), jnp.sin(ang)], axis=0)
    return jnp.tile(cos, (1, n_rep)), jnp.tile(sin, (1, n_rep))


def kernel(x, c, ctx, c_ctx, w_mod, b_mod, norm1_w, norm2_w, w_in, w_out, lambda_q1, lambda_k1, lambda_q2,
           lambda_k2, diff_norm_w, conv_w, a_log, dt_bias, gdn_norm_w, na_bias, w_ffn_in, w_ffn_out,
           final_norm_w):
    nb, seq, d = x.shape
    ctx_len = ctx.shape[1]
    depth = w_mod.shape[0]
    b_heads = a_log.shape[-1]
    c_heads = na_bias.shape[1]
    b_w = b_heads * HEAD_DIM
    c_w = c_heads * HEAD_DIM
    a_w = w_out.shape[1] - b_w - c_w
    assert seq % TILE == 0 and ctx_len % TILE == 0 and seq % GRID_W == 0
    assert a_w % LANES == 0 and b_w % LANES == 0 and c_w % LANES == 0 and 4 * b_heads <= LANES
    n_ctx_tiles = ctx_len // TILE

    xs = jnp.concatenate([ctx, x], axis=1)

    n_rows = -(-(nb + 1) // SUBLANES) * SUBLANES
    cc = jnp.concatenate([c, c_ctx[None, :], jnp.zeros((n_rows - nb - 1, d), F32)], axis=0)
    mods_all = _modulation(cc, w_mod, b_mod).reshape(depth, n_rows, 6, d)

    cos, sin = _rope_tables(seq, ctx_len, a_w // A_QK)
    cost, sint = cos.T, sin.T

    sizes = (3 * a_w, 3 * b_w, b_w, 2 * b_heads, 2 * b_heads, 3 * c_w)
    offs = np.concatenate([[0], np.cumsum(sizes)])
    o_a, o_b, o_g, o_al, o_be, o_c = (int(v) for v in offs[:6])
    pad = LANES - 4 * b_heads

    wm_all = jnp.concatenate([
        w_in[:, :, o_a:o_a + a_w],
        w_in[:, :, o_a + 2 * a_w:o_a + 3 * a_w],
        w_in[:, :, o_b:o_b + 3 * b_w],
        w_in[:, :, o_g:o_g + b_w],
        w_in[:, :, o_c:o_c + c_w],
        w_in[:, :, o_c + 2 * c_w:o_c + 3 * c_w],
        w_in[:, :, o_al:o_al + 4 * b_heads],
        jnp.zeros((depth, d, pad), F32),
    ], axis=2).astype(D_IN)
    wk_all = jnp.swapaxes(jnp.concatenate([w_in[:, :, o_a + a_w:o_a + 2 * a_w],
                                           w_in[:, :, o_c + c_w:o_c + 2 * c_w]], axis=2), 1, 2).astype(D_IN)
    w_out_all = w_out.astype(D_OUT)
    w_ffn_in_all = w_ffn_in.astype(D_FFN)
    w_ffn_out_all = w_ffn_out.astype(D_FFN)
    bias_all = _na_bias_table(na_bias, seq)
    nw2_all = jnp.tile(diff_norm_w, (1, LANES // HEAD_DIM))
    gnw2_all = jnp.tile(gdn_norm_w, (1, LANES // HEAD_DIM))
    zpad = jnp.zeros((depth, LANES - 2 * b_heads), F32)
    alog_all = jnp.concatenate([a_log.reshape(depth, -1), zpad], axis=1)
    dtb_all = jnp.concatenate([dt_bias.reshape(depth, -1), zpad], axis=1)
    fnw = final_norm_w[None, :]

    for l in range(depth):
        last = l == depth - 1
        mods = mods_all[l]
        lam_init = 0.8 - 0.6 * math.exp(-0.3 * l)

        qa, kat, va, qkvb, gate, ab, qc, kct, vc = _inproj(
            xs, mods, norm1_w[l][None, :], wm_all, wk_all, cos, sin, cost, sint,
            layer=l, n_batch=nb, n_ctx_tiles=n_ctx_tiles, a_w=a_w, b_w=b_w, c_w=c_w)

        lam_vecs = jnp.stack([lambda_q1[l], lambda_k1[l], lambda_q2[l], lambda_k2[l]], axis=0).astype(F32)
        ya = _diffattn(lam_vecs, nw2_all[l][None, :], qa, kat, va, n_ctx_tiles=n_ctx_tiles, lam_init=lam_init)

        yc = _na(qc, kct, vc, bias_all, layer=l, n_ctx_tiles=n_ctx_tiles)

        qn, kn, vv, sc, sct = _gdn_prep(qkvb, ab, conv_w[l], alog_all[l][None, :], dtb_all[l][None, :],
                                        n_ctx_tiles=n_ctx_tiles, n_heads=b_heads)
        u, w, qh, aqk, ktt, egl = _gdn_local(qn, kn, vv, sc, sct)
        scan_f, scan_b = _gdn_scan(u, w, qh, aqk, ktt, egl, n_ctx_tiles=n_ctx_tiles)
        scan_f = scan_f.reshape(nb, b_w // LANES, ctx_len + seq, LANES)
        scan_b = scan_b.reshape(nb, b_w // LANES, ctx_len + seq, LANES)

        tile0 = n_ctx_tiles if last else 0
        xs = _mix_ffn(xs, mods, ya, scan_f, scan_b, gate, gnw2_all[l][None, :], yc, w_out_all,
                      norm2_w[l][None, :], w_ffn_in_all, w_ffn_out_all, fnw,
                      layer=l, tile0=tile0, final=last, n_batch=nb, n_ctx_tiles=n_ctx_tiles)

    return xs
```

```python
import functools
import math

import jax
import jax.numpy as jnp
import numpy as np
from jax import lax
from jax.experimental import pallas as pl
from jax.experimental.pallas import tpu as pltpu

F32 = jnp.float32
BF16 = jnp.bfloat16
D_MOD = D_IN = D_OUT = D_FFN = D_ATT = D_NA = D_GDN = BF16
D_GA = D_GI = D_GT = D_GS = BF16

HEAD_DIM = 64
A_QK = HEAD_DIM // 2
GRID_W = 64
ROPE_THETA = 10000.0
CONV_W = 5
GDN_CHUNK = 64
WIN_ROWS = 8
WIN_COLS = 16
EPS = 1e-6

LANES = 128
SUBLANES = 8
TILE = 256
NA_WIN_TILES = 3
NEG = -1e30
VMEM_LIMIT = 56 * 1024 * 1024


def _cparams(sem):
    return pltpu.CompilerParams(dimension_semantics=sem, vmem_limit_bytes=VMEM_LIMIT)


def _silu(x):
    return x * (1.0 / (1.0 + jnp.exp(-x)))


def _dot(a, b):
    return jnp.dot(a, b, preferred_element_type=F32)


def _dot_nt(a, b):
    return lax.dot_general(a, b, (((1,), (1,)), ((), ())), preferred_element_type=F32)


def _split3(x):
    hi = x.astype(BF16)
    r1 = x - hi.astype(F32)
    mid = r1.astype(BF16)
    lo = (r1 - mid.astype(F32)).astype(BF16)
    return hi, mid, lo


def _dot_sel_right(x, sel):
    hi, mid, lo = _split3(x)
    return _dot(hi, sel) + _dot(mid, sel) + _dot(lo, sel)


def _dot_sel_left(sel, x):
    hi, mid, lo = _split3(x)
    return _dot(sel, hi) + _dot(sel, mid) + _dot(sel, lo)


def _mod_kernel(c_ref, w_ref, b_ref, o_ref):
    s = _silu(c_ref[...])
    o_ref[0] = _dot(s.astype(D_MOD), w_ref[0].astype(D_MOD)) + b_ref[0]


def _modulation(cc, w_mod, b_mod):
    depth, d, n = w_mod.shape
    rows = cc.shape[0]
    tn = 1536
    assert n % tn == 0
    return pl.pallas_call(
        _mod_kernel,
        grid=(depth, n // tn),
        in_specs=[
            pl.BlockSpec((rows, d), lambda l, j: (0, 0)),
            pl.BlockSpec((1, d, tn), lambda l, j: (l, 0, j)),
            pl.BlockSpec((1, 1, tn), lambda l, j: (l, 0, j)),
        ],
        out_specs=pl.BlockSpec((1, rows, tn), lambda l, j: (l, 0, j)),
        out_shape=jax.ShapeDtypeStruct((depth, rows, n), F32),
        compiler_params=_cparams(("parallel", "parallel")),
        name="modulation",
    )(cc, w_mod, b_mod.reshape(depth, 1, n))


def _norm_mod(x, nw, shift, scale):
    ms = jnp.mean(x * x, axis=-1, keepdims=True)
    y = x * lax.rsqrt(ms + EPS) * nw
    return y * (1.0 + scale) + shift


def _rope_lanes(x, cos, sin):
    parts = []
    for j in range(x.shape[1] // LANES):
        xj = x[:, LANES * j:LANES * (j + 1)]
        nxt = pltpu.roll(xj, LANES - 8, axis=1)
        prv = pltpu.roll(xj, 8, axis=1)
        lane = lax.broadcasted_iota(jnp.int32, xj.shape, 1)
        parts.append(jnp.where((lane % 16) < 8, -nxt, prv))
    rot = jnp.concatenate(parts, axis=1)
    return x * cos + rot * sin


def _rope_rows(x, cos, sin):
    n = x.shape[0]
    nxt = pltpu.roll(x, n - 8, axis=0)
    prv = pltpu.roll(x, 8, axis=0)
    row = lax.broadcasted_iota(jnp.int32, x.shape, 0)
    rot = jnp.where((row % 16) < 8, -nxt, prv)
    return x * cos + rot * sin


def _inproj_kernel(x_ref, mod_ref, nw_ref, wm_ref, wk_ref, cos_ref, sin_ref, cost_ref, sint_ref,
                   qa_ref, kat_ref, va_ref, qkvb_ref, gate_ref, ab_ref, qc_ref, kct_ref, vc_ref,
                   *, a_w, b_w, c_w):
    mod = mod_ref[0]
    h = _norm_mod(x_ref[0], nw_ref[...], mod[0:1], mod[1:2]).astype(D_IN)
    acc = _dot(h, wm_ref[0])
    kt = _dot_nt(wk_ref[0], h)
    o = 0
    qa = _rope_lanes(acc[:, o:o + a_w], cos_ref[...], sin_ref[...])
    qa_ref[0] = (qa * (A_QK ** -0.5 * math.log2(math.e))).astype(D_ATT)
    o += a_w
    va_ref[0] = acc[:, o:o + a_w].astype(D_ATT)
    o += a_w
    qkvb_ref[0] = acc[:, o:o + 3 * b_w]
    o += 3 * b_w
    gate_ref[0] = acc[:, o:o + b_w]
    o += b_w
    qc_ref[0] = (acc[:, o:o + c_w] * (HEAD_DIM ** -0.5 * math.log2(math.e))).astype(D_NA)
    o += c_w
    vc_ref[0] = acc[:, o:o + c_w].astype(D_NA)
    o += c_w
    ab_ref[0] = acc[:, o:o + LANES]
    kat_ref[0, 0] = _rope_rows(kt[:a_w], cost_ref[...], sint_ref[...]).astype(D_ATT)
    kct_ref[0, 0] = kt[a_w:].astype(D_NA)


def _inproj(xs, mods, nw, wm, wk, cos, sin, cost, sint, *, layer, n_batch, n_ctx_tiles, a_w, b_w, c_w):
    nb, tt, d = xs.shape
    nt = tt // TILE
    ctx_row = n_batch

    def mod_idx(b, i):
        return (jnp.where(i < n_ctx_tiles, ctx_row, b), 0, 0)

    tok = lambda w: pl.BlockSpec((1, TILE, w), lambda b, i: (b, i, 0))
    tokt = lambda w: pl.BlockSpec((1, 1, w, TILE), lambda b, i: (b, i, 0, 0))
    out_shape = (
        jax.ShapeDtypeStruct((nb, tt, a_w), D_ATT),
        jax.ShapeDtypeStruct((nb, nt, a_w, TILE), D_ATT),
        jax.ShapeDtypeStruct((nb, tt, a_w), D_ATT),
        jax.ShapeDtypeStruct((nb, tt, 3 * b_w), F32),
        jax.ShapeDtypeStruct((nb, tt, b_w), F32),
        jax.ShapeDtypeStruct((nb, tt, LANES), F32),
        jax.ShapeDtypeStruct((nb, tt, c_w), D_NA),
        jax.ShapeDtypeStruct((nb, nt, c_w, TILE), D_NA),
        jax.ShapeDtypeStruct((nb, tt, c_w), D_NA),
    )
    return pl.pallas_call(
        functools.partial(_inproj_kernel, a_w=a_w, b_w=b_w, c_w=c_w),
        grid=(nb, nt),
        in_specs=[
            tok(d),
            pl.BlockSpec((1, 6, d), mod_idx),
            pl.BlockSpec((1, d), lambda b, i: (0, 0)),
            pl.BlockSpec((1,) + wm.shape[1:], lambda b, i: (layer, 0, 0)),
            pl.BlockSpec((1,) + wk.shape[1:], lambda b, i: (layer, 0, 0)),
            pl.BlockSpec((TILE, a_w), lambda b, i: (i, 0)),
            pl.BlockSpec((TILE, a_w), lambda b, i: (i, 0)),
            pl.BlockSpec((a_w, TILE), lambda b, i: (0, i)),
            pl.BlockSpec((a_w, TILE), lambda b, i: (0, i)),
        ],
        out_specs=(tok(a_w), tokt(a_w), tok(a_w), tok(3 * b_w), tok(b_w), tok(LANES),
                   tok(c_w), tokt(c_w), tok(c_w)),
        out_shape=out_shape,
        compiler_params=_cparams(("parallel", "parallel")),
        name="inproj",
    )(xs, mods, nw, wm, wk, cos, sin, cost, sint)


def _diffattn_kernel(lam_ref, nw_ref, q_ref, kt_ref, v_ref, o_ref, mx_ref, acc_ref,
                     *, n_ctx_tiles, n_key_tiles, lam_init):
    i = pl.program_id(2)
    lv = lam_ref[...]
    lam = (jnp.exp(jnp.sum(lv[0:1] * lv[1:2], axis=-1, keepdims=True))
           - jnp.exp(jnp.sum(lv[2:3] * lv[3:4], axis=-1, keepdims=True)) + lam_init)
    q = q_ref[0]
    tq = q.shape[0]
    lane = lax.broadcasted_iota(jnp.int32, (tq, LANES), 1)
    lo = lane < HEAD_DIM
    zero = jnp.zeros_like(q)
    qm = [jnp.where((lane // A_QK) == m, q, zero) for m in range(LANES // A_QK)]
    n_maps = len(qm)
    n_lat = n_key_tiles - n_ctx_tiles
    group = max(u for u in (16, 8, 4, 2, 1) if n_lat % u == 0)
    ctx_tiles = list(range(n_ctx_tiles))
    lat_tiles = lambda g: [n_ctx_tiles + g * group + u for u in range(group)]

    mx_ref[...] = jnp.full(mx_ref.shape, NEG, F32)
    acc_ref[...] = jnp.zeros(acc_ref.shape, F32)
    lane_v = lax.broadcasted_iota(jnp.int32, (TILE, LANES), 1)

    def visit(tiles):
        ktiles = [kt_ref[0, j] for j in tiles]
        v_ext = []
        for j in tiles:
            start = j * TILE if isinstance(j, int) else pl.multiple_of(j * TILE, TILE)
            vtile = v_ref[0, pl.ds(start, TILE), :]
            ones = jnp.ones_like(vtile)
            v_ext.append((jnp.where(lane_v < HEAD_DIM, vtile, ones), jnp.where(lane_v < HEAD_DIM, ones, vtile)))
        scores = [[_dot(qm[m], ktile) for ktile in ktiles] for m in range(n_maps)]
        m_new = []
        for m in range(n_maps):
            top = scores[m][0]
            for s in scores[m][1:]:
                top = jnp.maximum(top, s)
            top = jnp.max(jnp.maximum(top[:, :LANES], top[:, LANES:]), axis=-1, keepdims=True)
            m_new.append(jnp.maximum(mx_ref[m], top))
        probs = [[jnp.exp2(s - m_new[m]).astype(D_ATT) for s in scores[m]] for m in range(n_maps)]
        for m in range(n_maps):
            acc = acc_ref[m] * jnp.exp2(mx_ref[m] - m_new[m])
            for p, ve in zip(probs[m], v_ext):
                acc = acc + _dot(p, ve[m // 2])
            mx_ref[m] = m_new[m]
            acc_ref[m] = acc

    @pl.when(i < n_ctx_tiles)
    def _():
        visit(ctx_tiles)

    @pl.when(i >= n_ctx_tiles)
    def _():
        visit(ctx_tiles + lat_tiles(0))
        lax.fori_loop(1, n_lat // group, lambda g, c: (visit(lat_tiles(g)), c)[1], 0)
    attn = []
    for m in range(n_maps):
        a = acc_ref[m]
        attn.append(a / pltpu.roll(a, HEAD_DIM, axis=1))
    o = jnp.where(lo, attn[0] - lam * attn[1], attn[2] - lam * attn[3])
    sq = o * o
    ms_lo = jnp.sum(jnp.where(lo, sq, 0.0), axis=-1, keepdims=True) * (1.0 / HEAD_DIM)
    ms_hi = jnp.sum(jnp.where(lo, 0.0, sq), axis=-1, keepdims=True) * (1.0 / HEAD_DIM)
    y = o * lax.rsqrt(jnp.where(lo, ms_lo, ms_hi) + EPS) * nw_ref[...]
    o_ref[0] = (y * (1.0 - lam_init)).astype(D_OUT)


def _diffattn(lam_vecs, nw2, qa, kat, va, *, n_ctx_tiles, lam_init):
    nb, tt, a_w = qa.shape
    nt = tt // TILE
    return pl.pallas_call(
        functools.partial(_diffattn_kernel, n_ctx_tiles=n_ctx_tiles, n_key_tiles=nt, lam_init=lam_init),
        grid=(nb, a_w // LANES, nt),
        in_specs=[
            pl.BlockSpec(lam_vecs.shape, lambda b, p, i: (0, 0)),
            pl.BlockSpec((1, LANES), lambda b, p, i: (0, 0)),
            pl.BlockSpec((1, TILE, LANES), lambda b, p, i: (b, i, p)),
            pl.BlockSpec((1, nt, LANES, TILE), lambda b, p, i: (b, 0, p, 0)),
            pl.BlockSpec((1, tt, LANES), lambda b, p, i: (b, 0, p)),
        ],
        out_specs=pl.BlockSpec((1, TILE, LANES), lambda b, p, i: (b, i, p)),
        out_shape=jax.ShapeDtypeStruct((nb, tt, a_w), D_OUT),
        scratch_shapes=[pltpu.VMEM((LANES // A_QK, TILE, 1), F32),
                        pltpu.VMEM((LANES // A_QK, TILE, LANES), F32)],
        compiler_params=_cparams(("parallel", "parallel", "parallel")),
        name="diffattn",
    )(lam_vecs, nw2, qa, kat, va)


def _na_kernel(q_ref, kt_ref, v_ref, bias_ref, o_ref, *, n_ctx_tiles, n_key_tiles):
    i = pl.program_id(1)
    n_lat = n_key_tiles - n_ctx_tiles
    w0 = n_ctx_tiles + jnp.clip(i - n_ctx_tiles - 1, 0, n_lat - NA_WIN_TILES)
    tq = q_ref.shape[1]
    n_pairs = q_ref.shape[2] // LANES
    lane = lax.broadcasted_iota(jnp.int32, (tq, LANES), 1)
    lo = lane < HEAD_DIM
    heads = [(p, hd) for p in range(n_pairs) for hd in range(2)]
    qh = []
    for p in range(n_pairs):
        q = q_ref[0, :, LANES * p:LANES * (p + 1)]
        zero = jnp.zeros_like(q)
        qh += [jnp.where(lo, q, zero), jnp.where(lo, zero, q)]
    tiles = [w0 + c for c in range(NA_WIN_TILES)] + list(range(n_ctx_tiles))
    scores = []
    for h, (p, hd) in enumerate(heads):
        row = []
        for c, t in enumerate(tiles):
            s = _dot(qh[h], kt_ref[0, t, LANES * p:LANES * (p + 1), :])
            if c < NA_WIN_TILES:
                s = s + bias_ref[0, h, 0, :, TILE * c:TILE * (c + 1)]
            row.append(s)
        scores.append(row)
    tops = []
    for row in scores:
        top = row[0]
        for s in row[1:]:
            top = jnp.maximum(top, s)
        tops.append(jnp.max(jnp.maximum(top[:, :LANES], top[:, LANES:]), axis=-1, keepdims=True))
    probs = [[jnp.exp2(s - top).astype(D_NA) for s in row] for row, top in zip(scores, tops)]
    lane_v = lax.broadcasted_iota(jnp.int32, (TILE, LANES), 1)
    accs = [jnp.zeros((tq, LANES), F32) for _ in heads]
    for c, t in enumerate(tiles):
        start = t * TILE if isinstance(t, int) else pl.multiple_of(t * TILE, TILE)
        for p in range(n_pairs):
            vt = v_ref[0, pl.ds(start, TILE), LANES * p:LANES * (p + 1)]
            ones = jnp.ones_like(vt)
            v_ext = (jnp.where(lane_v < HEAD_DIM, vt, ones), jnp.where(lane_v < HEAD_DIM, ones, vt))
            for hd in range(2):
                h = 2 * p + hd
                accs[h] = accs[h] + _dot(probs[h][c], v_ext[hd])
    outs = [a / pltpu.roll(a, HEAD_DIM, axis=1) for a in accs]
    o_ref[0] = jnp.concatenate([jnp.where(lo, outs[2 * p], outs[2 * p + 1]) for p in range(n_pairs)],
                               axis=1).astype(D_OUT)


def _na(qc, kct, vc, bias, *, layer, n_ctx_tiles):
    nb, tt, c_w = qc.shape
    nt = tt // TILE
    n_lat = nt - n_ctx_tiles

    def bias_idx(b, i):
        il = i - n_ctx_tiles
        cls = jnp.where(il < 0, 3, jnp.where(il == 0, 0, jnp.where(il == n_lat - 1, 2, 1)))
        return (layer, 0, cls, 0, 0)

    return pl.pallas_call(
        functools.partial(_na_kernel, n_ctx_tiles=n_ctx_tiles, n_key_tiles=nt),
        grid=(nb, nt),
        in_specs=[
            pl.BlockSpec((1, TILE, c_w), lambda b, i: (b, i, 0)),
            pl.BlockSpec((1, nt, c_w, TILE), lambda b, i: (b, 0, 0, 0)),
            pl.BlockSpec((1, tt, c_w), lambda b, i: (b, 0, 0)),
            pl.BlockSpec((1, bias.shape[1], 1, TILE, NA_WIN_TILES * TILE), bias_idx),
        ],
        out_specs=pl.BlockSpec((1, TILE, c_w), lambda b, i: (b, i, 0)),
        out_shape=jax.ShapeDtypeStruct((nb, tt, c_w), D_OUT),
        compiler_params=_cparams(("parallel", "parallel")),
        name="nbr_attn",
    )(qc, kct, vc, bias)


def _na_bias_kernel(u_ref, o_ref, *, plan):
    neg = jnp.full((GRID_W, GRID_W), NEG, F32)
    for cls, rows_plan in enumerate(plan):
        for qr, row_plan in enumerate(rows_plan):
            blocks = [neg if dr is None else u_ref[0, 0, dr] for dr in row_plan]
            o_ref[0, 0, cls, GRID_W * qr:GRID_W * (qr + 1), :] = jnp.concatenate(blocks, axis=1)
    o_ref[0, 0, len(plan)] = jnp.full(o_ref.shape[3:], NEG, F32)


def _na_bias_table(rel_bias, seq):
    rows = seq // GRID_W
    rpt = TILE // GRID_W
    wrows = NA_WIN_TILES * rpt
    assert rows >= wrows and rows >= WIN_ROWS
    n_layers, n_heads = rel_bias.shape[:2]
    n_dr, n_dc = 2 * WIN_ROWS - 1, 2 * WIN_COLS - 1
    qcol, kcol = np.arange(GRID_W)[:, None], np.arange(GRID_W)[None, :]
    kc0 = np.clip(qcol - WIN_COLS // 2, 0, GRID_W - WIN_COLS)
    col_ok = (kcol >= kc0) & (kcol < kc0 + WIN_COLS)
    dcol = kcol - qcol + (WIN_COLS - 1)
    sel_c = ((dcol[None] == np.arange(n_dc)[:, None, None]) & col_ok[None]).astype(np.float32)
    blocks = jnp.einsum("lhdc,cqk->lhdqk", rel_bias.astype(F32) * math.log2(math.e), sel_c,
                        precision=lax.Precision.HIGHEST)
    blocks = jnp.where(jnp.asarray(col_ok), blocks, NEG)
    n_lat = rows // rpt
    plan = []
    for il in (0, min(1, n_lat - 1), n_lat - 1):
        ws = int(np.clip(il - 1, 0, n_lat - NA_WIN_TILES)) * rpt
        rows_plan = []
        for qr in range(rpt):
            r_abs = il * rpt + qr
            kr0 = int(np.clip(r_abs - WIN_ROWS // 2, 0, rows - WIN_ROWS))
            rows_plan.append(tuple((ws + kr) - r_abs + (WIN_ROWS - 1) if kr0 <= ws + kr < kr0 + WIN_ROWS else None
                                   for kr in range(wrows)))
        plan.append(tuple(rows_plan))
    n_cls = len(plan) + 1
    return pl.pallas_call(
        functools.partial(_na_bias_kernel, plan=tuple(plan)),
        grid=(n_layers, n_heads),
        in_specs=[pl.BlockSpec((1, 1, n_dr, GRID_W, GRID_W), lambda l, h: (l, h, 0, 0, 0))],
        out_specs=pl.BlockSpec((1, 1, n_cls, TILE, NA_WIN_TILES * TILE), lambda l, h: (l, h, 0, 0, 0)),
        out_shape=jax.ShapeDtypeStruct((n_layers, n_heads, n_cls, TILE, NA_WIN_TILES * TILE), F32),
        compiler_params=_cparams(("parallel", "parallel")),
        name="nbr_bias_table",
    )(blocks)


def _gdn_prep_kernel(x_ref, prev_ref, next_ref, ab_ref, cw_ref, alog_ref, dtb_ref,
                     q_ref, k_ref, v_ref, sc_ref, sct_ref, *, n_ctx_tiles, n_tiles, n_heads):
    i = pl.program_id(1)
    first = jnp.logical_or(i == 0, i == n_ctx_tiles)
    last = jnp.logical_or(i == n_ctx_tiles - 1, i == n_tiles - 1)
    prev = jnp.where(first, 0.0, prev_ref[0, 0])
    nxt = jnp.where(last, 0.0, next_ref[0, 0])
    ext = jnp.concatenate([prev, x_ref[0], nxt], axis=0)
    cw = cw_ref[...]
    y = None
    for j in range(CONV_W):
        o = SUBLANES - CONV_W // 2 + j
        term = ext[o:o + TILE] * cw[j:j + 1]
        y = term if y is None else y + term
    y = _silu(y)
    b_w = n_heads * HEAD_DIM
    lo = lax.broadcasted_iota(jnp.int32, (TILE, LANES), 1) < HEAD_DIM

    def l2norm_heads(z):
        parts = []
        for p in range(z.shape[1] // LANES):
            zp = z[:, LANES * p:LANES * (p + 1)]
            sq = zp * zp
            s_lo = jnp.sum(jnp.where(lo, sq, 0.0), axis=-1, keepdims=True)
            s_hi = jnp.sum(jnp.where(lo, 0.0, sq), axis=-1, keepdims=True)
            parts.append(zp * lax.rsqrt(jnp.where(lo, s_lo, s_hi) + EPS))
        return jnp.concatenate(parts, axis=1)

    q_ref[0] = l2norm_heads(y[:, :b_w]) * (HEAD_DIM ** -0.5)
    k_ref[0] = l2norm_heads(y[:, b_w:2 * b_w])
    v_ref[0] = y[:, 2 * b_w:]

    ab = ab_ref[0]
    lane = lax.broadcasted_iota(jnp.int32, ab.shape, 1)
    z = ab + dtb_ref[...]
    softplus = jnp.maximum(z, 0.0) + jnp.log(1.0 + jnp.exp(-jnp.abs(z)))
    g = jnp.where(lane < 2 * n_heads, -jnp.exp(alog_ref[...]) * softplus, 0.0)
    beta = 1.0 / (1.0 + jnp.exp(-ab))
    r = lax.broadcasted_iota(jnp.int32, (TILE, TILE), 0)
    c = lax.broadcasted_iota(jnp.int32, (TILE, TILE), 1)
    same = (r // GDN_CHUNK) == (c // GDN_CHUNK)
    tri_f = jnp.where(jnp.logical_and(same, c <= r), 1.0, 0.0).astype(BF16)
    tri_b = jnp.where(jnp.logical_and(same, c >= r), 1.0, 0.0).astype(BF16)
    gc = jnp.where(lane < n_heads, _dot_sel_left(tri_f, g), _dot_sel_left(tri_b, g))
    rr = lax.broadcasted_iota(jnp.int32, (LANES, LANES), 0)
    cc = lax.broadcasted_iota(jnp.int32, (LANES, LANES), 1)
    hh, jj = cc // SUBLANES, cc % SUBLANES
    valid = hh < n_heads
    sel_gb = jnp.where(jnp.logical_and(valid, jnp.logical_and(jj < 4, rr == jj * n_heads + hh)), 1.0, 0.0).astype(BF16)
    sel_gc = jnp.where(jnp.logical_and(valid, jnp.logical_and(jnp.logical_and(jj >= 4, jj < 6),
                                                              rr == (jj - 4) * n_heads + hh)), 1.0, 0.0).astype(BF16)
    gb = jnp.where(lane < 2 * n_heads, g, beta)
    packed = _dot_sel_right(gb, sel_gb) + _dot_sel_right(gc, sel_gc)
    packed_t = packed.T
    for h in range(n_heads):
        sc_ref[0, h] = packed[:, SUBLANES * h:SUBLANES * (h + 1)]
        sct_ref[0, h] = packed_t[SUBLANES * h:SUBLANES * (h + 1), :]


def _gdn_prep(qkvb, ab, conv_w, alog_v, dtb_v, *, n_ctx_tiles, n_heads):
    nb, tt, w3 = qkvb.shape
    nt = tt // TILE
    rows8 = TILE // SUBLANES
    x8 = qkvb.reshape(nb, tt // SUBLANES, SUBLANES, w3)
    hd = lambda w: pl.BlockSpec((1, n_heads, TILE, w), lambda b, i: (b, 0, i, 0))
    tok = pl.BlockSpec((1, TILE, w3 // 3), lambda b, i: (b, i, 0))
    return pl.pallas_call(
        functools.partial(_gdn_prep_kernel, n_ctx_tiles=n_ctx_tiles, n_tiles=nt, n_heads=n_heads),
        grid=(nb, nt),
        in_specs=[
            pl.BlockSpec((1, TILE, w3), lambda b, i: (b, i, 0)),
            pl.BlockSpec((1, 1, SUBLANES, w3), lambda b, i: (b, jnp.maximum(i * rows8 - 1, 0), 0, 0)),
            pl.BlockSpec((1, 1, SUBLANES, w3), lambda b, i: (b, jnp.minimum((i + 1) * rows8, nt * rows8 - 1), 0, 0)),
            pl.BlockSpec((1, TILE, LANES), lambda b, i: (b, i, 0)),
            pl.BlockSpec((CONV_W, w3), lambda b, i: (0, 0)),
            pl.BlockSpec((1, LANES), lambda b, i: (0, 0)),
            pl.BlockSpec((1, LANES), lambda b, i: (0, 0)),
        ],
        out_specs=(tok, tok, tok, hd(SUBLANES),
                   pl.BlockSpec((1, n_heads, SUBLANES, TILE), lambda b, i: (b, 0, 0, i))),
        out_shape=(
            jax.ShapeDtypeStruct((nb, tt, w3 // 3), F32),
            jax.ShapeDtypeStruct((nb, tt, w3 // 3), F32),
            jax.ShapeDtypeStruct((nb, tt, w3 // 3), F32),
            jax.ShapeDtypeStruct((nb, n_heads, tt, SUBLANES), F32),
            jax.ShapeDtypeStruct((nb, n_heads, SUBLANES, tt), F32),
        ),
        compiler_params=_cparams(("parallel", "parallel")),
        name="gdn_prep",
    )(qkvb, x8, x8, ab, conv_w, alog_v, dtb_v)


def _bmm(a, b):
    return jnp.einsum("nik,nkj->nij", a, b, preferred_element_type=F32)


def _bmm_nt(a, b):
    return jnp.einsum("nik,njk->nij", a, b, preferred_element_type=F32)


def _pair_blockdiag(x, lo):
    zero = jnp.zeros_like(x)
    return jnp.concatenate([jnp.where(lo, x, zero), jnp.where(lo, zero, x)], axis=1)


def _pmm(p, x, lo, dtype):
    return _bmm(p.astype(dtype), _pair_blockdiag(x.astype(dtype), lo))


def _tri_inverse(lmat, ii, jj, lo):
    blk = 16
    same = (ii // blk) == (jj // blk)
    dmat = jnp.where(same, lmat, 0.0)
    x = jnp.where(ii == jj, 1.0, 0.0) - dmat
    p = _pmm(dmat, dmat, lo, D_GI)
    n_sq = int(math.log2(blk)) - 1
    for step in range(n_sq):
        x = x + _pmm(p, x, lo, D_GI)
        if step < n_sq - 1:
            p = _pmm(p, p, lo, D_GI)
    while blk < GDN_CHUNK:
        wider = (ii // (2 * blk)) == (jj // (2 * blk))
        off = jnp.where(jnp.logical_and(wider, jnp.logical_not(same)), lmat, 0.0)
        x = x - _pmm(x, _pmm(off, x, lo, D_GI), lo, D_GI)
        same = wider
        blk *= 2
    return x


def _gdn_local_kernel(q_ref, k_ref, v_ref, sc_ref, sct_ref,
                      u_ref, w_ref, qh_ref, aqk_ref, ktt_ref, egl_ref):
    c = GDN_CHUNK
    cpt = TILE // c
    n_pairs = q_ref.shape[2] // LANES
    n = n_pairs * 2 * cpt
    lo = lax.broadcasted_iota(jnp.int32, (1, 1, LANES), 2) < HEAD_DIM

    def both(x):
        parts = []
        for p in range(n_pairs):
            x3 = x[:, LANES * p:LANES * (p + 1)].reshape(cpt, c, LANES)
            parts += [x3, x3]
        return jnp.concatenate(parts, axis=0)

    sc = sc_ref[0]

    def per_dir(col):
        parts = []
        for p in range(n_pairs):
            a0 = sc[2 * p].reshape(cpt, c, SUBLANES)
            a1 = sc[2 * p + 1].reshape(cpt, c, SUBLANES)
            for d in range(2):
                parts.append(jnp.where(lo, a0[:, :, col + d:col + d + 1], a1[:, :, col + d:col + d + 1]))
        return jnp.concatenate(parts, axis=0)

    q, k, v = both(q_ref[0]), both(k_ref[0]), both(v_ref[0])
    beta = per_dir(2)
    gcc = per_dir(4)
    sct = sct_ref[0]
    gcr = jnp.stack([jnp.concatenate([sct[2 * p, 4 + d:5 + d, ch * c:(ch + 1) * c],
                                      sct[2 * p + 1, 4 + d:5 + d, ch * c:(ch + 1) * c]], axis=1)
                     for p in range(n_pairs) for d in range(2) for ch in range(cpt)], axis=0)

    ii = lax.broadcasted_iota(jnp.int32, (n, c, LANES), 1)
    jj = lax.broadcasted_iota(jnp.int32, (n, c, LANES), 2) % c
    fwd = (lax.broadcasted_iota(jnp.int32, (n, c, LANES), 0) // cpt) % 2 == 0
    incl = jnp.logical_or(jnp.logical_and(fwd, ii >= jj), jnp.logical_and(jnp.logical_not(fwd), ii <= jj))
    strict = jnp.logical_and(incl, ii != jj)
    decay = jnp.where(incl, jnp.exp(jnp.where(incl, gcc - gcr, 0.0)), 0.0)
    kb = k * beta
    a = _bmm_nt(jnp.concatenate([kb, q], axis=1).astype(D_GA), _pair_blockdiag(k.astype(D_GA), lo))
    lmat = jnp.where(strict, a[:, :c] * decay, 0.0)
    aqk = a[:, c:] * decay
    tinv = _tri_inverse(lmat, ii, jj, lo)
    egc = jnp.exp(gcc)
    u = _pmm(tinv, v * beta, lo, D_GT)
    w = _pmm(tinv, kb * egc, lo, D_GT)
    fwd1 = (lax.broadcasted_iota(jnp.int32, (n, 1, 1), 0) // cpt) % 2 == 0
    glast = jnp.where(fwd1, gcc[:, c - 1:c, :], gcc[:, 0:1, :])
    ktail = k * jnp.exp(glast - gcc)
    eye = jnp.where(ii == jj, 1.0, 0.0).astype(D_GDN)
    ktt = _bmm_nt(eye, _pair_blockdiag(ktail.astype(D_GDN), lo))
    split = lambda z: z.reshape((n_pairs, 2, cpt) + z.shape[1:])
    u_ref[0] = split(u)
    w_ref[0] = split(w.astype(D_GDN))
    qh_ref[0] = split((q * egc).astype(D_GDN))
    aqk_ref[0] = split(aqk.astype(D_GDN))
    ktt_ref[0] = split(ktt.astype(D_GDN))
    egl_ref[0] = split(jnp.exp(glast))


def _gdn_local(qn, kn, vv, sc, sct):
    nb, tt, b_w = qn.shape
    nh = sc.shape[1]
    n_pairs = b_w // LANES
    nt = tt // TILE
    nc = tt // GDN_CHUNK
    cpt = TILE // GDN_CHUNK
    tok = pl.BlockSpec((1, TILE, b_w), lambda b, i: (b, i, 0))
    chunked = lambda r: pl.BlockSpec((1, n_pairs, 2, cpt, r, LANES), lambda b, i: (b, 0, 0, i, 0, 0))
    shape = lambda r, dt: jax.ShapeDtypeStruct((nb, n_pairs, 2, nc, r, LANES), dt)
    return pl.pallas_call(
        _gdn_local_kernel,
        grid=(nb, nt),
        in_specs=[tok, tok, tok,
                  pl.BlockSpec((1, nh, TILE, SUBLANES), lambda b, i: (b, 0, i, 0)),
                  pl.BlockSpec((1, nh, SUBLANES, TILE), lambda b, i: (b, 0, 0, i))],
        out_specs=(chunked(GDN_CHUNK), chunked(GDN_CHUNK), chunked(GDN_CHUNK),
                   chunked(GDN_CHUNK), chunked(HEAD_DIM), chunked(1)),
        out_shape=(
            shape(GDN_CHUNK, F32),
            shape(GDN_CHUNK, D_GDN),
            shape(GDN_CHUNK, D_GDN),
            shape(GDN_CHUNK, D_GDN),
            shape(HEAD_DIM, D_GDN),
            shape(1, F32),
        ),
        compiler_params=_cparams(("parallel", "parallel")),
        name="gdn_local",
    )(qn, kn, vv, sc, sct)


def _gdn_scan_kernel(uf, ub, wf, wb, qf, qb, af, ab, kf, kb, ef, eb, of_ref, ob_ref, s_ref, *, cpt):
    @pl.when(pl.program_id(0) == 0)
    def _():
        s_ref[...] = jnp.zeros_like(s_ref)

    nb, n_pairs = uf.shape[:2]
    n = nb * n_pairs
    lo = lax.broadcasted_iota(jnp.int32, (1, 1, LANES), 2) < HEAD_DIM
    s = s_ref[...]
    for t in range(cpt):
        tb = cpt - 1 - t
        flat = lambda z: z.reshape((n,) + z.shape[2:])
        pair = lambda f, b: jnp.concatenate([flat(f[:, :, 0, t]), flat(b[:, :, 0, tb])], axis=0)
        sd = _pair_blockdiag(s.astype(D_GS), lo)
        v_new = pair(uf, ub) - _bmm(pair(wf, wb), sd)
        vd = _pair_blockdiag(v_new.astype(D_GS), lo)
        o = _bmm(pair(qf, qb), sd) + _bmm(pair(af, ab), vd)
        s = s * pair(ef, eb) + _bmm(pair(kf, kb), vd)
        of_ref[:, :, t] = o[:n].reshape((nb, n_pairs) + o.shape[1:])
        ob_ref[:, :, tb] = o[n:].reshape((nb, n_pairs) + o.shape[1:])
    s_ref[...] = s


def _gdn_scan(u, w, qh, aqk, ktt, egl, *, n_ctx_tiles):
    nb, n_pairs, _, nc, c, _ = u.shape
    cpt = TILE // c
    nt = nc // cpt

    def bwd_tile(j):
        return jnp.where(j < n_ctx_tiles, n_ctx_tiles - 1 - j, nt - 1 + n_ctx_tiles - j)

    def specs(a):
        blk = (nb, n_pairs, 1, cpt) + a.shape[4:]
        return [pl.BlockSpec(blk, lambda j: (0, 0, 0, j, 0, 0)),
                pl.BlockSpec(blk, lambda j: (0, 0, 1, bwd_tile(j), 0, 0))]

    args, in_specs = [], []
    for a in (u, w, qh, aqk, ktt, egl):
        args += [a, a]
        in_specs += specs(a)
    oshape = jax.ShapeDtypeStruct((nb, n_pairs, nc, c, LANES), F32)
    return pl.pallas_call(
        functools.partial(_gdn_scan_kernel, cpt=cpt),
        grid=(nt,),
        in_specs=in_specs,
        out_specs=(pl.BlockSpec((nb, n_pairs, cpt, c, LANES), lambda j: (0, 0, j, 0, 0)),
                   pl.BlockSpec((nb, n_pairs, cpt, c, LANES), lambda j: (0, 0, bwd_tile(j), 0, 0))),
        out_shape=(oshape, oshape),
        scratch_shapes=[pltpu.VMEM((2 * nb * n_pairs, HEAD_DIM, LANES), F32)],
        compiler_params=_cparams(("arbitrary",)),
        name="gdn_scan",
    )(*args)


def _mix_ffn_kernel(x_ref, mod_ref, ya_ref, of_ref, ob_ref, gate_ref, gnw_ref, yc_ref, w_ref,
                    nw_ref, wi_ref, wo_ref, fnw_ref, o_ref, *, d_ff, final):
    mod = mod_ref[0]
    lo = lax.broadcasted_iota(jnp.int32, (TILE, LANES), 1) < HEAD_DIM
    pairs = []
    for p in range(of_ref.shape[1]):
        o = of_ref[0, p] + ob_ref[0, p]
        sq = o * o
        ms_lo = jnp.sum(jnp.where(lo, sq, 0.0), axis=-1, keepdims=True) * (1.0 / HEAD_DIM)
        ms_hi = jnp.sum(jnp.where(lo, 0.0, sq), axis=-1, keepdims=True) * (1.0 / HEAD_DIM)
        pairs.append(o * lax.rsqrt(jnp.where(lo, ms_lo, ms_hi) + EPS) * gnw_ref[...])
    yb = (jnp.concatenate(pairs, axis=1) * _silu(gate_ref[0])).astype(D_OUT)
    y = jnp.concatenate([ya_ref[0], yb, yc_ref[0]], axis=1)
    x = x_ref[0] + mod[2:3] * _dot(y, w_ref[0])

    h = _norm_mod(x, nw_ref[...], mod[3:4], mod[4:5]).astype(D_FFN)
    gu = _dot(h, wi_ref[0])
    a = (_silu(gu[:, :d_ff]) * gu[:, d_ff:]).astype(D_FFN)
    out = x + mod[5:6] * _dot(a, wo_ref[0])
    if final:
        out = out * lax.rsqrt(jnp.mean(out * out, axis=-1, keepdims=True) + EPS) * fnw_ref[...]
    o_ref[0] = out


def _mix_ffn(xs, mods, ya, scan_f, scan_b, gate, gnw2, yc, w, nw, wi, wo, fnw,
             *, layer, tile0, final, n_batch, n_ctx_tiles):
    nb, tt, d = xs.shape
    nt = tt // TILE - tile0
    a_w, b_w, c_w = ya.shape[2], gate.shape[2], yc.shape[2]
    n_pairs = scan_f.shape[1]
    d_ff = wo.shape[1]

    def mod_idx(b, i):
        return (jnp.where(i + tile0 < n_ctx_tiles, n_batch, b), 0, 0)

    tok = lambda wd: pl.BlockSpec((1, TILE, wd), lambda b, i: (b, i + tile0, 0))
    pairs = pl.BlockSpec((1, n_pairs, TILE, LANES), lambda b, i: (b, 0, i + tile0, 0))
    vec = lambda wd: pl.BlockSpec((1, wd), lambda b, i: (0, 0))
    weight = lambda arr: pl.BlockSpec((1,) + arr.shape[1:], lambda b, i: (layer, 0, 0), pipeline_mode=pl.Buffered(1))
    return pl.pallas_call(
        functools.partial(_mix_ffn_kernel, d_ff=d_ff, final=final),
        grid=(nb, nt),
        in_specs=[tok(d), pl.BlockSpec((1, 6, d), mod_idx), tok(a_w), pairs, pairs, tok(b_w), vec(LANES), tok(c_w),
                  weight(w), vec(d), weight(wi), weight(wo), vec(d)],
        out_specs=pl.BlockSpec((1, TILE, d), lambda b, i: (b, i, 0)),
        out_shape=jax.ShapeDtypeStruct((nb, nt * TILE, d), F32),
        compiler_params=_cparams(("parallel", "parallel")),
        name="mix_ffn",
    )(xs, mods, ya, scan_f, scan_b, gate, gnw2, yc, w, nw, wi, wo, fnw)


def _rope_tables(seq, ctx_len, n_rep):
    half = A_QK // 2
    inv_freq = 1.0 / (ROPE_THETA ** (jnp.arange(0, half, 2, dtype=F32) / half))
    t = jnp.arange(seq, dtype=jnp.int32)
    ang_r = (t // GRID_W).astype(F32)[:, None] * inv_freq
    ang_c = (t % GRID_W).astype(F32)[:, None] * inv_freq
    ang = jnp.concatenate([ang_r, ang_r, ang_c, ang_c], axis=-1)
    cos = jnp.concatenate([jnp.ones((ctx_len, A_QK), F32), jnp.cos(ang)], axis=0)
    sin = jnp.concatenate([jnp.zeros((ctx_len, A_QK), F32), jnp.sin(ang)], axis=0)
    return jnp.tile(cos, (1, n_rep)), jnp.tile(sin, (1, n_rep))


def kernel(x, c, ctx, c_ctx, w_mod, b_mod, norm1_w, norm2_w, w_in, w_out, lambda_q1, lambda_k1, lambda_q2,
           lambda_k2, diff_norm_w, conv_w, a_log, dt_bias, gdn_norm_w, na_bias, w_ffn_in, w_ffn_out,
           final_norm_w):
    nb, seq, d = x.shape
    ctx_len = ctx.shape[1]
    depth = w_mod.shape[0]
    b_heads = a_log.shape[-1]
    c_heads = na_bias.shape[1]
    b_w = b_heads * HEAD_DIM
    c_w = c_heads * HEAD_DIM
    a_w = w_out.shape[1] - b_w - c_w
    assert seq % TILE == 0 and ctx_len % TILE == 0 and seq % GRID_W == 0
    assert a_w % LANES == 0 and b_w % LANES == 0 and c_w % LANES == 0 and 4 * b_heads <= LANES
    n_ctx_tiles = ctx_len // TILE

    xs = jnp.concatenate([ctx, x], axis=1)

    n_rows = -(-(nb + 1) // SUBLANES) * SUBLANES
    cc = jnp.concatenate([c, c_ctx[None, :], jnp.zeros((n_rows - nb - 1, d), F32)], axis=0)
    mods_all = _modulation(cc, w_mod, b_mod).reshape(depth, n_rows, 6, d)

    cos, sin = _rope_tables(seq, ctx_len, a_w // A_QK)
    cost, sint = cos.T, sin.T

    sizes = (3 * a_w, 3 * b_w, b_w, 2 * b_heads, 2 * b_heads, 3 * c_w)
    offs = np.concatenate([[0], np.cumsum(sizes)])
    o_a, o_b, o_g, o_al, o_be, o_c = (int(v) for v in offs[:6])
    pad = LANES - 4 * b_heads

    wm_all = jnp.concatenate([
        w_in[:, :, o_a:o_a + a_w],
        w_in[:, :, o_a + 2 * a_w:o_a + 3 * a_w],
        w_in[:, :, o_b:o_b + 3 * b_w],
        w_in[:, :, o_g:o_g + b_w],
        w_in[:, :, o_c:o_c + c_w],
        w_in[:, :, o_c + 2 * c_w:o_c + 3 * c_w],
        w_in[:, :, o_al:o_al + 4 * b_heads],
        jnp.zeros((depth, d, pad), F32),
    ], axis=2).astype(D_IN)
    wk_all = jnp.swapaxes(jnp.concatenate([w_in[:, :, o_a + a_w:o_a + 2 * a_w],
                                           w_in[:, :, o_c + c_w:o_c + 2 * c_w]], axis=2), 1, 2).astype(D_IN)
    w_out_all = w_out.astype(D_OUT)
    w_ffn_in_all = w_ffn_in.astype(D_FFN)
    w_ffn_out_all = w_ffn_out.astype(D_FFN)
    bias_all = _na_bias_table(na_bias, seq)
    nw2_all = jnp.tile(diff_norm_w, (1, LANES // HEAD_DIM))
    gnw2_all = jnp.tile(gdn_norm_w, (1, LANES // HEAD_DIM))
    zpad = jnp.zeros((depth, LANES - 2 * b_heads), F32)
    alog_all = jnp.concatenate([a_log.reshape(depth, -1), zpad], axis=1)
    dtb_all = jnp.concatenate([dt_bias.reshape(depth, -1), zpad], axis=1)
    fnw = final_norm_w[None, :]

    for l in range(depth):
        last = l == depth - 1
        mods = mods_all[l]
        lam_init = 0.8 - 0.6 * math.exp(-0.3 * l)

        qa, kat, va, qkvb, gate, ab, qc, kct, vc = _inproj(
            xs, mods, norm1_w[l][None, :], wm_all, wk_all, cos, sin, cost, sint,
            layer=l, n_batch=nb, n_ctx_tiles=n_ctx_tiles, a_w=a_w, b_w=b_w, c_w=c_w)

        lam_vecs = jnp.stack([lambda_q1[l], lambda_k1[l], lambda_q2[l], lambda_k2[l]], axis=0).astype(F32)
        ya = _diffattn(lam_vecs, nw2_all[l][None, :], qa, kat, va, n_ctx_tiles=n_ctx_tiles, lam_init=lam_init)

        yc = _na(qc, kct, vc, bias_all, layer=l, n_ctx_tiles=n_ctx_tiles)

        qn, kn, vv, sc, sct = _gdn_prep(qkvb, ab, conv_w[l], alog_all[l][None, :], dtb_all[l][None, :],
                                        n_ctx_tiles=n_ctx_tiles, n_heads=b_heads)
        u, w, qh, aqk, ktt, egl = _gdn_local(qn, kn, vv, sc, sct)
        scan_f, scan_b = _gdn_scan(u, w, qh, aqk, ktt, egl, n_ctx_tiles=n_ctx_tiles)
        scan_f = scan_f.reshape(nb, b_w // LANES, ctx_len + seq, LANES)
        scan_b = scan_b.reshape(nb, b_w // LANES, ctx_len + seq, LANES)

        tile0 = n_ctx_tiles if last else 0
        xs = _mix_ffn(xs, mods, ya, scan_f, scan_b, gate, gnw2_all[l][None, :], yc, w_out_all,
                      norm2_w[l][None, :], w_ffn_in_all, w_ffn_out_all, fnw,
                      layer=l, tile0=tile0, final=last, n_batch=nb, n_ctx_tiles=n_ctx_tiles)

    return xs
```

```python
import functools
import math

import jax
import jax.numpy as jnp
import numpy as np
from jax import lax
from jax.experimental import pallas as pl
from jax.experimental.pallas import tpu as pltpu

F32 = jnp.float32
BF16 = jnp.bfloat16
D_MOD = D_IN = D_OUT = D_FFN = D_ATT = D_NA = D_GDN = BF16
D_GA = D_GI = D_GT = D_GS = BF16

HEAD_DIM = 64
A_QK = HEAD_DIM // 2
GRID_W = 64
ROPE_THETA = 10000.0
CONV_W = 5
GDN_CHUNK = 64
WIN_ROWS = 8
WIN_COLS = 16
EPS = 1e-6

LANES = 128
SUBLANES = 8
TILE = 256
NA_WIN_TILES = 3
NEG = -1e30
VMEM_LIMIT = 56 * 1024 * 1024


def _cparams(sem):
    return pltpu.CompilerParams(dimension_semantics=sem, vmem_limit_bytes=VMEM_LIMIT)


def _silu(x):
    return x * (1.0 / (1.0 + jnp.exp(-x)))


def _dot(a, b):
    return jnp.dot(a, b, preferred_element_type=F32)


def _dot_nt(a, b):
    return lax.dot_general(a, b, (((1,), (1,)), ((), ())), preferred_element_type=F32)


def _split3(x):
    hi = x.astype(BF16)
    r1 = x - hi.astype(F32)
    mid = r1.astype(BF16)
    lo = (r1 - mid.astype(F32)).astype(BF16)
    return hi, mid, lo


def _dot_sel_right(x, sel):
    hi, mid, lo = _split3(x)
    return _dot(hi, sel) + _dot(mid, sel) + _dot(lo, sel)


def _dot_sel_left(sel, x):
    hi, mid, lo = _split3(x)
    return _dot(sel, hi) + _dot(sel, mid) + _dot(sel, lo)


def _mod_kernel(c_ref, w_ref, b_ref, o_ref):
    s = _silu(c_ref[...])
    o_ref[0] = _dot(s.astype(D_MOD), w_ref[0].astype(D_MOD)) + b_ref[0]


def _modulation(cc, w_mod, b_mod):
    depth, d, n = w_mod.shape
    rows = cc.shape[0]
    tn = 1536
    assert n % tn == 0
    return pl.pallas_call(
        _mod_kernel,
        grid=(depth, n // tn),
        in_specs=[
            pl.BlockSpec((rows, d), lambda l, j: (0, 0)),
            pl.BlockSpec((1, d, tn), lambda l, j: (l, 0, j)),
            pl.BlockSpec((1, 1, tn), lambda l, j: (l, 0, j)),
        ],
        out_specs=pl.BlockSpec((1, rows, tn), lambda l, j: (l, 0, j)),
        out_shape=jax.ShapeDtypeStruct((depth, rows, n), F32),
        compiler_params=_cparams(("parallel", "parallel")),
        name="modulation",
    )(cc, w_mod, b_mod.reshape(depth, 1, n))


def _norm_mod(x, nw, shift, scale):
    ms = jnp.mean(x * x, axis=-1, keepdims=True)
    y = x * lax.rsqrt(ms + EPS) * nw
    return y * (1.0 + scale) + shift


def _rope_lanes(x, cos, sin):
    parts = []
    for j in range(x.shape[1] // LANES):
        xj = x[:, LANES * j:LANES * (j + 1)]
        nxt = pltpu.roll(xj, LANES - 8, axis=1)
        prv = pltpu.roll(xj, 8, axis=1)
        lane = lax.broadcasted_iota(jnp.int32, xj.shape, 1)
        parts.append(jnp.where((lane % 16) < 8, -nxt, prv))
    rot = jnp.concatenate(parts, axis=1)
    return x * cos + rot * sin


def _rope_rows(x, cos, sin):
    n = x.shape[0]
    nxt = pltpu.roll(x, n - 8, axis=0)
    prv = pltpu.roll(x, 8, axis=0)
    row = lax.broadcasted_iota(jnp.int32, x.shape, 0)
    rot = jnp.where((row % 16) < 8, -nxt, prv)
    return x * cos + rot * sin


def _inproj_kernel(x_ref, mod_ref, nw_ref, wm_ref, wk_ref, cos_ref, sin_ref, cost_ref, sint_ref,
                   qa_ref, kat_ref, va_ref, qkvb_ref, gate_ref, ab_ref, qc_ref, kct_ref, vc_ref,
                   *, a_w, b_w, c_w):
    mod = mod_ref[0]
    h = _norm_mod(x_ref[0], nw_ref[...], mod[0:1], mod[1:2]).astype(D_IN)
    acc = _dot(h, wm_ref[0])
    kt = _dot_nt(wk_ref[0], h)
    o = 0
    qa = _rope_lanes(acc[:, o:o + a_w], cos_ref[...], sin_ref[...])
    qa_ref[0] = (qa * (A_QK ** -0.5 * math.log2(math.e))).astype(D_ATT)
    o += a_w
    va_ref[0] = acc[:, o:o + a_w].astype(D_ATT)
    o += a_w
    qkvb_ref[0] = acc[:, o:o + 3 * b_w]
    o += 3 * b_w
    gate_ref[0] = acc[:, o:o + b_w]
    o += b_w
    qc_ref[0] = (acc[:, o:o + c_w] * (HEAD_DIM ** -0.5 * math.log2(math.e))).astype(D_NA)
    o += c_w
    vc_ref[0] = acc[:, o:o + c_w].astype(D_NA)
    o += c_w
    ab_ref[0] = acc[:, o:o + LANES]
    kat_ref[0, 0] = _rope_rows(kt[:a_w], cost_ref[...], sint_ref[...]).astype(D_ATT)
    kct_ref[0, 0] = kt[a_w:].astype(D_NA)


def _inproj(xs, mods, nw, wm, wk, cos, sin, cost, sint, *, layer, n_batch, n_ctx_tiles, a_w, b_w, c_w):
    nb, tt, d = xs.shape
    nt = tt // TILE
    ctx_row = n_batch

    def mod_idx(b, i):
        return (jnp.where(i < n_ctx_tiles, ctx_row, b), 0, 0)

    tok = lambda w: pl.BlockSpec((1, TILE, w), lambda b, i: (b, i, 0))
    tokt = lambda w: pl.BlockSpec((1, 1, w, TILE), lambda b, i: (b, i, 0, 0))
    out_shape = (
        jax.ShapeDtypeStruct((nb, tt, a_w), D_ATT),
        jax.ShapeDtypeStruct((nb, nt, a_w, TILE), D_ATT),
        jax.ShapeDtypeStruct((nb, tt, a_w), D_ATT),
        jax.ShapeDtypeStruct((nb, tt, 3 * b_w), F32),
        jax.ShapeDtypeStruct((nb, tt, b_w), F32),
        jax.ShapeDtypeStruct((nb, tt, LANES), F32),
        jax.ShapeDtypeStruct((nb, tt, c_w), D_NA),
        jax.ShapeDtypeStruct((nb, nt, c_w, TILE), D_NA),
        jax.ShapeDtypeStruct((nb, tt, c_w), D_NA),
    )
    return pl.pallas_call(
        functools.partial(_inproj_kernel, a_w=a_w, b_w=b_w, c_w=c_w),
        grid=(nb, nt),
        in_specs=[
            tok(d),
            pl.BlockSpec((1, 6, d), mod_idx),
            pl.BlockSpec((1, d), lambda b, i: (0, 0)),
            pl.BlockSpec((1,) + wm.shape[1:], lambda b, i: (layer, 0, 0)),
            pl.BlockSpec((1,) + wk.shape[1:], lambda b, i: (layer, 0, 0)),
            pl.BlockSpec((TILE, a_w), lambda b, i: (i, 0)),
            pl.BlockSpec((TILE, a_w), lambda b, i: (i, 0)),
            pl.BlockSpec((a_w, TILE), lambda b, i: (0, i)),
            pl.BlockSpec((a_w, TILE), lambda b, i: (0, i)),
        ],
        out_specs=(tok(a_w), tokt(a_w), tok(a_w), tok(3 * b_w), tok(b_w), tok(LANES),
                   tok(c_w), tokt(c_w), tok(c_w)),
        out_shape=out_shape,
        compiler_params=_cparams(("parallel", "parallel")),
        name="inproj",
    )(xs, mods, nw, wm, wk, cos, sin, cost, sint)


def _diffattn_kernel(lam_ref, nw_ref, q_ref, kt_ref, v_ref, o_ref, mx_ref, acc_ref,
                     *, n_ctx_tiles, n_key_tiles, lam_init):
    i = pl.program_id(2)
    lv = lam_ref[...]
    lam = (jnp.exp(jnp.sum(lv[0:1] * lv[1:2], axis=-1, keepdims=True))
           - jnp.exp(jnp.sum(lv[2:3] * lv[3:4], axis=-1, keepdims=True)) + lam_init)
    q = q_ref[0]
    tq = q.shape[0]
    lane = lax.broadcasted_iota(jnp.int32, (tq, LANES), 1)
    lo = lane < HEAD_DIM
    zero = jnp.zeros_like(q)
    qm = [jnp.where((lane // A_QK) == m, q, zero) for m in range(LANES // A_QK)]
    n_maps = len(qm)
    n_lat = n_key_tiles - n_ctx_tiles
    group = max(u for u in (16, 8, 4, 2, 1) if n_lat % u == 0)
    ctx_tiles = list(range(n_ctx_tiles))
    lat_tiles = lambda g: [n_ctx_tiles + g * group + u for u in range(group)]

    lane_v = lax.broadcasted_iota(jnp.int32, (TILE, LANES), 1)

    def visit(tiles, first):
        ktiles = [kt_ref[0, j] for j in tiles]
        v_ext = []
        for j in tiles:
            start = j * TILE if isinstance(j, int) else pl.multiple_of(j * TILE, TILE)
            vtile = v_ref[0, pl.ds(start, TILE), :]
            ones = jnp.ones_like(vtile)
            v_ext.append((jnp.where(lane_v < HEAD_DIM, vtile, ones), jnp.where(lane_v < HEAD_DIM, ones, vtile)))
        scores = [[_dot(qm[m], ktile) for ktile in ktiles] for m in range(n_maps)]
        m_new = []
        for m in range(n_maps):
            top = scores[m][0]
            for s in scores[m][1:]:
                top = jnp.maximum(top, s)
            top = jnp.max(jnp.maximum(top[:, :LANES], top[:, LANES:]), axis=-1, keepdims=True)
            m_new.append(top if first else jnp.maximum(mx_ref[m], top))
        probs = [[jnp.exp2(s - m_new[m]).astype(D_ATT) for s in scores[m]] for m in range(n_maps)]
        for m in range(n_maps):
            acc = None if first else acc_ref[m] * jnp.exp2(mx_ref[m] - m_new[m])
            for p, ve in zip(probs[m], v_ext):
                pv = _dot(p, ve[m // 2])
                acc = pv if acc is None else acc + pv
            mx_ref[m] = m_new[m]
            acc_ref[m] = acc

    @pl.when(i < n_ctx_tiles)
    def _():
        visit(ctx_tiles, True)

    @pl.when(i >= n_ctx_tiles)
    def _():
        visit(ctx_tiles + lat_tiles(0), True)
        lax.fori_loop(1, n_lat // group, lambda g, c: (visit(lat_tiles(g), False), c)[1], 0)
    attn = []
    for m in range(n_maps):
        a = acc_ref[m]
        attn.append(a / pltpu.roll(a, HEAD_DIM, axis=1))
    o = jnp.where(lo, attn[0] - lam * attn[1], attn[2] - lam * attn[3])
    sq = o * o
    ms_lo = jnp.sum(jnp.where(lo, sq, 0.0), axis=-1, keepdims=True) * (1.0 / HEAD_DIM)
    ms_hi = jnp.sum(jnp.where(lo, 0.0, sq), axis=-1, keepdims=True) * (1.0 / HEAD_DIM)
    y = o * lax.rsqrt(jnp.where(lo, ms_lo, ms_hi) + EPS) * nw_ref[...]
    o_ref[0] = (y * (1.0 - lam_init)).astype(D_OUT)


def _diffattn(lam_vecs, nw2, qa, kat, va, *, n_ctx_tiles, lam_init):
    nb, tt, a_w = qa.shape
    nt = tt // TILE
    return pl.pallas_call(
        functools.partial(_diffattn_kernel, n_ctx_tiles=n_ctx_tiles, n_key_tiles=nt, lam_init=lam_init),
        grid=(nb, a_w // LANES, nt),
        in_specs=[
            pl.BlockSpec(lam_vecs.shape, lambda b, p, i: (0, 0)),
            pl.BlockSpec((1, LANES), lambda b, p, i: (0, 0)),
            pl.BlockSpec((1, TILE, LANES), lambda b, p, i: (b, i, p)),
            pl.BlockSpec((1, nt, LANES, TILE), lambda b, p, i: (b, 0, p, 0)),
            pl.BlockSpec((1, tt, LANES), lambda b, p, i: (b, 0, p)),
        ],
        out_specs=pl.BlockSpec((1, TILE, LANES), lambda b, p, i: (b, i, p)),
        out_shape=jax.ShapeDtypeStruct((nb, tt, a_w), D_OUT),
        scratch_shapes=[pltpu.VMEM((LANES // A_QK, TILE, 1), F32),
                        pltpu.VMEM((LANES // A_QK, TILE, LANES), F32)],
        compiler_params=_cparams(("parallel", "parallel", "parallel")),
        name="diffattn",
    )(lam_vecs, nw2, qa, kat, va)


def _na_kernel(q_ref, kt_ref, v_ref, bias_ref, o_ref, *, n_ctx_tiles, n_key_tiles):
    i = pl.program_id(1)
    n_lat = n_key_tiles - n_ctx_tiles
    w0 = n_ctx_tiles + jnp.clip(i - n_ctx_tiles - 1, 0, n_lat - NA_WIN_TILES)
    tq = q_ref.shape[1]
    n_pairs = q_ref.shape[2] // LANES
    lane = lax.broadcasted_iota(jnp.int32, (tq, LANES), 1)
    lo = lane < HEAD_DIM
    heads = [(p, hd) for p in range(n_pairs) for hd in range(2)]
    qh = []
    for p in range(n_pairs):
        q = q_ref[0, :, LANES * p:LANES * (p + 1)]
        zero = jnp.zeros_like(q)
        qh += [jnp.where(lo, q, zero), jnp.where(lo, zero, q)]
    tiles = [w0 + c for c in range(NA_WIN_TILES)] + list(range(n_ctx_tiles))
    scores = []
    for h, (p, hd) in enumerate(heads):
        row = []
        for c, t in enumerate(tiles):
            s = _dot(qh[h], kt_ref[0, t, LANES * p:LANES * (p + 1), :])
            if c < NA_WIN_TILES:
                s = s + bias_ref[0, h, 0, :, TILE * c:TILE * (c + 1)]
            row.append(s)
        scores.append(row)
    tops = []
    for row in scores:
        top = row[0]
        for s in row[1:]:
            top = jnp.maximum(top, s)
        tops.append(jnp.max(jnp.maximum(top[:, :LANES], top[:, LANES:]), axis=-1, keepdims=True))
    probs = [[jnp.exp2(s - top).astype(D_NA) for s in row] for row, top in zip(scores, tops)]
    lane_v = lax.broadcasted_iota(jnp.int32, (TILE, LANES), 1)
    accs = [jnp.zeros((tq, LANES), F32) for _ in heads]
    for c, t in enumerate(tiles):
        start = t * TILE if isinstance(t, int) else pl.multiple_of(t * TILE, TILE)
        for p in range(n_pairs):
            vt = v_ref[0, pl.ds(start, TILE), LANES * p:LANES * (p + 1)]
            ones = jnp.ones_like(vt)
            v_ext = (jnp.where(lane_v < HEAD_DIM, vt, ones), jnp.where(lane_v < HEAD_DIM, ones, vt))
            for hd in range(2):
                h = 2 * p + hd
                accs[h] = accs[h] + _dot(probs[h][c], v_ext[hd])
    outs = [a / pltpu.roll(a, HEAD_DIM, axis=1) for a in accs]
    o_ref[0] = jnp.concatenate([jnp.where(lo, outs[2 * p], outs[2 * p + 1]) for p in range(n_pairs)],
                               axis=1).astype(D_OUT)


def _na(qc, kct, vc, bias, *, layer, n_ctx_tiles):
    nb, tt, c_w = qc.shape
    nt = tt // TILE
    n_lat = nt - n_ctx_tiles

    def bias_idx(b, i):
        il = i - n_ctx_tiles
        cls = jnp.where(il < 0, 3, jnp.where(il == 0, 0, jnp.where(il == n_lat - 1, 2, 1)))
        return (layer, 0, cls, 0, 0)

    return pl.pallas_call(
        functools.partial(_na_kernel, n_ctx_tiles=n_ctx_tiles, n_key_tiles=nt),
        grid=(nb, nt),
        in_specs=[
            pl.BlockSpec((1, TILE, c_w), lambda b, i: (b, i, 0)),
            pl.BlockSpec((1, nt, c_w, TILE), lambda b, i: (b, 0, 0, 0)),
            pl.BlockSpec((1, tt, c_w), lambda b, i: (b, 0, 0)),
            pl.BlockSpec((1, bias.shape[1], 1, TILE, NA_WIN_TILES * TILE), bias_idx),
        ],
        out_specs=pl.BlockSpec((1, TILE, c_w), lambda b, i: (b, i, 0)),
        out_shape=jax.ShapeDtypeStruct((nb, tt, c_w), D_OUT),
        compiler_params=_cparams(("parallel", "parallel")),
        name="nbr_attn",
    )(qc, kct, vc, bias)


def _na_bias_kernel(u_ref, o_ref, *, plan):
    neg = jnp.full((GRID_W, GRID_W), NEG, F32)
    for cls, rows_plan in enumerate(plan):
        for qr, row_plan in enumerate(rows_plan):
            blocks = [neg if dr is None else u_ref[0, 0, dr] for dr in row_plan]
            o_ref[0, 0, cls, GRID_W * qr:GRID_W * (qr + 1), :] = jnp.concatenate(blocks, axis=1)
    o_ref[0, 0, len(plan)] = jnp.full(o_ref.shape[3:], NEG, F32)


def _na_bias_table(rel_bias, seq):
    rows = seq // GRID_W
    rpt = TILE // GRID_W
    wrows = NA_WIN_TILES * rpt
    assert rows >= wrows and rows >= WIN_ROWS
    n_layers, n_heads = rel_bias.shape[:2]
    n_dr, n_dc = 2 * WIN_ROWS - 1, 2 * WIN_COLS - 1
    qcol, kcol = np.arange(GRID_W)[:, None], np.arange(GRID_W)[None, :]
    kc0 = np.clip(qcol - WIN_COLS // 2, 0, GRID_W - WIN_COLS)
    col_ok = (kcol >= kc0) & (kcol < kc0 + WIN_COLS)
    dcol = kcol - qcol + (WIN_COLS - 1)
    sel_c = ((dcol[None] == np.arange(n_dc)[:, None, None]) & col_ok[None]).astype(np.float32)
    blocks = jnp.einsum("lhdc,cqk->lhdqk", rel_bias.astype(F32) * math.log2(math.e), sel_c,
                        precision=lax.Precision.HIGHEST)
    blocks = jnp.where(jnp.asarray(col_ok), blocks, NEG)
    n_lat = rows // rpt
    plan = []
    for il in (0, min(1, n_lat - 1), n_lat - 1):
        ws = int(np.clip(il - 1, 0, n_lat - NA_WIN_TILES)) * rpt
        rows_plan = []
        for qr in range(rpt):
            r_abs = il * rpt + qr
            kr0 = int(np.clip(r_abs - WIN_ROWS // 2, 0, rows - WIN_ROWS))
            rows_plan.append(tuple((ws + kr) - r_abs + (WIN_ROWS - 1) if kr0 <= ws + kr < kr0 + WIN_ROWS else None
                                   for kr in range(wrows)))
        plan.append(tuple(rows_plan))
    n_cls = len(plan) + 1
    return pl.pallas_call(
        functools.partial(_na_bias_kernel, plan=tuple(plan)),
        grid=(n_layers, n_heads),
        in_specs=[pl.BlockSpec((1, 1, n_dr, GRID_W, GRID_W), lambda l, h: (l, h, 0, 0, 0))],
        out_specs=pl.BlockSpec((1, 1, n_cls, TILE, NA_WIN_TILES * TILE), lambda l, h: (l, h, 0, 0, 0)),
        out_shape=jax.ShapeDtypeStruct((n_layers, n_heads, n_cls, TILE, NA_WIN_TILES * TILE), F32),
        compiler_params=_cparams(("parallel", "parallel")),
        name="nbr_bias_table",
    )(blocks)


def _gdn_prep_kernel(x_ref, prev_ref, next_ref, ab_ref, cw_ref, alog_ref, dtb_ref,
                     q_ref, k_ref, v_ref, sc_ref, sct_ref, *, n_ctx_tiles, n_tiles, n_heads):
    i = pl.program_id(1)
    first = jnp.logical_or(i == 0, i == n_ctx_tiles)
    last = jnp.logical_or(i == n_ctx_tiles - 1, i == n_tiles - 1)
    prev = jnp.where(first, 0.0, prev_ref[0, 0])
    nxt = jnp.where(last, 0.0, next_ref[0, 0])
    ext = jnp.concatenate([prev, x_ref[0], nxt], axis=0)
    cw = cw_ref[...]
    y = None
    for j in range(CONV_W):
        o = SUBLANES - CONV_W // 2 + j
        term = ext[o:o + TILE] * cw[j:j + 1]
        y = term if y is None else y + term
    y = _silu(y)
    b_w = n_heads * HEAD_DIM
    lo = lax.broadcasted_iota(jnp.int32, (TILE, LANES), 1) < HEAD_DIM

    def l2norm_heads(z):
        parts = []
        for p in range(z.shape[1] // LANES):
            zp = z[:, LANES * p:LANES * (p + 1)]
            sq = zp * zp
            s_lo = jnp.sum(jnp.where(lo, sq, 0.0), axis=-1, keepdims=True)
            s_hi = jnp.sum(jnp.where(lo, 0.0, sq), axis=-1, keepdims=True)
            parts.append(zp * lax.rsqrt(jnp.where(lo, s_lo, s_hi) + EPS))
        return jnp.concatenate(parts, axis=1)

    q_ref[0] = l2norm_heads(y[:, :b_w]) * (HEAD_DIM ** -0.5)
    k_ref[0] = l2norm_heads(y[:, b_w:2 * b_w])
    v_ref[0] = y[:, 2 * b_w:]

    ab = ab_ref[0]
    lane = lax.broadcasted_iota(jnp.int32, ab.shape, 1)
    z = ab + dtb_ref[...]
    softplus = jnp.maximum(z, 0.0) + jnp.log(1.0 + jnp.exp(-jnp.abs(z)))
    g = jnp.where(lane < 2 * n_heads, -jnp.exp(alog_ref[...]) * softplus, 0.0)
    beta = 1.0 / (1.0 + jnp.exp(-ab))
    r = lax.broadcasted_iota(jnp.int32, (TILE, TILE), 0)
    c = lax.broadcasted_iota(jnp.int32, (TILE, TILE), 1)
    same = (r // GDN_CHUNK) == (c // GDN_CHUNK)
    tri_f = jnp.where(jnp.logical_and(same, c <= r), 1.0, 0.0).astype(BF16)
    tri_b = jnp.where(jnp.logical_and(same, c >= r), 1.0, 0.0).astype(BF16)
    gc = jnp.where(lane < n_heads, _dot_sel_left(tri_f, g), _dot_sel_left(tri_b, g))
    rr = lax.broadcasted_iota(jnp.int32, (LANES, LANES), 0)
    cc = lax.broadcasted_iota(jnp.int32, (LANES, LANES), 1)
    hh, jj = cc // SUBLANES, cc % SUBLANES
    valid = hh < n_heads
    sel_gb = jnp.where(jnp.logical_and(valid, jnp.logical_and(jj < 4, rr == jj * n_heads + hh)), 1.0, 0.0).astype(BF16)
    sel_gc = jnp.where(jnp.logical_and(valid, jnp.logical_and(jnp.logical_and(jj >= 4, jj < 6),
                                                              rr == (jj - 4) * n_heads + hh)), 1.0, 0.0).astype(BF16)
    gb = jnp.where(lane < 2 * n_heads, g, beta)
    packed = _dot_sel_right(gb, sel_gb) + _dot_sel_right(gc, sel_gc)
    packed_t = packed.T
    for h in range(n_heads):
        sc_ref[0, h] = packed[:, SUBLANES * h:SUBLANES * (h + 1)]
        sct_ref[0, h] = packed_t[SUBLANES * h:SUBLANES * (h + 1), :]


def _gdn_prep(qkvb, ab, conv_w, alog_v, dtb_v, *, n_ctx_tiles, n_heads):
    nb, tt, w3 = qkvb.shape
    nt = tt // TILE
    rows8 = TILE // SUBLANES
    x8 = qkvb.reshape(nb, tt // SUBLANES, SUBLANES, w3)
    hd = lambda w: pl.BlockSpec((1, n_heads, TILE, w), lambda b, i: (b, 0, i, 0))
    tok = pl.BlockSpec((1, TILE, w3 // 3), lambda b, i: (b, i, 0))
    return pl.pallas_call(
        functools.partial(_gdn_prep_kernel, n_ctx_tiles=n_ctx_tiles, n_tiles=nt, n_heads=n_heads),
        grid=(nb, nt),
        in_specs=[
            pl.BlockSpec((1, TILE, w3), lambda b, i: (b, i, 0)),
            pl.BlockSpec((1, 1, SUBLANES, w3), lambda b, i: (b, jnp.maximum(i * rows8 - 1, 0), 0, 0)),
            pl.BlockSpec((1, 1, SUBLANES, w3), lambda b, i: (b, jnp.minimum((i + 1) * rows8, nt * rows8 - 1), 0, 0)),
            pl.BlockSpec((1, TILE, LANES), lambda b, i: (b, i, 0)),
            pl.BlockSpec((CONV_W, w3), lambda b, i: (0, 0)),
            pl.BlockSpec((1, LANES), lambda b, i: (0, 0)),
            pl.BlockSpec((1, LANES), lambda b, i: (0, 0)),
        ],
        out_specs=(tok, tok, tok, hd(SUBLANES),
                   pl.BlockSpec((1, n_heads, SUBLANES, TILE), lambda b, i: (b, 0, 0, i))),
        out_shape=(
            jax.ShapeDtypeStruct((nb, tt, w3 // 3), F32),
            jax.ShapeDtypeStruct((nb, tt, w3 // 3), F32),
            jax.ShapeDtypeStruct((nb, tt, w3 // 3), F32),
            jax.ShapeDtypeStruct((nb, n_heads, tt, SUBLANES), F32),
            jax.ShapeDtypeStruct((nb, n_heads, SUBLANES, tt), F32),
        ),
        compiler_params=_cparams(("parallel", "parallel")),
        name="gdn_prep",
    )(qkvb, x8, x8, ab, conv_w, alog_v, dtb_v)


def _bmm(a, b):
    return jnp.einsum("nik,nkj->nij", a, b, preferred_element_type=F32)


def _bmm_nt(a, b):
    return jnp.einsum("nik,njk->nij", a, b, preferred_element_type=F32)


def _pair_blockdiag(x, lo):
    zero = jnp.zeros_like(x)
    return jnp.concatenate([jnp.where(lo, x, zero), jnp.where(lo, zero, x)], axis=1)


def _pmm(p, x, lo, dtype):
    return _bmm(p.astype(dtype), _pair_blockdiag(x.astype(dtype), lo))


def _tri_inverse(lmat, ii, jj, lo):
    blk = 16
    same = (ii // blk) == (jj // blk)
    dmat = jnp.where(same, lmat, 0.0)
    x = jnp.where(ii == jj, 1.0, 0.0) - dmat
    p = _pmm(dmat, dmat, lo, D_GI)
    n_sq = int(math.log2(blk)) - 1
    for step in range(n_sq):
        x = x + _pmm(p, x, lo, D_GI)
        if step < n_sq - 1:
            p = _pmm(p, p, lo, D_GI)
    while blk < GDN_CHUNK:
        wider = (ii // (2 * blk)) == (jj // (2 * blk))
        off = jnp.where(jnp.logical_and(wider, jnp.logical_not(same)), lmat, 0.0)
        x = x - _pmm(x, _pmm(off, x, lo, D_GI), lo, D_GI)
        same = wider
        blk *= 2
    return x


def _gdn_local_kernel(q_ref, k_ref, v_ref, sc_ref, sct_ref,
                      u_ref, w_ref, qh_ref, aqk_ref, ktt_ref, egl_ref):
    c = GDN_CHUNK
    cpt = TILE // c
    n_pairs = q_ref.shape[2] // LANES
    n = n_pairs * 2 * cpt
    lo = lax.broadcasted_iota(jnp.int32, (1, 1, LANES), 2) < HEAD_DIM

    def both(x):
        parts = []
        for p in range(n_pairs):
            x3 = x[:, LANES * p:LANES * (p + 1)].reshape(cpt, c, LANES)
            parts += [x3, x3]
        return jnp.concatenate(parts, axis=0)

    sc = sc_ref[0]

    def per_dir(col):
        parts = []
        for p in range(n_pairs):
            a0 = sc[2 * p].reshape(cpt, c, SUBLANES)
            a1 = sc[2 * p + 1].reshape(cpt, c, SUBLANES)
            for d in range(2):
                parts.append(jnp.where(lo, a0[:, :, col + d:col + d + 1], a1[:, :, col + d:col + d + 1]))
        return jnp.concatenate(parts, axis=0)

    q, k, v = both(q_ref[0]), both(k_ref[0]), both(v_ref[0])
    beta = per_dir(2)
    gcc = per_dir(4)
    sct = sct_ref[0]
    gcr = jnp.stack([jnp.concatenate([sct[2 * p, 4 + d:5 + d, ch * c:(ch + 1) * c],
                                      sct[2 * p + 1, 4 + d:5 + d, ch * c:(ch + 1) * c]], axis=1)
                     for p in range(n_pairs) for d in range(2) for ch in range(cpt)], axis=0)

    ii = lax.broadcasted_iota(jnp.int32, (n, c, LANES), 1)
    jj = lax.broadcasted_iota(jnp.int32, (n, c, LANES), 2) % c
    fwd = (lax.broadcasted_iota(jnp.int32, (n, c, LANES), 0) // cpt) % 2 == 0
    incl = jnp.logical_or(jnp.logical_and(fwd, ii >= jj), jnp.logical_and(jnp.logical_not(fwd), ii <= jj))
    strict = jnp.logical_and(incl, ii != jj)
    decay = jnp.where(incl, jnp.exp(jnp.where(incl, gcc - gcr, 0.0)), 0.0)
    kb = k * beta
    a = _bmm_nt(jnp.concatenate([kb, q], axis=1).astype(D_GA), _pair_blockdiag(k.astype(D_GA), lo))
    lmat = jnp.where(strict, a[:, :c] * decay, 0.0)
    aqk = a[:, c:] * decay
    tinv = _tri_inverse(lmat, ii, jj, lo)
    egc = jnp.exp(gcc)
    u = _pmm(tinv, v * beta, lo, D_GT)
    w = _pmm(tinv, kb * egc, lo, D_GT)
    fwd1 = (lax.broadcasted_iota(jnp.int32, (n, 1, 1), 0) // cpt) % 2 == 0
    glast = jnp.where(fwd1, gcc[:, c - 1:c, :], gcc[:, 0:1, :])
    ktail = k * jnp.exp(glast - gcc)
    eye = jnp.where(ii == jj, 1.0, 0.0).astype(D_GDN)
    ktt = _bmm_nt(eye, _pair_blockdiag(ktail.astype(D_GDN), lo))
    split = lambda z: z.reshape((n_pairs, 2, cpt) + z.shape[1:])
    u_ref[0] = split(u)
    w_ref[0] = split(w.astype(D_GDN))
    qh_ref[0] = split((q * egc).astype(D_GDN))
    aqk_ref[0] = split(aqk.astype(D_GDN))
    ktt_ref[0] = split(ktt.astype(D_GDN))
    egl_ref[0] = split(jnp.exp(glast))


def _gdn_local(qn, kn, vv, sc, sct):
    nb, tt, b_w = qn.shape
    nh = sc.shape[1]
    n_pairs = b_w // LANES
    nt = tt // TILE
    nc = tt // GDN_CHUNK
    cpt = TILE // GDN_CHUNK
    tok = pl.BlockSpec((1, TILE, b_w), lambda b, i: (b, i, 0))
    chunked = lambda r: pl.BlockSpec((1, n_pairs, 2, cpt, r, LANES), lambda b, i: (b, 0, 0, i, 0, 0))
    shape = lambda r, dt: jax.ShapeDtypeStruct((nb, n_pairs, 2, nc, r, LANES), dt)
    return pl.pallas_call(
        _gdn_local_kernel,
        grid=(nb, nt),
        in_specs=[tok, tok, tok,
                  pl.BlockSpec((1, nh, TILE, SUBLANES), lambda b, i: (b, 0, i, 0)),
                  pl.BlockSpec((1, nh, SUBLANES, TILE), lambda b, i: (b, 0, 0, i))],
        out_specs=(chunked(GDN_CHUNK), chunked(GDN_CHUNK), chunked(GDN_CHUNK),
                   chunked(GDN_CHUNK), chunked(HEAD_DIM), chunked(1)),
        out_shape=(
            shape(GDN_CHUNK, F32),
            shape(GDN_CHUNK, D_GDN),
            shape(GDN_CHUNK, D_GDN),
            shape(GDN_CHUNK, D_GDN),
            shape(HEAD_DIM, D_GDN),
            shape(1, F32),
        ),
        compiler_params=_cparams(("parallel", "parallel")),
        name="gdn_local",
    )(qn, kn, vv, sc, sct)


def _gdn_scan_kernel(uf, ub, wf, wb, qf, qb, af, ab, kf, kb, ef, eb, of_ref, ob_ref, s_ref, *, cpt):
    @pl.when(pl.program_id(0) == 0)
    def _():
        s_ref[...] = jnp.zeros_like(s_ref)

    nb, n_pairs = uf.shape[:2]
    n = nb * n_pairs
    lo = lax.broadcasted_iota(jnp.int32, (1, 1, LANES), 2) < HEAD_DIM
    s = s_ref[...]
    for t in range(cpt):
        tb = cpt - 1 - t
        flat = lambda z: z.reshape((n,) + z.shape[2:])
        pair = lambda f, b: jnp.concatenate([flat(f[:, :, 0, t]), flat(b[:, :, 0, tb])], axis=0)
        sd = _pair_blockdiag(s.astype(D_GS), lo)
        v_new = pair(uf, ub) - _bmm(pair(wf, wb), sd)
        vd = _pair_blockdiag(v_new.astype(D_GS), lo)
        o = _bmm(pair(qf, qb), sd) + _bmm(pair(af, ab), vd)
        s = s * pair(ef, eb) + _bmm(pair(kf, kb), vd)
        of_ref[:, :, t] = o[:n].reshape((nb, n_pairs) + o.shape[1:])
        ob_ref[:, :, tb] = o[n:].reshape((nb, n_pairs) + o.shape[1:])
    s_ref[...] = s


def _gdn_scan(u, w, qh, aqk, ktt, egl, *, n_ctx_tiles):
    nb, n_pairs, _, nc, c, _ = u.shape
    cpt = TILE // c
    nt = nc // cpt

    def bwd_tile(j):
        return jnp.where(j < n_ctx_tiles, n_ctx_tiles - 1 - j, nt - 1 + n_ctx_tiles - j)

    def specs(a):
        blk = (nb, n_pairs, 1, cpt) + a.shape[4:]
        return [pl.BlockSpec(blk, lambda j: (0, 0, 0, j, 0, 0)),
                pl.BlockSpec(blk, lambda j: (0, 0, 1, bwd_tile(j), 0, 0))]

    args, in_specs = [], []
    for a in (u, w, qh, aqk, ktt, egl):
        args += [a, a]
        in_specs += specs(a)
    oshape = jax.ShapeDtypeStruct((nb, n_pairs, nc, c, LANES), F32)
    return pl.pallas_call(
        functools.partial(_gdn_scan_kernel, cpt=cpt),
        grid=(nt,),
        in_specs=in_specs,
        out_specs=(pl.BlockSpec((nb, n_pairs, cpt, c, LANES), lambda j: (0, 0, j, 0, 0)),
                   pl.BlockSpec((nb, n_pairs, cpt, c, LANES), lambda j: (0, 0, bwd_tile(j), 0, 0))),
        out_shape=(oshape, oshape),
        scratch_shapes=[pltpu.VMEM((2 * nb * n_pairs, HEAD_DIM, LANES), F32)],
        compiler_params=_cparams(("arbitrary",)),
        name="gdn_scan",
    )(*args)


def _mix_ffn_kernel(x_ref, mod_ref, ya_ref, of_ref, ob_ref, gate_ref, gnw_ref, yc_ref, w_ref,
                    nw_ref, wi_ref, wo_ref, fnw_ref, o_ref, *, d_ff, final):
    mod = mod_ref[0]
    lo = lax.broadcasted_iota(jnp.int32, (TILE, LANES), 1) < HEAD_DIM
    pairs = []
    for p in range(of_ref.shape[1]):
        o = of_ref[0, p] + ob_ref[0, p]
        sq = o * o
        ms_lo = jnp.sum(jnp.where(lo, sq, 0.0), axis=-1, keepdims=True) * (1.0 / HEAD_DIM)
        ms_hi = jnp.sum(jnp.where(lo, 0.0, sq), axis=-1, keepdims=True) * (1.0 / HEAD_DIM)
        pairs.append(o * lax.rsqrt(jnp.where(lo, ms_lo, ms_hi) + EPS) * gnw_ref[...])
    yb = (jnp.concatenate(pairs, axis=1) * _silu(gate_ref[0])).astype(D_OUT)
    y = jnp.concatenate([ya_ref[0], yb, yc_ref[0]], axis=1)
    x = x_ref[0] + mod[2:3] * _dot(y, w_ref[0])

    h = _norm_mod(x, nw_ref[...], mod[3:4], mod[4:5]).astype(D_FFN)
    gu = _dot(h, wi_ref[0])
    a = (_silu(gu[:, :d_ff]) * gu[:, d_ff:]).astype(D_FFN)
    out = x + mod[5:6] * _dot(a, wo_ref[0])
    if final:
        out = out * lax.rsqrt(jnp.mean(out * out, axis=-1, keepdims=True) + EPS) * fnw_ref[...]
    o_ref[0] = out


def _mix_ffn(xs, mods, ya, scan_f, scan_b, gate, gnw2, yc, w, nw, wi, wo, fnw,
             *, layer, tile0, final, n_batch, n_ctx_tiles):
    nb, tt, d = xs.shape
    nt = tt // TILE - tile0
    a_w, b_w, c_w = ya.shape[2], gate.shape[2], yc.shape[2]
    n_pairs = scan_f.shape[1]
    d_ff = wo.shape[1]

    def mod_idx(b, i):
        return (jnp.where(i + tile0 < n_ctx_tiles, n_batch, b), 0, 0)

    tok = lambda wd: pl.BlockSpec((1, TILE, wd), lambda b, i: (b, i + tile0, 0))
    pairs = pl.BlockSpec((1, n_pairs, TILE, LANES), lambda b, i: (b, 0, i + tile0, 0))
    vec = lambda wd: pl.BlockSpec((1, wd), lambda b, i: (0, 0))
    weight = lambda arr: pl.BlockSpec((1,) + arr.shape[1:], lambda b, i: (layer, 0, 0), pipeline_mode=pl.Buffered(1))
    return pl.pallas_call(
        functools.partial(_mix_ffn_kernel, d_ff=d_ff, final=final),
        grid=(nb, nt),
        in_specs=[tok(d), pl.BlockSpec((1, 6, d), mod_idx), tok(a_w), pairs, pairs, tok(b_w), vec(LANES), tok(c_w),
                  weight(w), vec(d), weight(wi), weight(wo), vec(d)],
        out_specs=pl.BlockSpec((1, TILE, d), lambda b, i: (b, i, 0)),
        out_shape=jax.ShapeDtypeStruct((nb, nt * TILE, d), F32),
        compiler_params=_cparams(("parallel", "parallel")),
        name="mix_ffn",
    )(xs, mods, ya, scan_f, scan_b, gate, gnw2, yc, w, nw, wi, wo, fnw)


def _rope_tables(seq, ctx_len, n_rep):
    half = A_QK // 2
    inv_freq = 1.0 / (ROPE_THETA ** (jnp.arange(0, half, 2, dtype=F32) / half))
    t = jnp.arange(seq, dtype=jnp.int32)
    ang_r = (t // GRID_W).astype(F32)[:, None] * inv_freq
    ang_c = (t % GRID_W).astype(F32)[:, None] * inv_freq
    ang = jnp.concatenate([ang_r, ang_r, ang_c, ang_c], axis=-1)
    cos = jnp.concatenate([jnp.ones((ctx_len, A_QK), F32), jnp.cos(ang)], axis=0)
    sin = jnp.concatenate([jnp.zeros((ctx_len, A_QK), F32), jnp.sin(ang)], axis=0)
    return jnp.tile(cos, (1, n_rep)), jnp.tile(sin, (1, n_rep))


def kernel(x, c, ctx, c_ctx, w_mod, b_mod, norm1_w, norm2_w, w_in, w_out, lambda_q1, lambda_k1, lambda_q2,
           lambda_k2, diff_norm_w, conv_w, a_log, dt_bias, gdn_norm_w, na_bias, w_ffn_in, w_ffn_out,
           final_norm_w):
    nb, seq, d = x.shape
    ctx_len = ctx.shape[1]
    depth = w_mod.shape[0]
    b_heads = a_log.shape[-1]
    c_heads = na_bias.shape[1]
    b_w = b_heads * HEAD_DIM
    c_w = c_heads * HEAD_DIM
    a_w = w_out.shape[1] - b_w - c_w
    assert seq % TILE == 0 and ctx_len % TILE == 0 and seq % GRID_W == 0
    assert a_w % LANES == 0 and b_w % LANES == 0 and c_w % LANES == 0 and 4 * b_heads <= LANES
    n_ctx_tiles = ctx_len // TILE

    xs = jnp.concatenate([ctx, x], axis=1)

    n_rows = -(-(nb + 1) // SUBLANES) * SUBLANES
    cc = jnp.concatenate([c, c_ctx[None, :], jnp.zeros((n_rows - nb - 1, d), F32)], axis=0)
    mods_all = _modulation(cc, w_mod, b_mod).reshape(depth, n_rows, 6, d)

    cos, sin = _rope_tables(seq, ctx_len, a_w // A_QK)
    cost, sint = cos.T, sin.T

    sizes = (3 * a_w, 3 * b_w, b_w, 2 * b_heads, 2 * b_heads, 3 * c_w)
    offs = np.concatenate([[0], np.cumsum(sizes)])
    o_a, o_b, o_g, o_al, o_be, o_c = (int(v) for v in offs[:6])
    pad = LANES - 4 * b_heads

    wm_all = jnp.concatenate([
        w_in[:, :, o_a:o_a + a_w],
        w_in[:, :, o_a + 2 * a_w:o_a + 3 * a_w],
        w_in[:, :, o_b:o_b + 3 * b_w],
        w_in[:, :, o_g:o_g + b_w],
        w_in[:, :, o_c:o_c + c_w],
        w_in[:, :, o_c + 2 * c_w:o_c + 3 * c_w],
        w_in[:, :, o_al:o_al + 4 * b_heads],
        jnp.zeros((depth, d, pad), F32),
    ], axis=2).astype(D_IN)
    wk_all = jnp.swapaxes(jnp.concatenate([w_in[:, :, o_a + a_w:o_a + 2 * a_w],
                                           w_in[:, :, o_c + c_w:o_c + 2 * c_w]], axis=2), 1, 2).astype(D_IN)
    w_out_all = w_out.astype(D_OUT)
    w_ffn_in_all = w_ffn_in.astype(D_FFN)
    w_ffn_out_all = w_ffn_out.astype(D_FFN)
    bias_all = _na_bias_table(na_bias, seq)
    nw2_all = jnp.tile(diff_norm_w, (1, LANES // HEAD_DIM))
    gnw2_all = jnp.tile(gdn_norm_w, (1, LANES // HEAD_DIM))
    zpad = jnp.zeros((depth, LANES - 2 * b_heads), F32)
    alog_all = jnp.concatenate([a_log.reshape(depth, -1), zpad], axis=1)
    dtb_all = jnp.concatenate([dt_bias.reshape(depth, -1), zpad], axis=1)
    fnw = final_norm_w[None, :]

    for l in range(depth):
        last = l == depth - 1
        mods = mods_all[l]
        lam_init = 0.8 - 0.6 * math.exp(-0.3 * l)

        qa, kat, va, qkvb, gate, ab, qc, kct, vc = _inproj(
            xs, mods, norm1_w[l][None, :], wm_all, wk_all, cos, sin, cost, sint,
            layer=l, n_batch=nb, n_ctx_tiles=n_ctx_tiles, a_w=a_w, b_w=b_w, c_w=c_w)

        lam_vecs = jnp.stack([lambda_q1[l], lambda_k1[l], lambda_q2[l], lambda_k2[l]], axis=0).astype(F32)
        ya = _diffattn(lam_vecs, nw2_all[l][None, :], qa, kat, va, n_ctx_tiles=n_ctx_tiles, lam_init=lam_init)

        yc = _na(qc, kct, vc, bias_all, layer=l, n_ctx_tiles=n_ctx_tiles)

        qn, kn, vv, sc, sct = _gdn_prep(qkvb, ab, conv_w[l], alog_all[l][None, :], dtb_all[l][None, :],
                                        n_ctx_tiles=n_ctx_tiles, n_heads=b_heads)
        u, w, qh, aqk, ktt, egl = _gdn_local(qn, kn, vv, sc, sct)
        scan_f, scan_b = _gdn_scan(u, w, qh, aqk, ktt, egl, n_ctx_tiles=n_ctx_tiles)
        scan_f = scan_f.reshape(nb, b_w // LANES, ctx_len + seq, LANES)
        scan_b = scan_b.reshape(nb, b_w // LANES, ctx_len + seq, LANES)

        tile0 = n_ctx_tiles if last else 0
        xs = _mix_ffn(xs, mods, ya, scan_f, scan_b, gate, gnw2_all[l][None, :], yc, w_out_all,
                      norm2_w[l][None, :], w_ffn_in_all, w_ffn_out_all, fnw,
                      layer=l, tile0=tile0, final=last, n_batch=nb, n_ctx_tiles=n_ctx_tiles)

    return xs
```

```python
import functools
import math

import jax
import jax.numpy as jnp
import numpy as np
from jax import lax
from jax.experimental import pallas as pl
from jax.experimental.pallas import tpu as pltpu

F32 = jnp.float32
BF16 = jnp.bfloat16
D_MOD = D_IN = D_OUT = D_FFN = D_ATT = D_NA = D_GDN = BF16
D_GA = D_GI = D_GT = D_GS = BF16

HEAD_DIM = 64
A_QK = HEAD_DIM // 2
GRID_W = 64
ROPE_THETA = 10000.0
CONV_W = 5
GDN_CHUNK = 64
WIN_ROWS = 8
WIN_COLS = 16
EPS = 1e-6

LANES = 128
SUBLANES = 8
TILE = 256
NA_WIN_TILES = 3
MOD_COLS = 1536
TRI_BASE = 16
NEG = -1e30
VMEM_LIMIT = 56 * 1024 * 1024


def _cparams(sem):
    return pltpu.CompilerParams(dimension_semantics=sem, vmem_limit_bytes=VMEM_LIMIT)


def _silu(x):
    return x * (1.0 / (1.0 + jnp.exp(-x)))


def _dot(a, b):
    return jnp.dot(a, b, preferred_element_type=F32)


def _dot_nt(a, b):
    return lax.dot_general(a, b, (((1,), (1,)), ((), ())), preferred_element_type=F32)


def _split3(x):
    hi = x.astype(BF16)
    r1 = x - hi.astype(F32)
    mid = r1.astype(BF16)
    lo = (r1 - mid.astype(F32)).astype(BF16)
    return hi, mid, lo


def _dot_sel_right(x, sel):
    hi, mid, lo = _split3(x)
    return _dot(hi, sel) + _dot(mid, sel) + _dot(lo, sel)


def _dot_sel_left(sel, x):
    hi, mid, lo = _split3(x)
    return _dot(sel, hi) + _dot(sel, mid) + _dot(sel, lo)


def _mod_kernel(c_ref, w_ref, b_ref, o_ref):
    s = _silu(c_ref[...])
    o_ref[0] = _dot(s.astype(D_MOD), w_ref[0].astype(D_MOD)) + b_ref[0]


def _modulation(cc, w_mod, b_mod):
    depth, d, n = w_mod.shape
    rows = cc.shape[0]
    tn = MOD_COLS
    assert n % tn == 0
    return pl.pallas_call(
        _mod_kernel,
        grid=(depth, n // tn),
        in_specs=[
            pl.BlockSpec((rows, d), lambda l, j: (0, 0)),
            pl.BlockSpec((1, d, tn), lambda l, j: (l, 0, j)),
            pl.BlockSpec((1, 1, tn), lambda l, j: (l, 0, j)),
        ],
        out_specs=pl.BlockSpec((1, rows, tn), lambda l, j: (l, 0, j)),
        out_shape=jax.ShapeDtypeStruct((depth, rows, n), F32),
        compiler_params=_cparams(("parallel", "parallel")),
        name="modulation",
    )(cc, w_mod, b_mod.reshape(depth, 1, n))


def _norm_mod(x, nw, shift, scale):
    ms = jnp.mean(x * x, axis=-1, keepdims=True)
    y = x * lax.rsqrt(ms + EPS) * nw
    return y * (1.0 + scale) + shift


def _rope_lanes(x, cos, sin):
    parts = []
    for j in range(x.shape[1] // LANES):
        xj = x[:, LANES * j:LANES * (j + 1)]
        nxt = pltpu.roll(xj, LANES - 8, axis=1)
        prv = pltpu.roll(xj, 8, axis=1)
        lane = lax.broadcasted_iota(jnp.int32, xj.shape, 1)
        parts.append(jnp.where((lane % 16) < 8, -nxt, prv))
    rot = jnp.concatenate(parts, axis=1)
    return x * cos + rot * sin


def _rope_rows(x, cos, sin):
    n = x.shape[0]
    nxt = pltpu.roll(x, n - 8, axis=0)
    prv = pltpu.roll(x, 8, axis=0)
    row = lax.broadcasted_iota(jnp.int32, x.shape, 0)
    rot = jnp.where((row % 16) < 8, -nxt, prv)
    return x * cos + rot * sin


def _inproj_kernel(x_ref, mod_ref, nw_ref, wm_ref, wk_ref, cos_ref, sin_ref, cost_ref, sint_ref,
                   qa_ref, kat_ref, va_ref, qkvb_ref, gate_ref, ab_ref, qc_ref, kct_ref, vc_ref,
                   *, a_w, b_w, c_w):
    mod = mod_ref[0]
    h = _norm_mod(x_ref[0], nw_ref[...], mod[0:1], mod[1:2]).astype(D_IN)
    acc = _dot(h, wm_ref[0])
    kt = _dot_nt(wk_ref[0], h)
    o = 0
    qa = _rope_lanes(acc[:, o:o + a_w], cos_ref[...], sin_ref[...])
    qa_ref[0] = (qa * (A_QK ** -0.5 * math.log2(math.e))).astype(D_ATT)
    o += a_w
    va_ref[0] = acc[:, o:o + a_w].astype(D_ATT)
    o += a_w
    qkvb_ref[0] = acc[:, o:o + 3 * b_w]
    o += 3 * b_w
    gate_ref[0] = acc[:, o:o + b_w]
    o += b_w
    qc_ref[0] = (acc[:, o:o + c_w] * (HEAD_DIM ** -0.5 * math.log2(math.e))).astype(D_NA)
    o += c_w
    vc_ref[0] = acc[:, o:o + c_w].astype(D_NA)
    o += c_w
    ab_ref[0] = acc[:, o:o + LANES]
    kat_ref[0, 0] = _rope_rows(kt[:a_w], cost_ref[...], sint_ref[...]).astype(D_ATT)
    kct_ref[0, 0] = kt[a_w:].astype(D_NA)


def _inproj(xs, mods, nw, wm, wk, cos, sin, cost, sint, *, layer, n_batch, n_ctx_tiles, a_w, b_w, c_w):
    nb, tt, d = xs.shape
    nt = tt // TILE
    ctx_row = n_batch

    def mod_idx(b, i):
        return (jnp.where(i < n_ctx_tiles, ctx_row, b), 0, 0)

    tok = lambda w: pl.BlockSpec((1, TILE, w), lambda b, i: (b, i, 0))
    tokt = lambda w: pl.BlockSpec((1, 1, w, TILE), lambda b, i: (b, i, 0, 0))
    out_shape = (
        jax.ShapeDtypeStruct((nb, tt, a_w), D_ATT),
        jax.ShapeDtypeStruct((nb, nt, a_w, TILE), D_ATT),
        jax.ShapeDtypeStruct((nb, tt, a_w), D_ATT),
        jax.ShapeDtypeStruct((nb, tt, 3 * b_w), F32),
        jax.ShapeDtypeStruct((nb, tt, b_w), F32),
        jax.ShapeDtypeStruct((nb, tt, LANES), F32),
        jax.ShapeDtypeStruct((nb, tt, c_w), D_NA),
        jax.ShapeDtypeStruct((nb, nt, c_w, TILE), D_NA),
        jax.ShapeDtypeStruct((nb, tt, c_w), D_NA),
    )
    return pl.pallas_call(
        functools.partial(_inproj_kernel, a_w=a_w, b_w=b_w, c_w=c_w),
        grid=(nb, nt),
        in_specs=[
            tok(d),
            pl.BlockSpec((1, 6, d), mod_idx),
            pl.BlockSpec((1, d), lambda b, i: (0, 0)),
            pl.BlockSpec((1,) + wm.shape[1:], lambda b, i: (layer, 0, 0)),
            pl.BlockSpec((1,) + wk.shape[1:], lambda b, i: (layer, 0, 0)),
            pl.BlockSpec((TILE, a_w), lambda b, i: (i, 0)),
            pl.BlockSpec((TILE, a_w), lambda b, i: (i, 0)),
            pl.BlockSpec((a_w, TILE), lambda b, i: (0, i)),
            pl.BlockSpec((a_w, TILE), lambda b, i: (0, i)),
        ],
        out_specs=(tok(a_w), tokt(a_w), tok(a_w), tok(3 * b_w), tok(b_w), tok(LANES),
                   tok(c_w), tokt(c_w), tok(c_w)),
        out_shape=out_shape,
        compiler_params=_cparams(("parallel", "parallel")),
        name="inproj",
    )(xs, mods, nw, wm, wk, cos, sin, cost, sint)


def _diffattn_kernel(lam_ref, nw_ref, q_ref, kt_ref, v_ref, o_ref, mx_ref, acc_ref,
                     *, n_ctx_tiles, n_key_tiles, lam_init):
    i = pl.program_id(2)
    lv = lam_ref[...]
    lam = (jnp.exp(jnp.sum(lv[0:1] * lv[1:2], axis=-1, keepdims=True))
           - jnp.exp(jnp.sum(lv[2:3] * lv[3:4], axis=-1, keepdims=True)) + lam_init)
    q = q_ref[0]
    tq = q.shape[0]
    lane = lax.broadcasted_iota(jnp.int32, (tq, LANES), 1)
    lo = lane < HEAD_DIM
    zero = jnp.zeros_like(q)
    qm = [jnp.where((lane // A_QK) == m, q, zero) for m in range(LANES // A_QK)]
    n_maps = len(qm)
    n_lat = n_key_tiles - n_ctx_tiles
    group = max(u for u in (16, 8, 4, 2, 1) if n_lat % u == 0)
    ctx_tiles = list(range(n_ctx_tiles))
    lat_tiles = lambda g: [n_ctx_tiles + g * group + u for u in range(group)]

    lane_v = lax.broadcasted_iota(jnp.int32, (TILE, LANES), 1)

    def visit(tiles, first):
        ktiles = [kt_ref[0, j] for j in tiles]
        v_ext = []
        for j in tiles:
            start = j * TILE if isinstance(j, int) else pl.multiple_of(j * TILE, TILE)
            vtile = v_ref[0, pl.ds(start, TILE), :]
            ones = jnp.ones_like(vtile)
            v_ext.append((jnp.where(lane_v < HEAD_DIM, vtile, ones), jnp.where(lane_v < HEAD_DIM, ones, vtile)))
        scores = [[_dot(qm[m], ktile) for ktile in ktiles] for m in range(n_maps)]
        m_new = []
        for m in range(n_maps):
            top = scores[m][0]
            for s in scores[m][1:]:
                top = jnp.maximum(top, s)
            top = jnp.max(jnp.maximum(top[:, :LANES], top[:, LANES:]), axis=-1, keepdims=True)
            m_new.append(top if first else jnp.maximum(mx_ref[m], top))
        probs = [[jnp.exp2(s - m_new[m]).astype(D_ATT) for s in scores[m]] for m in range(n_maps)]
        for m in range(n_maps):
            acc = None if first else acc_ref[m] * jnp.exp2(mx_ref[m] - m_new[m])
            for p, ve in zip(probs[m], v_ext):
                pv = _dot(p, ve[m // 2])
                acc = pv if acc is None else acc + pv
            mx_ref[m] = m_new[m]
            acc_ref[m] = acc

    @pl.when(i < n_ctx_tiles)
    def _():
        visit(ctx_tiles, True)

    @pl.when(i >= n_ctx_tiles)
    def _():
        visit(ctx_tiles + lat_tiles(0), True)
        lax.fori_loop(1, n_lat // group, lambda g, c: (visit(lat_tiles(g), False), c)[1], 0)
    attn = []
    for m in range(n_maps):
        a = acc_ref[m]
        attn.append(a / pltpu.roll(a, HEAD_DIM, axis=1))
    o = jnp.where(lo, attn[0] - lam * attn[1], attn[2] - lam * attn[3])
    sq = o * o
    ms_lo = jnp.sum(jnp.where(lo, sq, 0.0), axis=-1, keepdims=True) * (1.0 / HEAD_DIM)
    ms_hi = jnp.sum(jnp.where(lo, 0.0, sq), axis=-1, keepdims=True) * (1.0 / HEAD_DIM)
    y = o * lax.rsqrt(jnp.where(lo, ms_lo, ms_hi) + EPS) * nw_ref[...]
    o_ref[0] = (y * (1.0 - lam_init)).astype(D_OUT)


def _diffattn(lam_vecs, nw2, qa, kat, va, *, n_ctx_tiles, lam_init):
    nb, tt, a_w = qa.shape
    nt = tt // TILE
    return pl.pallas_call(
        functools.partial(_diffattn_kernel, n_ctx_tiles=n_ctx_tiles, n_key_tiles=nt, lam_init=lam_init),
        grid=(nb, a_w // LANES, nt),
        in_specs=[
            pl.BlockSpec(lam_vecs.shape, lambda b, p, i: (0, 0)),
            pl.BlockSpec((1, LANES), lambda b, p, i: (0, 0)),
            pl.BlockSpec((1, TILE, LANES), lambda b, p, i: (b, i, p)),
            pl.BlockSpec((1, nt, LANES, TILE), lambda b, p, i: (b, 0, p, 0)),
            pl.BlockSpec((1, tt, LANES), lambda b, p, i: (b, 0, p)),
        ],
        out_specs=pl.BlockSpec((1, TILE, LANES), lambda b, p, i: (b, i, p)),
        out_shape=jax.ShapeDtypeStruct((nb, tt, a_w), D_OUT),
        scratch_shapes=[pltpu.VMEM((LANES // A_QK, TILE, 1), F32),
                        pltpu.VMEM((LANES // A_QK, TILE, LANES), F32)],
        compiler_params=_cparams(("parallel", "parallel", "parallel")),
        name="diffattn",
    )(lam_vecs, nw2, qa, kat, va)


def _na_kernel(q_ref, kt_ref, v_ref, bias_ref, o_ref, *, n_ctx_tiles, n_key_tiles):
    i = pl.program_id(1)
    n_lat = n_key_tiles - n_ctx_tiles
    w0 = n_ctx_tiles + jnp.clip(i - n_ctx_tiles - 1, 0, n_lat - NA_WIN_TILES)
    tq = q_ref.shape[1]
    n_pairs = q_ref.shape[2] // LANES
    lane = lax.broadcasted_iota(jnp.int32, (tq, LANES), 1)
    lo = lane < HEAD_DIM
    heads = [(p, hd) for p in range(n_pairs) for hd in range(2)]
    qh = []
    for p in range(n_pairs):
        q = q_ref[0, :, LANES * p:LANES * (p + 1)]
        zero = jnp.zeros_like(q)
        qh += [jnp.where(lo, q, zero), jnp.where(lo, zero, q)]
    tiles = [w0 + c for c in range(NA_WIN_TILES)] + list(range(n_ctx_tiles))
    scores = []
    for h, (p, hd) in enumerate(heads):
        row = []
        for c, t in enumerate(tiles):
            s = _dot(qh[h], kt_ref[0, t, LANES * p:LANES * (p + 1), :])
            if c < NA_WIN_TILES:
                s = s + bias_ref[0, h, 0, :, TILE * c:TILE * (c + 1)]
            row.append(s)
        scores.append(row)
    tops = []
    for row in scores:
        top = row[0]
        for s in row[1:]:
            top = jnp.maximum(top, s)
        tops.append(jnp.max(jnp.maximum(top[:, :LANES], top[:, LANES:]), axis=-1, keepdims=True))
    probs = [[jnp.exp2(s - top).astype(D_NA) for s in row] for row, top in zip(scores, tops)]
    lane_v = lax.broadcasted_iota(jnp.int32, (TILE, LANES), 1)
    accs = [jnp.zeros((tq, LANES), F32) for _ in heads]
    for c, t in enumerate(tiles):
        start = t * TILE if isinstance(t, int) else pl.multiple_of(t * TILE, TILE)
        for p in range(n_pairs):
            vt = v_ref[0, pl.ds(start, TILE), LANES * p:LANES * (p + 1)]
            ones = jnp.ones_like(vt)
            v_ext = (jnp.where(lane_v < HEAD_DIM, vt, ones), jnp.where(lane_v < HEAD_DIM, ones, vt))
            for hd in range(2):
                h = 2 * p + hd
                accs[h] = accs[h] + _dot(probs[h][c], v_ext[hd])
    outs = [a / pltpu.roll(a, HEAD_DIM, axis=1) for a in accs]
    o_ref[0] = jnp.concatenate([jnp.where(lo, outs[2 * p], outs[2 * p + 1]) for p in range(n_pairs)],
                               axis=1).astype(D_OUT)


def _na(qc, kct, vc, bias, *, layer, n_ctx_tiles):
    nb, tt, c_w = qc.shape
    nt = tt // TILE
    n_lat = nt - n_ctx_tiles

    def bias_idx(b, i):
        il = i - n_ctx_tiles
        cls = jnp.where(il < 0, 3, jnp.where(il == 0, 0, jnp.where(il == n_lat - 1, 2, 1)))
        return (layer, 0, cls, 0, 0)

    return pl.pallas_call(
        functools.partial(_na_kernel, n_ctx_tiles=n_ctx_tiles, n_key_tiles=nt),
        grid=(nb, nt),
        in_specs=[
            pl.BlockSpec((1, TILE, c_w), lambda b, i: (b, i, 0)),
            pl.BlockSpec((1, nt, c_w, TILE), lambda b, i: (b, 0, 0, 0)),
            pl.BlockSpec((1, tt, c_w), lambda b, i: (b, 0, 0)),
            pl.BlockSpec((1, bias.shape[1], 1, TILE, NA_WIN_TILES * TILE), bias_idx),
        ],
        out_specs=pl.BlockSpec((1, TILE, c_w), lambda b, i: (b, i, 0)),
        out_shape=jax.ShapeDtypeStruct((nb, tt, c_w), D_OUT),
        compiler_params=_cparams(("parallel", "parallel")),
        name="nbr_attn",
    )(qc, kct, vc, bias)


def _na_bias_kernel(u_ref, o_ref, *, plan):
    neg = jnp.full((GRID_W, GRID_W), NEG, F32)
    for cls, rows_plan in enumerate(plan):
        for qr, row_plan in enumerate(rows_plan):
            blocks = [neg if dr is None else u_ref[0, 0, dr] for dr in row_plan]
            o_ref[0, 0, cls, GRID_W * qr:GRID_W * (qr + 1), :] = jnp.concatenate(blocks, axis=1)
    o_ref[0, 0, len(plan)] = jnp.full(o_ref.shape[3:], NEG, F32)


def _na_bias_table(rel_bias, seq):
    rows = seq // GRID_W
    rpt = TILE // GRID_W
    wrows = NA_WIN_TILES * rpt
    assert rows >= wrows and rows >= WIN_ROWS
    n_layers, n_heads = rel_bias.shape[:2]
    n_dr, n_dc = 2 * WIN_ROWS - 1, 2 * WIN_COLS - 1
    qcol, kcol = np.arange(GRID_W)[:, None], np.arange(GRID_W)[None, :]
    kc0 = np.clip(qcol - WIN_COLS // 2, 0, GRID_W - WIN_COLS)
    col_ok = (kcol >= kc0) & (kcol < kc0 + WIN_COLS)
    dcol = kcol - qcol + (WIN_COLS - 1)
    sel_c = ((dcol[None] == np.arange(n_dc)[:, None, None]) & col_ok[None]).astype(np.float32)
    blocks = jnp.einsum("lhdc,cqk->lhdqk", rel_bias.astype(F32) * math.log2(math.e), sel_c,
                        precision=lax.Precision.HIGHEST)
    blocks = jnp.where(jnp.asarray(col_ok), blocks, NEG)
    n_lat = rows // rpt
    plan = []
    for il in (0, min(1, n_lat - 1), n_lat - 1):
        ws = int(np.clip(il - 1, 0, n_lat - NA_WIN_TILES)) * rpt
        rows_plan = []
        for qr in range(rpt):
            r_abs = il * rpt + qr
            kr0 = int(np.clip(r_abs - WIN_ROWS // 2, 0, rows - WIN_ROWS))
            rows_plan.append(tuple((ws + kr) - r_abs + (WIN_ROWS - 1) if kr0 <= ws + kr < kr0 + WIN_ROWS else None
                                   for kr in range(wrows)))
        plan.append(tuple(rows_plan))
    n_cls = len(plan) + 1
    return pl.pallas_call(
        functools.partial(_na_bias_kernel, plan=tuple(plan)),
        grid=(n_layers, n_heads),
        in_specs=[pl.BlockSpec((1, 1, n_dr, GRID_W, GRID_W), lambda l, h: (l, h, 0, 0, 0))],
        out_specs=pl.BlockSpec((1, 1, n_cls, TILE, NA_WIN_TILES * TILE), lambda l, h: (l, h, 0, 0, 0)),
        out_shape=jax.ShapeDtypeStruct((n_layers, n_heads, n_cls, TILE, NA_WIN_TILES * TILE), F32),
        compiler_params=_cparams(("parallel", "parallel")),
        name="nbr_bias_table",
    )(blocks)


def _gdn_prep_kernel(x_ref, prev_ref, next_ref, ab_ref, cw_ref, alog_ref, dtb_ref,
                     q_ref, k_ref, v_ref, sc_ref, sct_ref, *, n_ctx_tiles, n_tiles, n_heads):
    i = pl.program_id(1)
    first = jnp.logical_or(i == 0, i == n_ctx_tiles)
    last = jnp.logical_or(i == n_ctx_tiles - 1, i == n_tiles - 1)
    prev = jnp.where(first, 0.0, prev_ref[0, 0])
    nxt = jnp.where(last, 0.0, next_ref[0, 0])
    ext = jnp.concatenate([prev, x_ref[0], nxt], axis=0)
    cw = cw_ref[...]
    y = None
    for j in range(CONV_W):
        o = SUBLANES - CONV_W // 2 + j
        term = ext[o:o + TILE] * cw[j:j + 1]
        y = term if y is None else y + term
    y = _silu(y)
    b_w = n_heads * HEAD_DIM
    lo = lax.broadcasted_iota(jnp.int32, (TILE, LANES), 1) < HEAD_DIM

    def l2norm_heads(z):
        parts = []
        for p in range(z.shape[1] // LANES):
            zp = z[:, LANES * p:LANES * (p + 1)]
            sq = zp * zp
            s_lo = jnp.sum(jnp.where(lo, sq, 0.0), axis=-1, keepdims=True)
            s_hi = jnp.sum(jnp.where(lo, 0.0, sq), axis=-1, keepdims=True)
            parts.append(zp * lax.rsqrt(jnp.where(lo, s_lo, s_hi) + EPS))
        return jnp.concatenate(parts, axis=1)

    q_ref[0] = l2norm_heads(y[:, :b_w]) * (HEAD_DIM ** -0.5)
    k_ref[0] = l2norm_heads(y[:, b_w:2 * b_w])
    v_ref[0] = y[:, 2 * b_w:]

    ab = ab_ref[0]
    lane = lax.broadcasted_iota(jnp.int32, ab.shape, 1)
    z = ab + dtb_ref[...]
    softplus = jnp.maximum(z, 0.0) + jnp.log(1.0 + jnp.exp(-jnp.abs(z)))
    g = jnp.where(lane < 2 * n_heads, -jnp.exp(alog_ref[...]) * softplus, 0.0)
    beta = 1.0 / (1.0 + jnp.exp(-ab))
    r = lax.broadcasted_iota(jnp.int32, (TILE, TILE), 0)
    c = lax.broadcasted_iota(jnp.int32, (TILE, TILE), 1)
    same = (r // GDN_CHUNK) == (c // GDN_CHUNK)
    tri_f = jnp.where(jnp.logical_and(same, c <= r), 1.0, 0.0).astype(BF16)
    tri_b = jnp.where(jnp.logical_and(same, c >= r), 1.0, 0.0).astype(BF16)
    gc = jnp.where(lane < n_heads, _dot_sel_left(tri_f, g), _dot_sel_left(tri_b, g))
    rr = lax.broadcasted_iota(jnp.int32, (LANES, LANES), 0)
    cc = lax.broadcasted_iota(jnp.int32, (LANES, LANES), 1)
    hh, jj = cc // SUBLANES, cc % SUBLANES
    valid = hh < n_heads
    sel_gb = jnp.where(jnp.logical_and(valid, jnp.logical_and(jj < 4, rr == jj * n_heads + hh)), 1.0, 0.0).astype(BF16)
    sel_gc = jnp.where(jnp.logical_and(valid, jnp.logical_and(jnp.logical_and(jj >= 4, jj < 6),
                                                              rr == (jj - 4) * n_heads + hh)), 1.0, 0.0).astype(BF16)
    gb = jnp.where(lane < 2 * n_heads, g, beta)
    packed = _dot_sel_right(gb, sel_gb) + _dot_sel_right(gc, sel_gc)
    packed_t = packed.T
    for h in range(n_heads):
        sc_ref[0, h] = packed[:, SUBLANES * h:SUBLANES * (h + 1)]
        sct_ref[0, h] = packed_t[SUBLANES * h:SUBLANES * (h + 1), :]


def _gdn_prep(qkvb, ab, conv_w, alog_v, dtb_v, *, n_ctx_tiles, n_heads):
    nb, tt, w3 = qkvb.shape
    nt = tt // TILE
    rows8 = TILE // SUBLANES
    x8 = qkvb.reshape(nb, tt // SUBLANES, SUBLANES, w3)
    hd = lambda w: pl.BlockSpec((1, n_heads, TILE, w), lambda b, i: (b, 0, i, 0))
    tok = pl.BlockSpec((1, TILE, w3 // 3), lambda b, i: (b, i, 0))
    return pl.pallas_call(
        functools.partial(_gdn_prep_kernel, n_ctx_tiles=n_ctx_tiles, n_tiles=nt, n_heads=n_heads),
        grid=(nb, nt),
        in_specs=[
            pl.BlockSpec((1, TILE, w3), lambda b, i: (b, i, 0)),
            pl.BlockSpec((1, 1, SUBLANES, w3), lambda b, i: (b, jnp.maximum(i * rows8 - 1, 0), 0, 0)),
            pl.BlockSpec((1, 1, SUBLANES, w3), lambda b, i: (b, jnp.minimum((i + 1) * rows8, nt * rows8 - 1), 0, 0)),
            pl.BlockSpec((1, TILE, LANES), lambda b, i: (b, i, 0)),
            pl.BlockSpec((CONV_W, w3), lambda b, i: (0, 0)),
            pl.BlockSpec((1, LANES), lambda b, i: (0, 0)),
            pl.BlockSpec((1, LANES), lambda b, i: (0, 0)),
        ],
        out_specs=(tok, tok, tok, hd(SUBLANES),
                   pl.BlockSpec((1, n_heads, SUBLANES, TILE), lambda b, i: (b, 0, 0, i))),
        out_shape=(
            jax.ShapeDtypeStruct((nb, tt, w3 // 3), F32),
            jax.ShapeDtypeStruct((nb, tt, w3 // 3), F32),
            jax.ShapeDtypeStruct((nb, tt, w3 // 3), F32),
            jax.ShapeDtypeStruct((nb, n_heads, tt, SUBLANES), F32),
            jax.ShapeDtypeStruct((nb, n_heads, SUBLANES, tt), F32),
        ),
        compiler_params=_cparams(("parallel", "parallel")),
        name="gdn_prep",
    )(qkvb, x8, x8, ab, conv_w, alog_v, dtb_v)


def _bmm(a, b):
    return jnp.einsum("nik,nkj->nij", a, b, preferred_element_type=F32)


def _bmm_nt(a, b):
    return jnp.einsum("nik,njk->nij", a, b, preferred_element_type=F32)


def _pair_blockdiag(x, lo):
    zero = jnp.zeros_like(x)
    return jnp.concatenate([jnp.where(lo, x, zero), jnp.where(lo, zero, x)], axis=1)


def _pmm(p, x, lo, dtype):
    return _bmm(p.astype(dtype), _pair_blockdiag(x.astype(dtype), lo))


def _tri_inverse(lmat, ii, jj, lo):
    blk = TRI_BASE
    same = (ii // blk) == (jj // blk)
    dmat = jnp.where(same, lmat, 0.0)
    x = jnp.where(ii == jj, 1.0, 0.0) - dmat
    p = _pmm(dmat, dmat, lo, D_GI)
    n_sq = int(math.log2(blk)) - 1
    for step in range(n_sq):
        x = x + _pmm(p, x, lo, D_GI)
        if step < n_sq - 1:
            p = _pmm(p, p, lo, D_GI)
    while blk < GDN_CHUNK:
        wider = (ii // (2 * blk)) == (jj // (2 * blk))
        off = jnp.where(jnp.logical_and(wider, jnp.logical_not(same)), lmat, 0.0)
        x = x - _pmm(x, _pmm(off, x, lo, D_GI), lo, D_GI)
        same = wider
        blk *= 2
    return x


def _gdn_local_kernel(q_ref, k_ref, v_ref, sc_ref, sct_ref,
                      u_ref, w_ref, qh_ref, aqk_ref, ktt_ref, egl_ref):
    c = GDN_CHUNK
    cpt = TILE // c
    n_pairs = q_ref.shape[2] // LANES
    n = n_pairs * 2 * cpt
    lo = lax.broadcasted_iota(jnp.int32, (1, 1, LANES), 2) < HEAD_DIM

    def both(x):
        parts = []
        for p in range(n_pairs):
            x3 = x[:, LANES * p:LANES * (p + 1)].reshape(cpt, c, LANES)
            parts += [x3, x3]
        return jnp.concatenate(parts, axis=0)

    sc = sc_ref[0]

    def per_dir(col):
        parts = []
        for p in range(n_pairs):
            a0 = sc[2 * p].reshape(cpt, c, SUBLANES)
            a1 = sc[2 * p + 1].reshape(cpt, c, SUBLANES)
            for d in range(2):
                parts.append(jnp.where(lo, a0[:, :, col + d:col + d + 1], a1[:, :, col + d:col + d + 1]))
        return jnp.concatenate(parts, axis=0)

    q, k, v = both(q_ref[0]), both(k_ref[0]), both(v_ref[0])
    beta = per_dir(2)
    gcc = per_dir(4)
    sct = sct_ref[0]
    gcr = jnp.stack([jnp.concatenate([sct[2 * p, 4 + d:5 + d, ch * c:(ch + 1) * c],
                                      sct[2 * p + 1, 4 + d:5 + d, ch * c:(ch + 1) * c]], axis=1)
                     for p in range(n_pairs) for d in range(2) for ch in range(cpt)], axis=0)

    ii = lax.broadcasted_iota(jnp.int32, (n, c, LANES), 1)
    jj = lax.broadcasted_iota(jnp.int32, (n, c, LANES), 2) % c
    fwd = (lax.broadcasted_iota(jnp.int32, (n, c, LANES), 0) // cpt) % 2 == 0
    incl = jnp.logical_or(jnp.logical_and(fwd, ii >= jj), jnp.logical_and(jnp.logical_not(fwd), ii <= jj))
    strict = jnp.logical_and(incl, ii != jj)
    decay = jnp.where(incl, jnp.exp(jnp.where(incl, gcc - gcr, 0.0)), 0.0)
    kb = k * beta
    a = _bmm_nt(jnp.concatenate([kb, q], axis=1).astype(D_GA), _pair_blockdiag(k.astype(D_GA), lo))
    lmat = jnp.where(strict, a[:, :c] * decay, 0.0)
    aqk = a[:, c:] * decay
    tinv = _tri_inverse(lmat, ii, jj, lo)
    egc = jnp.exp(gcc)
    u = _pmm(tinv, v * beta, lo, D_GT)
    w = _pmm(tinv, kb * egc, lo, D_GT)
    fwd1 = (lax.broadcasted_iota(jnp.int32, (n, 1, 1), 0) // cpt) % 2 == 0
    glast = jnp.where(fwd1, gcc[:, c - 1:c, :], gcc[:, 0:1, :])
    ktail = k * jnp.exp(glast - gcc)
    eye = jnp.where(ii == jj, 1.0, 0.0).astype(D_GDN)
    ktt = _bmm_nt(eye, _pair_blockdiag(ktail.astype(D_GDN), lo))
    split = lambda z: z.reshape((n_pairs, 2, cpt) + z.shape[1:])
    u_ref[0] = split(u)
    w_ref[0] = split(w.astype(D_GDN))
    qh_ref[0] = split((q * egc).astype(D_GDN))
    aqk_ref[0] = split(aqk.astype(D_GDN))
    ktt_ref[0] = split(ktt.astype(D_GDN))
    egl_ref[0] = split(jnp.exp(glast))


def _gdn_local(qn, kn, vv, sc, sct):
    nb, tt, b_w = qn.shape
    nh = sc.shape[1]
    n_pairs = b_w // LANES
    nt = tt // TILE
    nc = tt // GDN_CHUNK
    cpt = TILE // GDN_CHUNK
    tok = pl.BlockSpec((1, TILE, b_w), lambda b, i: (b, i, 0))
    chunked = lambda r: pl.BlockSpec((1, n_pairs, 2, cpt, r, LANES), lambda b, i: (b, 0, 0, i, 0, 0))
    shape = lambda r, dt: jax.ShapeDtypeStruct((nb, n_pairs, 2, nc, r, LANES), dt)
    return pl.pallas_call(
        _gdn_local_kernel,
        grid=(nb, nt),
        in_specs=[tok, tok, tok,
                  pl.BlockSpec((1, nh, TILE, SUBLANES), lambda b, i: (b, 0, i, 0)),
                  pl.BlockSpec((1, nh, SUBLANES, TILE), lambda b, i: (b, 0, 0, i))],
        out_specs=(chunked(GDN_CHUNK), chunked(GDN_CHUNK), chunked(GDN_CHUNK),
                   chunked(GDN_CHUNK), chunked(HEAD_DIM), chunked(1)),
        out_shape=(
            shape(GDN_CHUNK, F32),
            shape(GDN_CHUNK, D_GDN),
            shape(GDN_CHUNK, D_GDN),
            shape(GDN_CHUNK, D_GDN),
            shape(HEAD_DIM, D_GDN),
            shape(1, F32),
        ),
        compiler_params=_cparams(("parallel", "parallel")),
        name="gdn_local",
    )(qn, kn, vv, sc, sct)


def _gdn_scan_kernel(uf, ub, wf, wb, qf, qb, af, ab, kf, kb, ef, eb, of_ref, ob_ref, s_ref, *, cpt):
    @pl.when(pl.program_id(0) == 0)
    def _():
        s_ref[...] = jnp.zeros_like(s_ref)

    nb, n_pairs = uf.shape[:2]
    n = nb * n_pairs
    lo = lax.broadcasted_iota(jnp.int32, (1, 1, LANES), 2) < HEAD_DIM
    s = s_ref[...]
    for t in range(cpt):
        tb = cpt - 1 - t
        flat = lambda z: z.reshape((n,) + z.shape[2:])
        pair = lambda f, b: jnp.concatenate([flat(f[:, :, 0, t]), flat(b[:, :, 0, tb])], axis=0)
        sd = _pair_blockdiag(s.astype(D_GS), lo)
        v_new = pair(uf, ub) - _bmm(pair(wf, wb), sd)
        vd = _pair_blockdiag(v_new.astype(D_GS), lo)
        o = _bmm(pair(qf, qb), sd) + _bmm(pair(af, ab), vd)
        s = s * pair(ef, eb) + _bmm(pair(kf, kb), vd)
        of_ref[:, :, t] = o[:n].reshape((nb, n_pairs) + o.shape[1:])
        ob_ref[:, :, tb] = o[n:].reshape((nb, n_pairs) + o.shape[1:])
    s_ref[...] = s


def _gdn_scan(u, w, qh, aqk, ktt, egl, *, n_ctx_tiles):
    nb, n_pairs, _, nc, c, _ = u.shape
    cpt = TILE // c
    nt = nc // cpt

    def bwd_tile(j):
        return jnp.where(j < n_ctx_tiles, n_ctx_tiles - 1 - j, nt - 1 + n_ctx_tiles - j)

    def specs(a):
        blk = (nb, n_pairs, 1, cpt) + a.shape[4:]
        return [pl.BlockSpec(blk, lambda j: (0, 0, 0, j, 0, 0)),
                pl.BlockSpec(blk, lambda j: (0, 0, 1, bwd_tile(j), 0, 0))]

    args, in_specs = [], []
    for a in (u, w, qh, aqk, ktt, egl):
        args += [a, a]
        in_specs += specs(a)
    oshape = jax.ShapeDtypeStruct((nb, n_pairs, nc, c, LANES), F32)
    return pl.pallas_call(
        functools.partial(_gdn_scan_kernel, cpt=cpt),
        grid=(nt,),
        in_specs=in_specs,
        out_specs=(pl.BlockSpec((nb, n_pairs, cpt, c, LANES), lambda j: (0, 0, j, 0, 0)),
                   pl.BlockSpec((nb, n_pairs, cpt, c, LANES), lambda j: (0, 0, bwd_tile(j), 0, 0))),
        out_shape=(oshape, oshape),
        scratch_shapes=[pltpu.VMEM((2 * nb * n_pairs, HEAD_DIM, LANES), F32)],
        compiler_params=_cparams(("arbitrary",)),
        name="gdn_scan",
    )(*args)


def _mix_ffn_kernel(x_ref, mod_ref, ya_ref, of_ref, ob_ref, gate_ref, gnw_ref, yc_ref, w_ref,
                    nw_ref, wi_ref, wo_ref, fnw_ref, o_ref, *, d_ff, final):
    mod = mod_ref[0]
    lo = lax.broadcasted_iota(jnp.int32, (TILE, LANES), 1) < HEAD_DIM
    pairs = []
    for p in range(of_ref.shape[1]):
        o = of_ref[0, p] + ob_ref[0, p]
        sq = o * o
        ms_lo = jnp.sum(jnp.where(lo, sq, 0.0), axis=-1, keepdims=True) * (1.0 / HEAD_DIM)
        ms_hi = jnp.sum(jnp.where(lo, 0.0, sq), axis=-1, keepdims=True) * (1.0 / HEAD_DIM)
        pairs.append(o * lax.rsqrt(jnp.where(lo, ms_lo, ms_hi) + EPS) * gnw_ref[...])
    yb = (jnp.concatenate(pairs, axis=1) * _silu(gate_ref[0])).astype(D_OUT)
    y = jnp.concatenate([ya_ref[0], yb, yc_ref[0]], axis=1)
    x = x_ref[0] + mod[2:3] * _dot(y, w_ref[0])

    h = _norm_mod(x, nw_ref[...], mod[3:4], mod[4:5]).astype(D_FFN)
    gu = _dot(h, wi_ref[0])
    a = (_silu(gu[:, :d_ff]) * gu[:, d_ff:]).astype(D_FFN)
    out = x + mod[5:6] * _dot(a, wo_ref[0])
    if final:
        out = out * lax.rsqrt(jnp.mean(out * out, axis=-1, keepdims=True) + EPS) * fnw_ref[...]
    o_ref[0] = out


def _mix_ffn(xs, mods, ya, scan_f, scan_b, gate, gnw2, yc, w, nw, wi, wo, fnw,
             *, layer, tile0, final, n_batch, n_ctx_tiles):
    nb, tt, d = xs.shape
    nt = tt // TILE - tile0
    a_w, b_w, c_w = ya.shape[2], gate.shape[2], yc.shape[2]
    n_pairs = scan_f.shape[1]
    d_ff = wo.shape[1]

    def mod_idx(b, i):
        return (jnp.where(i + tile0 < n_ctx_tiles, n_batch, b), 0, 0)

    tok = lambda wd: pl.BlockSpec((1, TILE, wd), lambda b, i: (b, i + tile0, 0))
    pairs = pl.BlockSpec((1, n_pairs, TILE, LANES), lambda b, i: (b, 0, i + tile0, 0))
    vec = lambda wd: pl.BlockSpec((1, wd), lambda b, i: (0, 0))
    weight = lambda arr: pl.BlockSpec((1,) + arr.shape[1:], lambda b, i: (layer, 0, 0), pipeline_mode=pl.Buffered(1))
    return pl.pallas_call(
        functools.partial(_mix_ffn_kernel, d_ff=d_ff, final=final),
        grid=(nb, nt),
        in_specs=[tok(d), pl.BlockSpec((1, 6, d), mod_idx), tok(a_w), pairs, pairs, tok(b_w), vec(LANES), tok(c_w),
                  weight(w), vec(d), weight(wi), weight(wo), vec(d)],
        out_specs=pl.BlockSpec((1, TILE, d), lambda b, i: (b, i, 0)),
        out_shape=jax.ShapeDtypeStruct((nb, nt * TILE, d), F32),
        compiler_params=_cparams(("parallel", "parallel")),
        name="mix_ffn",
    )(xs, mods, ya, scan_f, scan_b, gate, gnw2, yc, w, nw, wi, wo, fnw)


def _rope_tables(seq, ctx_len, n_rep):
    half = A_QK // 2
    inv_freq = 1.0 / (ROPE_THETA ** (jnp.arange(0, half, 2, dtype=F32) / half))
    t = jnp.arange(seq, dtype=jnp.int32)
    ang_r = (t // GRID_W).astype(F32)[:, None] * inv_freq
    ang_c = (t % GRID_W).astype(F32)[:, None] * inv_freq
    ang = jnp.concatenate([ang_r, ang_r, ang_c, ang_c], axis=-1)
    cos = jnp.concatenate([jnp.ones((ctx_len, A_QK), F32), jnp.cos(ang)], axis=0)
    sin = jnp.concatenate([jnp.zeros((ctx_len, A_QK), F32), jnp.sin(ang)], axis=0)
    return jnp.tile(cos, (1, n_rep)), jnp.tile(sin, (1, n_rep))


def kernel(x, c, ctx, c_ctx, w_mod, b_mod, norm1_w, norm2_w, w_in, w_out, lambda_q1, lambda_k1, lambda_q2,
           lambda_k2, diff_norm_w, conv_w, a_log, dt_bias, gdn_norm_w, na_bias, w_ffn_in, w_ffn_out,
           final_norm_w):
    nb, seq, d = x.shape
    ctx_len = ctx.shape[1]
    depth = w_mod.shape[0]
    b_heads = a_log.shape[-1]
    c_heads = na_bias.shape[1]
    b_w = b_heads * HEAD_DIM
    c_w = c_heads * HEAD_DIM
    a_w = w_out.shape[1] - b_w - c_w
    assert seq % TILE == 0 and ctx_len % TILE == 0 and seq % GRID_W == 0
    assert a_w % LANES == 0 and b_w % LANES == 0 and c_w % LANES == 0 and 4 * b_heads <= LANES
    n_ctx_tiles = ctx_len // TILE

    xs = jnp.concatenate([ctx, x], axis=1)

    n_rows = -(-(nb + 1) // SUBLANES) * SUBLANES
    cc = jnp.concatenate([c, c_ctx[None, :], jnp.zeros((n_rows - nb - 1, d), F32)], axis=0)
    mods_all = _modulation(cc, w_mod, b_mod).reshape(depth, n_rows, 6, d)

    cos, sin = _rope_tables(seq, ctx_len, a_w // A_QK)
    cost, sint = cos.T, sin.T

    sizes = (3 * a_w, 3 * b_w, b_w, 2 * b_heads, 2 * b_heads, 3 * c_w)
    offs = np.concatenate([[0], np.cumsum(sizes)])
    o_a, o_b, o_g, o_al, o_be, o_c = (int(v) for v in offs[:6])
    pad = LANES - 4 * b_heads

    wm_all = jnp.concatenate([
        w_in[:, :, o_a:o_a + a_w],
        w_in[:, :, o_a + 2 * a_w:o_a + 3 * a_w],
        w_in[:, :, o_b:o_b + 3 * b_w],
        w_in[:, :, o_g:o_g + b_w],
        w_in[:, :, o_c:o_c + c_w],
        w_in[:, :, o_c + 2 * c_w:o_c + 3 * c_w],
        w_in[:, :, o_al:o_al + 4 * b_heads],
        jnp.zeros((depth, d, pad), F32),
    ], axis=2).astype(D_IN)
    wk_all = jnp.swapaxes(jnp.concatenate([w_in[:, :, o_a + a_w:o_a + 2 * a_w],
                                           w_in[:, :, o_c + c_w:o_c + 2 * c_w]], axis=2), 1, 2).astype(D_IN)
    w_out_all = w_out.astype(D_OUT)
    w_ffn_in_all = w_ffn_in.astype(D_FFN)
    w_ffn_out_all = w_ffn_out.astype(D_FFN)
    bias_all = _na_bias_table(na_bias, seq)
    nw2_all = jnp.tile(diff_norm_w, (1, LANES // HEAD_DIM))
    gnw2_all = jnp.tile(gdn_norm_w, (1, LANES // HEAD_DIM))
    zpad = jnp.zeros((depth, LANES - 2 * b_heads), F32)
    alog_all = jnp.concatenate([a_log.reshape(depth, -1), zpad], axis=1)
    dtb_all = jnp.concatenate([dt_bias.reshape(depth, -1), zpad], axis=1)
    fnw = final_norm_w[None, :]

    for l in range(depth):
        last = l == depth - 1
        mods = mods_all[l]
        lam_init = 0.8 - 0.6 * math.exp(-0.3 * l)

        qa, kat, va, qkvb, gate, ab, qc, kct, vc = _inproj(
            xs, mods, norm1_w[l][None, :], wm_all, wk_all, cos, sin, cost, sint,
            layer=l, n_batch=nb, n_ctx_tiles=n_ctx_tiles, a_w=a_w, b_w=b_w, c_w=c_w)

        lam_vecs = jnp.stack([lambda_q1[l], lambda_k1[l], lambda_q2[l], lambda_k2[l]], axis=0).astype(F32)
        ya = _diffattn(lam_vecs, nw2_all[l][None, :], qa, kat, va, n_ctx_tiles=n_ctx_tiles, lam_init=lam_init)

        yc = _na(qc, kct, vc, bias_all, layer=l, n_ctx_tiles=n_ctx_tiles)

        qn, kn, vv, sc, sct = _gdn_prep(qkvb, ab, conv_w[l], alog_all[l][None, :], dtb_all[l][None, :],
                                        n_ctx_tiles=n_ctx_tiles, n_heads=b_heads)
        u, w, qh, aqk, ktt, egl = _gdn_local(qn, kn, vv, sc, sct)
        scan_f, scan_b = _gdn_scan(u, w, qh, aqk, ktt, egl, n_ctx_tiles=n_ctx_tiles)
        scan_f = scan_f.reshape(nb, b_w // LANES, ctx_len + seq, LANES)
        scan_b = scan_b.reshape(nb, b_w // LANES, ctx_len + seq, LANES)

        tile0 = n_ctx_tiles if last else 0
        xs = _mix_ffn(xs, mods, ya, scan_f, scan_b, gate, gnw2_all[l][None, :], yc, w_out_all,
                      norm2_w[l][None, :], w_ffn_in_all, w_ffn_out_all, fnw,
                      layer=l, tile0=tile0, final=last, n_batch=nb, n_ctx_tiles=n_ctx_tiles)

    return xs
```

```python
import functools
import math

import jax
import jax.numpy as jnp
import numpy as np
from jax import lax
from jax.experimental import pallas as pl
from jax.experimental.pallas import tpu as pltpu

F32 = jnp.float32
BF16 = jnp.bfloat16
D_MOD = D_IN = D_OUT = D_FFN = D_ATT = D_NA = D_GDN = BF16
D_GA = D_GI = D_GT = D_GS = BF16

HEAD_DIM = 64
A_QK = HEAD_DIM // 2
GRID_W = 64
ROPE_THETA = 10000.0
CONV_W = 5
GDN_CHUNK = 64
WIN_ROWS = 8
WIN_COLS = 16
EPS = 1e-6

LANES = 128
SUBLANES = 8
TILE = 256
NA_WIN_TILES = 3
MOD_COLS = 1536
TRI_BASE = 16
NEG = -1e30
VMEM_LIMIT = 56 * 1024 * 1024


def _cparams(sem):
    return pltpu.CompilerParams(dimension_semantics=sem, vmem_limit_bytes=VMEM_LIMIT)


def _silu(x):
    return x * (1.0 / (1.0 + jnp.exp(-x)))


def _dot(a, b):
    return jnp.dot(a, b, preferred_element_type=F32)


def _dot_nt(a, b):
    return lax.dot_general(a, b, (((1,), (1,)), ((), ())), preferred_element_type=F32)


def _split3(x):
    hi = x.astype(BF16)
    r1 = x - hi.astype(F32)
    mid = r1.astype(BF16)
    lo = (r1 - mid.astype(F32)).astype(BF16)
    return hi, mid, lo


def _dot_sel_right(x, sel):
    hi, mid, lo = _split3(x)
    return _dot(hi, sel) + _dot(mid, sel) + _dot(lo, sel)


def _dot_sel_left(sel, x):
    hi, mid, lo = _split3(x)
    return _dot(sel, hi) + _dot(sel, mid) + _dot(sel, lo)


def _mod_kernel(c_ref, w_ref, b_ref, o_ref):
    s = _silu(c_ref[...])
    o_ref[0] = _dot(s.astype(D_MOD), w_ref[0].astype(D_MOD)) + b_ref[0]


def _modulation(cc, w_mod, b_mod):
    depth, d, n = w_mod.shape
    rows = cc.shape[0]
    tn = MOD_COLS
    assert n % tn == 0
    return pl.pallas_call(
        _mod_kernel,
        grid=(depth, n // tn),
        in_specs=[
            pl.BlockSpec((rows, d), lambda l, j: (0, 0)),
            pl.BlockSpec((1, d, tn), lambda l, j: (l, 0, j)),
            pl.BlockSpec((1, 1, tn), lambda l, j: (l, 0, j)),
        ],
        out_specs=pl.BlockSpec((1, rows, tn), lambda l, j: (l, 0, j)),
        out_shape=jax.ShapeDtypeStruct((depth, rows, n), F32),
        compiler_params=_cparams(("parallel", "parallel")),
        name="modulation",
    )(cc, w_mod, b_mod.reshape(depth, 1, n))


def _norm_mod(x, nw, shift, scale):
    ms = jnp.mean(x * x, axis=-1, keepdims=True)
    y = x * lax.rsqrt(ms + EPS) * nw
    return y * (1.0 + scale) + shift


def _rope_lanes(x, cos, sin):
    parts = []
    for j in range(x.shape[1] // LANES):
        xj = x[:, LANES * j:LANES * (j + 1)]
        nxt = pltpu.roll(xj, LANES - 8, axis=1)
        prv = pltpu.roll(xj, 8, axis=1)
        lane = lax.broadcasted_iota(jnp.int32, xj.shape, 1)
        parts.append(jnp.where((lane % 16) < 8, -nxt, prv))
    rot = jnp.concatenate(parts, axis=1)
    return x * cos + rot * sin


def _rope_rows(x, cos, sin):
    n = x.shape[0]
    nxt = pltpu.roll(x, n - 8, axis=0)
    prv = pltpu.roll(x, 8, axis=0)
    row = lax.broadcasted_iota(jnp.int32, x.shape, 0)
    rot = jnp.where((row % 16) < 8, -nxt, prv)
    return x * cos + rot * sin


def _inproj_kernel(x_ref, mod_ref, nw_ref, wm_ref, wk_ref, cos_ref, sin_ref, cost_ref, sint_ref,
                   qa_ref, kat_ref, va_ref, qkvb_ref, gate_ref, ab_ref, qc_ref, kct_ref, vc_ref,
                   *, a_w, b_w, c_w):
    mod = mod_ref[0]
    h = _norm_mod(x_ref[0], nw_ref[...], mod[0:1], mod[1:2]).astype(D_IN)
    acc = _dot(h, wm_ref[0])
    kt = _dot_nt(wk_ref[0], h)
    o = 0
    qa = _rope_lanes(acc[:, o:o + a_w], cos_ref[...], sin_ref[...])
    qa_ref[0] = (qa * (A_QK ** -0.5 * math.log2(math.e))).astype(D_ATT)
    o += a_w
    va_ref[0] = acc[:, o:o + a_w].astype(D_ATT)
    o += a_w
    qkvb_ref[0] = acc[:, o:o + 3 * b_w]
    o += 3 * b_w
    gate_ref[0] = acc[:, o:o + b_w]
    o += b_w
    qc_ref[0] = (acc[:, o:o + c_w] * (HEAD_DIM ** -0.5 * math.log2(math.e))).astype(D_NA)
    o += c_w
    vc_ref[0] = acc[:, o:o + c_w].astype(D_NA)
    o += c_w
    ab_ref[0] = acc[:, o:o + LANES]
    kat_ref[0, 0] = _rope_rows(kt[:a_w], cost_ref[...], sint_ref[...]).astype(D_ATT)
    kct_ref[0, 0] = kt[a_w:].astype(D_NA)


def _inproj(xs, mods, nw, wm, wk, cos, sin, cost, sint, *, layer, n_batch, n_ctx_tiles, a_w, b_w, c_w):
    nb, tt, d = xs.shape
    nt = tt // TILE
    ctx_row = n_batch

    def mod_idx(b, i):
        return (jnp.where(i < n_ctx_tiles, ctx_row, b), 0, 0)

    tok = lambda w: pl.BlockSpec((1, TILE, w), lambda b, i: (b, i, 0))
    tokt = lambda w: pl.BlockSpec((1, 1, w, TILE), lambda b, i: (b, i, 0, 0))
    out_shape = (
        jax.ShapeDtypeStruct((nb, tt, a_w), D_ATT),
        jax.ShapeDtypeStruct((nb, nt, a_w, TILE), D_ATT),
        jax.ShapeDtypeStruct((nb, tt, a_w), D_ATT),
        jax.ShapeDtypeStruct((nb, tt, 3 * b_w), F32),
        jax.ShapeDtypeStruct((nb, tt, b_w), F32),
        jax.ShapeDtypeStruct((nb, tt, LANES), F32),
        jax.ShapeDtypeStruct((nb, tt, c_w), D_NA),
        jax.ShapeDtypeStruct((nb, nt, c_w, TILE), D_NA),
        jax.ShapeDtypeStruct((nb, tt, c_w), D_NA),
    )
    return pl.pallas_call(
        functools.partial(_inproj_kernel, a_w=a_w, b_w=b_w, c_w=c_w),
        grid=(nb, nt),
        in_specs=[
            tok(d),
            pl.BlockSpec((1, 6, d), mod_idx),
            pl.BlockSpec((1, d), lambda b, i: (0, 0)),
            pl.BlockSpec((1,) + wm.shape[1:], lambda b, i: (layer, 0, 0)),
            pl.BlockSpec((1,) + wk.shape[1:], lambda b, i: (layer, 0, 0)),
            pl.BlockSpec((TILE, a_w), lambda b, i: (i, 0)),
            pl.BlockSpec((TILE, a_w), lambda b, i: (i, 0)),
            pl.BlockSpec((a_w, TILE), lambda b, i: (0, i)),
            pl.BlockSpec((a_w, TILE), lambda b, i: (0, i)),
        ],
        out_specs=(tok(a_w), tokt(a_w), tok(a_w), tok(3 * b_w), tok(b_w), tok(LANES),
                   tok(c_w), tokt(c_w), tok(c_w)),
        out_shape=out_shape,
        compiler_params=_cparams(("parallel", "parallel")),
        name="inproj",
    )(xs, mods, nw, wm, wk, cos, sin, cost, sint)


def _diffattn_kernel(lam_ref, nw_ref, q_ref, kt_ref, v_ref, o_ref, mx_ref, acc_ref,
                     *, n_ctx_tiles, n_key_tiles, lam_init):
    i = pl.program_id(2)
    lv = lam_ref[...]
    lam = (jnp.exp(jnp.sum(lv[0:1] * lv[1:2], axis=-1, keepdims=True))
           - jnp.exp(jnp.sum(lv[2:3] * lv[3:4], axis=-1, keepdims=True)) + lam_init)
    q = q_ref[0]
    tq = q.shape[0]
    lane = lax.broadcasted_iota(jnp.int32, (tq, LANES), 1)
    lo = lane < HEAD_DIM
    zero = jnp.zeros_like(q)
    qm = [jnp.where((lane // A_QK) == m, q, zero) for m in range(LANES // A_QK)]
    n_maps = len(qm)
    n_lat = n_key_tiles - n_ctx_tiles
    group = max(u for u in (16, 8, 4, 2, 1) if n_lat % u == 0)
    ctx_tiles = list(range(n_ctx_tiles))
    lat_tiles = lambda g: [n_ctx_tiles + g * group + u for u in range(group)]

    lane_v = lax.broadcasted_iota(jnp.int32, (TILE, LANES), 1)

    def visit(tiles, first):
        ktiles = [kt_ref[0, j] for j in tiles]
        v_ext = []
        for j in tiles:
            start = j * TILE if isinstance(j, int) else pl.multiple_of(j * TILE, TILE)
            vtile = v_ref[0, pl.ds(start, TILE), :]
            ones = jnp.ones_like(vtile)
            v_ext.append((jnp.where(lane_v < HEAD_DIM, vtile, ones), jnp.where(lane_v < HEAD_DIM, ones, vtile)))
        scores = [[_dot(qm[m], ktile) for ktile in ktiles] for m in range(n_maps)]
        m_new = []
        for m in range(n_maps):
            top = scores[m][0]
            for s in scores[m][1:]:
                top = jnp.maximum(top, s)
            top = jnp.max(jnp.maximum(top[:, :LANES], top[:, LANES:]), axis=-1, keepdims=True)
            m_new.append(top if first else jnp.maximum(mx_ref[m], top))
        probs = [[jnp.exp2(s - m_new[m]).astype(D_ATT) for s in scores[m]] for m in range(n_maps)]
        for m in range(n_maps):
            acc = None if first else acc_ref[m] * jnp.exp2(mx_ref[m] - m_new[m])
            for p, ve in zip(probs[m], v_ext):
                pv = _dot(p, ve[m // 2])
                acc = pv if acc is None else acc + pv
            mx_ref[m] = m_new[m]
            acc_ref[m] = acc

    @pl.when(i < n_ctx_tiles)
    def _():
        visit(ctx_tiles, True)

    @pl.when(i >= n_ctx_tiles)
    def _():
        visit(ctx_tiles + lat_tiles(0), True)
        lax.fori_loop(1, n_lat // group, lambda g, c: (visit(lat_tiles(g), False), c)[1], 0)
    attn = []
    for m in range(n_maps):
        a = acc_ref[m]
        attn.append(a / pltpu.roll(a, HEAD_DIM, axis=1))
    o = jnp.where(lo, attn[0] - lam * attn[1], attn[2] - lam * attn[3])
    sq = o * o
    ms_lo = jnp.sum(jnp.where(lo, sq, 0.0), axis=-1, keepdims=True) * (1.0 / HEAD_DIM)
    ms_hi = jnp.sum(jnp.where(lo, 0.0, sq), axis=-1, keepdims=True) * (1.0 / HEAD_DIM)
    y = o * lax.rsqrt(jnp.where(lo, ms_lo, ms_hi) + EPS) * nw_ref[...]
    o_ref[0] = (y * (1.0 - lam_init)).astype(D_OUT)


def _diffattn(lam_vecs, nw2, qa, kat, va, *, n_ctx_tiles, lam_init):
    nb, tt, a_w = qa.shape
    nt = tt // TILE
    return pl.pallas_call(
        functools.partial(_diffattn_kernel, n_ctx_tiles=n_ctx_tiles, n_key_tiles=nt, lam_init=lam_init),
        grid=(nb, a_w // LANES, nt),
        in_specs=[
            pl.BlockSpec(lam_vecs.shape, lambda b, p, i: (0, 0)),
            pl.BlockSpec((1, LANES), lambda b, p, i: (0, 0)),
            pl.BlockSpec((1, TILE, LANES), lambda b, p, i: (b, i, p)),
            pl.BlockSpec((1, nt, LANES, TILE), lambda b, p, i: (b, 0, p, 0)),
            pl.BlockSpec((1, tt, LANES), lambda b, p, i: (b, 0, p)),
        ],
        out_specs=pl.BlockSpec((1, TILE, LANES), lambda b, p, i: (b, i, p)),
        out_shape=jax.ShapeDtypeStruct((nb, tt, a_w), D_OUT),
        scratch_shapes=[pltpu.VMEM((LANES // A_QK, TILE, 1), F32),
                        pltpu.VMEM((LANES // A_QK, TILE, LANES), F32)],
        compiler_params=_cparams(("parallel", "parallel", "parallel")),
        name="diffattn",
    )(lam_vecs, nw2, qa, kat, va)


def _na_kernel(q_ref, kt_ref, v_ref, bias_ref, o_ref, *, n_ctx_tiles, n_key_tiles):
    i = pl.program_id(1)
    n_lat = n_key_tiles - n_ctx_tiles
    w0 = n_ctx_tiles + jnp.clip(i - n_ctx_tiles - 1, 0, n_lat - NA_WIN_TILES)
    tq = q_ref.shape[1]
    n_pairs = q_ref.shape[2] // LANES
    lane = lax.broadcasted_iota(jnp.int32, (tq, LANES), 1)
    lo = lane < HEAD_DIM
    heads = [(p, hd) for p in range(n_pairs) for hd in range(2)]
    qh = []
    for p in range(n_pairs):
        q = q_ref[0, :, LANES * p:LANES * (p + 1)]
        zero = jnp.zeros_like(q)
        qh += [jnp.where(lo, q, zero), jnp.where(lo, zero, q)]
    tiles = [w0 + c for c in range(NA_WIN_TILES)] + list(range(n_ctx_tiles))
    scores = []
    for h, (p, hd) in enumerate(heads):
        row = []
        for c, t in enumerate(tiles):
            s = _dot(qh[h], kt_ref[0, t, LANES * p:LANES * (p + 1), :])
            if c < NA_WIN_TILES:
                s = s + bias_ref[0, h, 0, :, TILE * c:TILE * (c + 1)]
            row.append(s)
        scores.append(row)
    tops = []
    for row in scores:
        top = row[0]
        for s in row[1:]:
            top = jnp.maximum(top, s)
        tops.append(jnp.max(jnp.maximum(top[:, :LANES], top[:, LANES:]), axis=-1, keepdims=True))
    probs = [[jnp.exp2(s - top).astype(D_NA) for s in row] for row, top in zip(scores, tops)]
    lane_v = lax.broadcasted_iota(jnp.int32, (TILE, LANES), 1)
    accs = [jnp.zeros((tq, LANES), F32) for _ in heads]
    for c, t in enumerate(tiles):
        start = t * TILE if isinstance(t, int) else pl.multiple_of(t * TILE, TILE)
        for p in range(n_pairs):
            vt = v_ref[0, pl.ds(start, TILE), LANES * p:LANES * (p + 1)]
            ones = jnp.ones_like(vt)
            v_ext = (jnp.where(lane_v < HEAD_DIM, vt, ones), jnp.where(lane_v < HEAD_DIM, ones, vt))
            for hd in range(2):
                h = 2 * p + hd
                accs[h] = accs[h] + _dot(probs[h][c], v_ext[hd])
    outs = [a / pltpu.roll(a, HEAD_DIM, axis=1) for a in accs]
    o_ref[0] = jnp.concatenate([jnp.where(lo, outs[2 * p], outs[2 * p + 1]) for p in range(n_pairs)],
                               axis=1).astype(D_OUT)


def _na(qc, kct, vc, bias, *, layer, n_ctx_tiles):
    nb, tt, c_w = qc.shape
    nt = tt // TILE
    n_lat = nt - n_ctx_tiles

    def bias_idx(b, i):
        il = i - n_ctx_tiles
        cls = jnp.where(il < 0, 3, jnp.where(il == 0, 0, jnp.where(il == n_lat - 1, 2, 1)))
        return (layer, 0, cls, 0, 0)

    return pl.pallas_call(
        functools.partial(_na_kernel, n_ctx_tiles=n_ctx_tiles, n_key_tiles=nt),
        grid=(nb, nt),
        in_specs=[
            pl.BlockSpec((1, TILE, c_w), lambda b, i: (b, i, 0)),
            pl.BlockSpec((1, nt, c_w, TILE), lambda b, i: (b, 0, 0, 0)),
            pl.BlockSpec((1, tt, c_w), lambda b, i: (b, 0, 0)),
            pl.BlockSpec((1, bias.shape[1], 1, TILE, NA_WIN_TILES * TILE), bias_idx),
        ],
        out_specs=pl.BlockSpec((1, TILE, c_w), lambda b, i: (b, i, 0)),
        out_shape=jax.ShapeDtypeStruct((nb, tt, c_w), D_OUT),
        compiler_params=_cparams(("parallel", "parallel")),
        name="nbr_attn",
    )(qc, kct, vc, bias)


def _na_bias_kernel(u_ref, o_ref, *, plan):
    neg = jnp.full((GRID_W, GRID_W), NEG, F32)
    for cls, rows_plan in enumerate(plan):
        for qr, row_plan in enumerate(rows_plan):
            blocks = [neg if dr is None else u_ref[0, 0, dr] for dr in row_plan]
            o_ref[0, 0, cls, GRID_W * qr:GRID_W * (qr + 1), :] = jnp.concatenate(blocks, axis=1)
    o_ref[0, 0, len(plan)] = jnp.full(o_ref.shape[3:], NEG, F32)


def _na_bias_table(rel_bias, seq):
    rows = seq // GRID_W
    rpt = TILE // GRID_W
    wrows = NA_WIN_TILES * rpt
    assert rows >= wrows and rows >= WIN_ROWS
    n_layers, n_heads = rel_bias.shape[:2]
    n_dr, n_dc = 2 * WIN_ROWS - 1, 2 * WIN_COLS - 1
    qcol, kcol = np.arange(GRID_W)[:, None], np.arange(GRID_W)[None, :]
    kc0 = np.clip(qcol - WIN_COLS // 2, 0, GRID_W - WIN_COLS)
    col_ok = (kcol >= kc0) & (kcol < kc0 + WIN_COLS)
    dcol = kcol - qcol + (WIN_COLS - 1)
    sel_c = ((dcol[None] == np.arange(n_dc)[:, None, None]) & col_ok[None]).astype(np.float32)
    blocks = jnp.einsum("lhdc,cqk->lhdqk", rel_bias.astype(F32) * math.log2(math.e), sel_c,
                        precision=lax.Precision.HIGHEST)
    blocks = jnp.where(jnp.asarray(col_ok), blocks, NEG)
    n_lat = rows // rpt
    plan = []
    for il in (0, min(1, n_lat - 1), n_lat - 1):
        ws = int(np.clip(il - 1, 0, n_lat - NA_WIN_TILES)) * rpt
        rows_plan = []
        for qr in range(rpt):
            r_abs = il * rpt + qr
            kr0 = int(np.clip(r_abs - WIN_ROWS // 2, 0, rows - WIN_ROWS))
            rows_plan.append(tuple((ws + kr) - r_abs + (WIN_ROWS - 1) if kr0 <= ws + kr < kr0 + WIN_ROWS else None
                                   for kr in range(wrows)))
        plan.append(tuple(rows_plan))
    n_cls = len(plan) + 1
    return pl.pallas_call(
        functools.partial(_na_bias_kernel, plan=tuple(plan)),
        grid=(n_layers, n_heads),
        in_specs=[pl.BlockSpec((1, 1, n_dr, GRID_W, GRID_W), lambda l, h: (l, h, 0, 0, 0))],
        out_specs=pl.BlockSpec((1, 1, n_cls, TILE, NA_WIN_TILES * TILE), lambda l, h: (l, h, 0, 0, 0)),
        out_shape=jax.ShapeDtypeStruct((n_layers, n_heads, n_cls, TILE, NA_WIN_TILES * TILE), F32),
        compiler_params=_cparams(("parallel", "parallel")),
        name="nbr_bias_table",
    )(blocks)


def _gdn_prep_kernel(x_ref, prev_ref, next_ref, ab_ref, cw_ref, alog_ref, dtb_ref,
                     q_ref, k_ref, v_ref, sc_ref, sct_ref, *, n_ctx_tiles, n_tiles, n_heads):
    i = pl.program_id(1)
    first = jnp.logical_or(i == 0, i == n_ctx_tiles)
    last = jnp.logical_or(i == n_ctx_tiles - 1, i == n_tiles - 1)
    prev = jnp.where(first, 0.0, prev_ref[0, 0])
    nxt = jnp.where(last, 0.0, next_ref[0, 0])
    ext = jnp.concatenate([prev, x_ref[0], nxt], axis=0)
    cw = cw_ref[...]
    y = None
    for j in range(CONV_W):
        o = SUBLANES - CONV_W // 2 + j
        term = ext[o:o + TILE] * cw[j:j + 1]
        y = term if y is None else y + term
    y = _silu(y)
    b_w = n_heads * HEAD_DIM
    lo = lax.broadcasted_iota(jnp.int32, (TILE, LANES), 1) < HEAD_DIM

    def l2norm_heads(z):
        parts = []
        for p in range(z.shape[1] // LANES):
            zp = z[:, LANES * p:LANES * (p + 1)]
            sq = zp * zp
            s_lo = jnp.sum(jnp.where(lo, sq, 0.0), axis=-1, keepdims=True)
            s_hi = jnp.sum(jnp.where(lo, 0.0, sq), axis=-1, keepdims=True)
            parts.append(zp * lax.rsqrt(jnp.where(lo, s_lo, s_hi) + EPS))
        return jnp.concatenate(parts, axis=1)

    q_ref[0] = l2norm_heads(y[:, :b_w]) * (HEAD_DIM ** -0.5)
    k_ref[0] = l2norm_heads(y[:, b_w:2 * b_w])
    v_ref[0] = y[:, 2 * b_w:]

    ab = ab_ref[0]
    lane = lax.broadcasted_iota(jnp.int32, ab.shape, 1)
    z = ab + dtb_ref[...]
    softplus = jnp.maximum(z, 0.0) + jnp.log(1.0 + jnp.exp(-jnp.abs(z)))
    g = jnp.where(lane < 2 * n_heads, -jnp.exp(alog_ref[...]) * softplus, 0.0)
    beta = 1.0 / (1.0 + jnp.exp(-ab))
    r = lax.broadcasted_iota(jnp.int32, (TILE, TILE), 0)
    c = lax.broadcasted_iota(jnp.int32, (TILE, TILE), 1)
    same = (r // GDN_CHUNK) == (c // GDN_CHUNK)
    tri_f = jnp.where(jnp.logical_and(same, c <= r), 1.0, 0.0).astype(BF16)
    tri_b = jnp.where(jnp.logical_and(same, c >= r), 1.0, 0.0).astype(BF16)
    gc = jnp.where(lane < n_heads, _dot_sel_left(tri_f, g), _dot_sel_left(tri_b, g))
    rr = lax.broadcasted_iota(jnp.int32, (LANES, LANES), 0)
    cc = lax.broadcasted_iota(jnp.int32, (LANES, LANES), 1)
    hh, jj = cc // SUBLANES, cc % SUBLANES
    valid = hh < n_heads
    sel_gb = jnp.where(jnp.logical_and(valid, jnp.logical_and(jj < 4, rr == jj * n_heads + hh)), 1.0, 0.0).astype(BF16)
    sel_gc = jnp.where(jnp.logical_and(valid, jnp.logical_and(jnp.logical_and(jj >= 4, jj < 6),
                                                              rr == (jj - 4) * n_heads + hh)), 1.0, 0.0).astype(BF16)
    gb = jnp.where(lane < 2 * n_heads, g, beta)
    packed = _dot_sel_right(gb, sel_gb) + _dot_sel_right(gc, sel_gc)
    packed_t = packed.T
    for h in range(n_heads):
        sc_ref[0, h] = packed[:, SUBLANES * h:SUBLANES * (h + 1)]
        sct_ref[0, h] = packed_t[SUBLANES * h:SUBLANES * (h + 1), :]


def _gdn_prep(qkvb, ab, conv_w, alog_v, dtb_v, *, n_ctx_tiles, n_heads):
    nb, tt, w3 = qkvb.shape
    nt = tt // TILE
    rows8 = TILE // SUBLANES
    x8 = qkvb.reshape(nb, tt // SUBLANES, SUBLANES, w3)
    hd = lambda w: pl.BlockSpec((1, n_heads, TILE, w), lambda b, i: (b, 0, i, 0))
    tok = pl.BlockSpec((1, TILE, w3 // 3), lambda b, i: (b, i, 0))
    return pl.pallas_call(
        functools.partial(_gdn_prep_kernel, n_ctx_tiles=n_ctx_tiles, n_tiles=nt, n_heads=n_heads),
        grid=(nb, nt),
        in_specs=[
            pl.BlockSpec((1, TILE, w3), lambda b, i: (b, i, 0)),
            pl.BlockSpec((1, 1, SUBLANES, w3), lambda b, i: (b, jnp.maximum(i * rows8 - 1, 0), 0, 0)),
            pl.BlockSpec((1, 1, SUBLANES, w3), lambda b, i: (b, jnp.minimum((i + 1) * rows8, nt * rows8 - 1), 0, 0)),
            pl.BlockSpec((1, TILE, LANES), lambda b, i: (b, i, 0)),
            pl.BlockSpec((CONV_W, w3), lambda b, i: (0, 0)),
            pl.BlockSpec((1, LANES), lambda b, i: (0, 0)),
            pl.BlockSpec((1, LANES), lambda b, i: (0, 0)),
        ],
        out_specs=(tok, tok, tok, hd(SUBLANES),
                   pl.BlockSpec((1, n_heads, SUBLANES, TILE), lambda b, i: (b, 0, 0, i))),
        out_shape=(
            jax.ShapeDtypeStruct((nb, tt, w3 // 3), F32),
            jax.ShapeDtypeStruct((nb, tt, w3 // 3), F32),
            jax.ShapeDtypeStruct((nb, tt, w3 // 3), F32),
            jax.ShapeDtypeStruct((nb, n_heads, tt, SUBLANES), F32),
            jax.ShapeDtypeStruct((nb, n_heads, SUBLANES, tt), F32),
        ),
        compiler_params=_cparams(("parallel", "parallel")),
        name="gdn_prep",
    )(qkvb, x8, x8, ab, conv_w, alog_v, dtb_v)


def _bmm(a, b):
    return jnp.einsum("nik,nkj->nij", a, b, preferred_element_type=F32)


def _bmm_nt(a, b):
    return jnp.einsum("nik,njk->nij", a, b, preferred_element_type=F32)


def _pair_blockdiag(x, lo):
    zero = jnp.zeros_like(x)
    return jnp.concatenate([jnp.where(lo, x, zero), jnp.where(lo, zero, x)], axis=1)


def _pmm(p, x, lo, dtype):
    return _bmm(p.astype(dtype), _pair_blockdiag(x.astype(dtype), lo))


def _tri_inverse(lmat, ii, jj, lo):
    blk = TRI_BASE
    same = (ii // blk) == (jj // blk)
    dmat = jnp.where(same, lmat, 0.0)
    x = jnp.where(ii == jj, 1.0, 0.0) - dmat
    p = _pmm(dmat, dmat, lo, D_GI)
    n_sq = int(math.log2(blk)) - 1
    for step in range(n_sq):
        x = x + _pmm(p, x, lo, D_GI)
        if step < n_sq - 1:
            p = _pmm(p, p, lo, D_GI)
    while blk < GDN_CHUNK:
        wider = (ii // (2 * blk)) == (jj // (2 * blk))
        off = jnp.where(jnp.logical_and(wider, jnp.logical_not(same)), lmat, 0.0)
        x = x - _pmm(x, _pmm(off, x, lo, D_GI), lo, D_GI)
        same = wider
        blk *= 2
    return x


def _gdn_local_kernel(q_ref, k_ref, v_ref, sc_ref, sct_ref,
                      u_ref, w_ref, qh_ref, aqk_ref, ktt_ref, egl_ref):
    c = GDN_CHUNK
    cpt = TILE // c
    n_pairs = q_ref.shape[2] // LANES
    n = n_pairs * 2 * cpt
    lo = lax.broadcasted_iota(jnp.int32, (1, 1, LANES), 2) < HEAD_DIM

    def both(x):
        parts = []
        for p in range(n_pairs):
            x3 = x[:, LANES * p:LANES * (p + 1)].reshape(cpt, c, LANES)
            parts += [x3, x3]
        return jnp.concatenate(parts, axis=0)

    sc = sc_ref[0]

    def per_dir(col):
        parts = []
        for p in range(n_pairs):
            a0 = sc[2 * p].reshape(cpt, c, SUBLANES)
            a1 = sc[2 * p + 1].reshape(cpt, c, SUBLANES)
            for d in range(2):
                parts.append(jnp.where(lo, a0[:, :, col + d:col + d + 1], a1[:, :, col + d:col + d + 1]))
        return jnp.concatenate(parts, axis=0)

    q, k, v = both(q_ref[0]), both(k_ref[0]), both(v_ref[0])
    beta = per_dir(2)
    gcc = per_dir(4)
    sct = sct_ref[0]
    gcr = jnp.stack([jnp.concatenate([sct[2 * p, 4 + d:5 + d, ch * c:(ch + 1) * c],
                                      sct[2 * p + 1, 4 + d:5 + d, ch * c:(ch + 1) * c]], axis=1)
                     for p in range(n_pairs) for d in range(2) for ch in range(cpt)], axis=0)

    ii = lax.broadcasted_iota(jnp.int32, (n, c, LANES), 1)
    jj = lax.broadcasted_iota(jnp.int32, (n, c, LANES), 2) % c
    fwd = (lax.broadcasted_iota(jnp.int32, (n, c, LANES), 0) // cpt) % 2 == 0
    incl = jnp.logical_or(jnp.logical_and(fwd, ii >= jj), jnp.logical_and(jnp.logical_not(fwd), ii <= jj))
    strict = jnp.logical_and(incl, ii != jj)
    decay = jnp.where(incl, jnp.exp(jnp.where(incl, gcc - gcr, 0.0)), 0.0)
    kb = k * beta
    a = _bmm_nt(jnp.concatenate([kb, q], axis=1).astype(D_GA), _pair_blockdiag(k.astype(D_GA), lo))
    lmat = jnp.where(strict, a[:, :c] * decay, 0.0)
    aqk = a[:, c:] * decay
    tinv = _tri_inverse(lmat, ii, jj, lo)
    egc = jnp.exp(gcc)
    u = _pmm(tinv, v * beta, lo, D_GT)
    w = _pmm(tinv, kb * egc, lo, D_GT)
    fwd1 = (lax.broadcasted_iota(jnp.int32, (n, 1, 1), 0) // cpt) % 2 == 0
    glast = jnp.where(fwd1, gcc[:, c - 1:c, :], gcc[:, 0:1, :])
    ktail = k * jnp.exp(glast - gcc)
    ktt = jnp.concatenate([jnp.swapaxes(ktail[:, :, :HEAD_DIM], 1, 2), jnp.swapaxes(ktail[:, :, HEAD_DIM:], 1, 2)],
                          axis=2)
    split = lambda z: z.reshape((n_pairs, 2, cpt) + z.shape[1:])
    u_ref[0] = split(u)
    w_ref[0] = split(w.astype(D_GDN))
    qh_ref[0] = split((q * egc).astype(D_GDN))
    aqk_ref[0] = split(aqk.astype(D_GDN))
    ktt_ref[0] = split(ktt.astype(D_GDN))
    egl_ref[0] = split(jnp.exp(glast))


def _gdn_local(qn, kn, vv, sc, sct):
    nb, tt, b_w = qn.shape
    nh = sc.shape[1]
    n_pairs = b_w // LANES
    nt = tt // TILE
    nc = tt // GDN_CHUNK
    cpt = TILE // GDN_CHUNK
    tok = pl.BlockSpec((1, TILE, b_w), lambda b, i: (b, i, 0))
    chunked = lambda r: pl.BlockSpec((1, n_pairs, 2, cpt, r, LANES), lambda b, i: (b, 0, 0, i, 0, 0))
    shape = lambda r, dt: jax.ShapeDtypeStruct((nb, n_pairs, 2, nc, r, LANES), dt)
    return pl.pallas_call(
        _gdn_local_kernel,
        grid=(nb, nt),
        in_specs=[tok, tok, tok,
                  pl.BlockSpec((1, nh, TILE, SUBLANES), lambda b, i: (b, 0, i, 0)),
                  pl.BlockSpec((1, nh, SUBLANES, TILE), lambda b, i: (b, 0, 0, i))],
        out_specs=(chunked(GDN_CHUNK), chunked(GDN_CHUNK), chunked(GDN_CHUNK),
                   chunked(GDN_CHUNK), chunked(HEAD_DIM), chunked(1)),
        out_shape=(
            shape(GDN_CHUNK, F32),
            shape(GDN_CHUNK, D_GDN),
            shape(GDN_CHUNK, D_GDN),
            shape(GDN_CHUNK, D_GDN),
            shape(HEAD_DIM, D_GDN),
            shape(1, F32),
        ),
        compiler_params=_cparams(("parallel", "parallel")),
        name="gdn_local",
    )(qn, kn, vv, sc, sct)


def _gdn_scan_kernel(uf, ub, wf, wb, qf, qb, af, ab, kf, kb, ef, eb, of_ref, ob_ref, s_ref, *, cpt):
    @pl.when(pl.program_id(0) == 0)
    def _():
        s_ref[...] = jnp.zeros_like(s_ref)

    nb, n_pairs = uf.shape[:2]
    n = nb * n_pairs
    lo = lax.broadcasted_iota(jnp.int32, (1, 1, LANES), 2) < HEAD_DIM
    s = s_ref[...]
    for t in range(cpt):
        tb = cpt - 1 - t
        flat = lambda z: z.reshape((n,) + z.shape[2:])
        pair = lambda f, b: jnp.concatenate([flat(f[:, :, 0, t]), flat(b[:, :, 0, tb])], axis=0)
        sd = _pair_blockdiag(s.astype(D_GS), lo)
        v_new = pair(uf, ub) - _bmm(pair(wf, wb), sd)
        vd = _pair_blockdiag(v_new.astype(D_GS), lo)
        o = _bmm(pair(qf, qb), sd) + _bmm(pair(af, ab), vd)
        s = s * pair(ef, eb) + _bmm(pair(kf, kb), vd)
        of_ref[:, :, t] = o[:n].reshape((nb, n_pairs) + o.shape[1:])
        ob_ref[:, :, tb] = o[n:].reshape((nb, n_pairs) + o.shape[1:])
    s_ref[...] = s


def _gdn_scan(u, w, qh, aqk, ktt, egl, *, n_ctx_tiles):
    nb, n_pairs, _, nc, c, _ = u.shape
    cpt = TILE // c
    nt = nc // cpt

    def bwd_tile(j):
        return jnp.where(j < n_ctx_tiles, n_ctx_tiles - 1 - j, nt - 1 + n_ctx_tiles - j)

    def specs(a):
        blk = (nb, n_pairs, 1, cpt) + a.shape[4:]
        return [pl.BlockSpec(blk, lambda j: (0, 0, 0, j, 0, 0)),
                pl.BlockSpec(blk, lambda j: (0, 0, 1, bwd_tile(j), 0, 0))]

    args, in_specs = [], []
    for a in (u, w, qh, aqk, ktt, egl):
        args += [a, a]
        in_specs += specs(a)
    oshape = jax.ShapeDtypeStruct((nb, n_pairs, nc, c, LANES), F32)
    return pl.pallas_call(
        functools.partial(_gdn_scan_kernel, cpt=cpt),
        grid=(nt,),
        in_specs=in_specs,
        out_specs=(pl.BlockSpec((nb, n_pairs, cpt, c, LANES), lambda j: (0, 0, j, 0, 0)),
                   pl.BlockSpec((nb, n_pairs, cpt, c, LANES), lambda j: (0, 0, bwd_tile(j), 0, 0))),
        out_shape=(oshape, oshape),
        scratch_shapes=[pltpu.VMEM((2 * nb * n_pairs, HEAD_DIM, LANES), F32)],
        compiler_params=_cparams(("arbitrary",)),
        name="gdn_scan",
    )(*args)


def _mix_ffn_kernel(x_ref, mod_ref, ya_ref, of_ref, ob_ref, gate_ref, gnw_ref, yc_ref, w_ref,
                    nw_ref, wi_ref, wo_ref, fnw_ref, o_ref, *, d_ff, final):
    mod = mod_ref[0]
    lo = lax.broadcasted_iota(jnp.int32, (TILE, LANES), 1) < HEAD_DIM
    pairs = []
    for p in range(of_ref.shape[1]):
        o = of_ref[0, p] + ob_ref[0, p]
        sq = o * o
        ms_lo = jnp.sum(jnp.where(lo, sq, 0.0), axis=-1, keepdims=True) * (1.0 / HEAD_DIM)
        ms_hi = jnp.sum(jnp.where(lo, 0.0, sq), axis=-1, keepdims=True) * (1.0 / HEAD_DIM)
        pairs.append(o * lax.rsqrt(jnp.where(lo, ms_lo, ms_hi) + EPS) * gnw_ref[...])
    yb = (jnp.concatenate(pairs, axis=1) * _silu(gate_ref[0])).astype(D_OUT)
    y = jnp.concatenate([ya_ref[0], yb, yc_ref[0]], axis=1)
    x = x_ref[0] + mod[2:3] * _dot(y, w_ref[0])

    h = _norm_mod(x, nw_ref[...], mod[3:4], mod[4:5]).astype(D_FFN)
    gu = _dot(h, wi_ref[0])
    a = (_silu(gu[:, :d_ff]) * gu[:, d_ff:]).astype(D_FFN)
    out = x + mod[5:6] * _dot(a, wo_ref[0])
    if final:
        out = out * lax.rsqrt(jnp.mean(out * out, axis=-1, keepdims=True) + EPS) * fnw_ref[...]
    o_ref[0] = out


def _mix_ffn(xs, mods, ya, scan_f, scan_b, gate, gnw2, yc, w, nw, wi, wo, fnw,
             *, layer, tile0, final, n_batch, n_ctx_tiles):
    nb, tt, d = xs.shape
    nt = tt // TILE - tile0
    a_w, b_w, c_w = ya.shape[2], gate.shape[2], yc.shape[2]
    n_pairs = scan_f.shape[1]
    d_ff = wo.shape[1]

    def mod_idx(b, i):
        return (jnp.where(i + tile0 < n_ctx_tiles, n_batch, b), 0, 0)

    tok = lambda wd: pl.BlockSpec((1, TILE, wd), lambda b, i: (b, i + tile0, 0))
    pairs = pl.BlockSpec((1, n_pairs, TILE, LANES), lambda b, i: (b, 0, i + tile0, 0))
    vec = lambda wd: pl.BlockSpec((1, wd), lambda b, i: (0, 0))
    weight = lambda arr: pl.BlockSpec((1,) + arr.shape[1:], lambda b, i: (layer, 0, 0), pipeline_mode=pl.Buffered(1))
    return pl.pallas_call(
        functools.partial(_mix_ffn_kernel, d_ff=d_ff, final=final),
        grid=(nb, nt),
        in_specs=[tok(d), pl.BlockSpec((1, 6, d), mod_idx), tok(a_w), pairs, pairs, tok(b_w), vec(LANES), tok(c_w),
                  weight(w), vec(d), weight(wi), weight(wo), vec(d)],
        out_specs=pl.BlockSpec((1, TILE, d), lambda b, i: (b, i, 0)),
        out_shape=jax.ShapeDtypeStruct((nb, nt * TILE, d), F32),
        compiler_params=_cparams(("parallel", "parallel")),
        name="mix_ffn",
    )(xs, mods, ya, scan_f, scan_b, gate, gnw2, yc, w, nw, wi, wo, fnw)


def _rope_tables(seq, ctx_len, n_rep):
    half = A_QK // 2
    inv_freq = 1.0 / (ROPE_THETA ** (jnp.arange(0, half, 2, dtype=F32) / half))
    t = jnp.arange(seq, dtype=jnp.int32)
    ang_r = (t // GRID_W).astype(F32)[:, None] * inv_freq
    ang_c = (t % GRID_W).astype(F32)[:, None] * inv_freq
    ang = jnp.concatenate([ang_r, ang_r, ang_c, ang_c], axis=-1)
    cos = jnp.concatenate([jnp.ones((ctx_len, A_QK), F32), jnp.cos(ang)], axis=0)
    sin = jnp.concatenate([jnp.zeros((ctx_len, A_QK), F32), jnp.sin(ang)], axis=0)
    return jnp.tile(cos, (1, n_rep)), jnp.tile(sin, (1, n_rep))


def kernel(x, c, ctx, c_ctx, w_mod, b_mod, norm1_w, norm2_w, w_in, w_out, lambda_q1, lambda_k1, lambda_q2,
           lambda_k2, diff_norm_w, conv_w, a_log, dt_bias, gdn_norm_w, na_bias, w_ffn_in, w_ffn_out,
           final_norm_w):
    nb, seq, d = x.shape
    ctx_len = ctx.shape[1]
    depth = w_mod.shape[0]
    b_heads = a_log.shape[-1]
    c_heads = na_bias.shape[1]
    b_w = b_heads * HEAD_DIM
    c_w = c_heads * HEAD_DIM
    a_w = w_out.shape[1] - b_w - c_w
    assert seq % TILE == 0 and ctx_len % TILE == 0 and seq % GRID_W == 0
    assert a_w % LANES == 0 and b_w % LANES == 0 and c_w % LANES == 0 and 4 * b_heads <= LANES
    n_ctx_tiles = ctx_len // TILE

    xs = jnp.concatenate([ctx, x], axis=1)

    n_rows = -(-(nb + 1) // SUBLANES) * SUBLANES
    cc = jnp.concatenate([c, c_ctx[None, :], jnp.zeros((n_rows - nb - 1, d), F32)], axis=0)
    mods_all = _modulation(cc, w_mod, b_mod).reshape(depth, n_rows, 6, d)

    cos, sin = _rope_tables(seq, ctx_len, a_w // A_QK)
    cost, sint = cos.T, sin.T

    sizes = (3 * a_w, 3 * b_w, b_w, 2 * b_heads, 2 * b_heads, 3 * c_w)
    offs = np.concatenate([[0], np.cumsum(sizes)])
    o_a, o_b, o_g, o_al, o_be, o_c = (int(v) for v in offs[:6])
    pad = LANES - 4 * b_heads

    wm_all = jnp.concatenate([
        w_in[:, :, o_a:o_a + a_w],
        w_in[:, :, o_a + 2 * a_w:o_a + 3 * a_w],
        w_in[:, :, o_b:o_b + 3 * b_w],
        w_in[:, :, o_g:o_g + b_w],
        w_in[:, :, o_c:o_c + c_w],
        w_in[:, :, o_c + 2 * c_w:o_c + 3 * c_w],
        w_in[:, :, o_al:o_al + 4 * b_heads],
        jnp.zeros((depth, d, pad), F32),
    ], axis=2).astype(D_IN)
    wk_all = jnp.swapaxes(jnp.concatenate([w_in[:, :, o_a + a_w:o_a + 2 * a_w],
                                           w_in[:, :, o_c + c_w:o_c + 2 * c_w]], axis=2), 1, 2).astype(D_IN)
    w_out_all = w_out.astype(D_OUT)
    w_ffn_in_all = w_ffn_in.astype(D_FFN)
    w_ffn_out_all = w_ffn_out.astype(D_FFN)
    bias_all = _na_bias_table(na_bias, seq)
    nw2_all = jnp.tile(diff_norm_w, (1, LANES // HEAD_DIM))
    gnw2_all = jnp.tile(gdn_norm_w, (1, LANES // HEAD_DIM))
    zpad = jnp.zeros((depth, LANES - 2 * b_heads), F32)
    alog_all = jnp.concatenate([a_log.reshape(depth, -1), zpad], axis=1)
    dtb_all = jnp.concatenate([dt_bias.reshape(depth, -1), zpad], axis=1)
    fnw = final_norm_w[None, :]

    for l in range(depth):
        last = l == depth - 1
        mods = mods_all[l]
        lam_init = 0.8 - 0.6 * math.exp(-0.3 * l)

        qa, kat, va, qkvb, gate, ab, qc, kct, vc = _inproj(
            xs, mods, norm1_w[l][None, :], wm_all, wk_all, cos, sin, cost, sint,
            layer=l, n_batch=nb, n_ctx_tiles=n_ctx_tiles, a_w=a_w, b_w=b_w, c_w=c_w)

        lam_vecs = jnp.stack([lambda_q1[l], lambda_k1[l], lambda_q2[l], lambda_k2[l]], axis=0).astype(F32)
        ya = _diffattn(lam_vecs, nw2_all[l][None, :], qa, kat, va, n_ctx_tiles=n_ctx_tiles, lam_init=lam_init)

        yc = _na(qc, kct, vc, bias_all, layer=l, n_ctx_tiles=n_ctx_tiles)

        qn, kn, vv, sc, sct = _gdn_prep(qkvb, ab, conv_w[l], alog_all[l][None, :], dtb_all[l][None, :],
                                        n_ctx_tiles=n_ctx_tiles, n_heads=b_heads)
        u, w, qh, aqk, ktt, egl = _gdn_local(qn, kn, vv, sc, sct)
        scan_f, scan_b = _gdn_scan(u, w, qh, aqk, ktt, egl, n_ctx_tiles=n_ctx_tiles)
        scan_f = scan_f.reshape(nb, b_w // LANES, ctx_len + seq, LANES)
        scan_b = scan_b.reshape(nb, b_w // LANES, ctx_len + seq, LANES)

        tile0 = n_ctx_tiles if last else 0
        xs = _mix_ffn(xs, mods, ya, scan_f, scan_b, gate, gnw2_all[l][None, :], yc, w_out_all,
                      norm2_w[l][None, :], w_ffn_in_all, w_ffn_out_all, fnw,
                      layer=l, tile0=tile0, final=last, n_batch=nb, n_ctx_tiles=n_ctx_tiles)

    return xs
```

```python
import functools
import math

import jax
import jax.numpy as jnp
import numpy as np
from jax import lax
from jax.experimental import pallas as pl
from jax.experimental.pallas import tpu as pltpu

F32 = jnp.float32
BF16 = jnp.bfloat16
D_MOD = D_IN = D_OUT = D_FFN = D_ATT = D_NA = D_GDN = BF16
D_GA = D_GI = D_GT = D_GS = BF16

HEAD_DIM = 64
A_QK = HEAD_DIM // 2
GRID_W = 64
ROPE_THETA = 10000.0
CONV_W = 5
GDN_CHUNK = 64
WIN_ROWS = 8
WIN_COLS = 16
EPS = 1e-6

LANES = 128
SUBLANES = 8
TILE = 256
NA_WIN_TILES = 3
MOD_COLS = 1536
TRI_BASE = 16
NEG = -1e30
VMEM_LIMIT = 56 * 1024 * 1024


def _cparams(sem):
    return pltpu.CompilerParams(dimension_semantics=sem, vmem_limit_bytes=VMEM_LIMIT)


def _silu(x):
    return x * (1.0 / (1.0 + jnp.exp(-x)))


def _dot(a, b):
    return jnp.dot(a, b, preferred_element_type=F32)


def _dot_nt(a, b):
    return lax.dot_general(a, b, (((1,), (1,)), ((), ())), preferred_element_type=F32)


def _split3(x):
    hi = x.astype(BF16)
    r1 = x - hi.astype(F32)
    mid = r1.astype(BF16)
    lo = (r1 - mid.astype(F32)).astype(BF16)
    return hi, mid, lo


def _dot_sel_right(x, sel):
    hi, mid, lo = _split3(x)
    return _dot(hi, sel) + _dot(mid, sel) + _dot(lo, sel)


def _dot_sel_left(sel, x):
    hi, mid, lo = _split3(x)
    return _dot(sel, hi) + _dot(sel, mid) + _dot(sel, lo)


def _mod_kernel(c_ref, w_ref, b_ref, o_ref):
    s = _silu(c_ref[...])
    o_ref[0] = _dot(s.astype(D_MOD), w_ref[0].astype(D_MOD)) + b_ref[0]


def _modulation(cc, w_mod, b_mod):
    depth, d, n = w_mod.shape
    rows = cc.shape[0]
    tn = MOD_COLS
    assert n % tn == 0
    return pl.pallas_call(
        _mod_kernel,
        grid=(depth, n // tn),
        in_specs=[
            pl.BlockSpec((rows, d), lambda l, j: (0, 0)),
            pl.BlockSpec((1, d, tn), lambda l, j: (l, 0, j)),
            pl.BlockSpec((1, 1, tn), lambda l, j: (l, 0, j)),
        ],
        out_specs=pl.BlockSpec((1, rows, tn), lambda l, j: (l, 0, j)),
        out_shape=jax.ShapeDtypeStruct((depth, rows, n), F32),
        compiler_params=_cparams(("parallel", "parallel")),
        name="modulation",
    )(cc, w_mod, b_mod.reshape(depth, 1, n))


def _norm_mod(x, nw, shift, scale):
    ms = jnp.mean(x * x, axis=-1, keepdims=True)
    y = x * lax.rsqrt(ms + EPS) * nw
    return y * (1.0 + scale) + shift


def _rope_lanes(x, cos, sin):
    parts = []
    for j in range(x.shape[1] // LANES):
        xj = x[:, LANES * j:LANES * (j + 1)]
        nxt = pltpu.roll(xj, LANES - 8, axis=1)
        prv = pltpu.roll(xj, 8, axis=1)
        lane = lax.broadcasted_iota(jnp.int32, xj.shape, 1)
        parts.append(jnp.where((lane % 16) < 8, -nxt, prv))
    rot = jnp.concatenate(parts, axis=1)
    return x * cos + rot * sin


def _rope_rows(x, cos, sin):
    n = x.shape[0]
    nxt = pltpu.roll(x, n - 8, axis=0)
    prv = pltpu.roll(x, 8, axis=0)
    row = lax.broadcasted_iota(jnp.int32, x.shape, 0)
    rot = jnp.where((row % 16) < 8, -nxt, prv)
    return x * cos + rot * sin


def _inproj_kernel(x_ref, mod_ref, nw_ref, wm_ref, wk_ref, cos_ref, sin_ref, cost_ref, sint_ref,
                   qa_ref, kat_ref, va_ref, qkvb_ref, gate_ref, ab_ref, qc_ref, kct_ref, vc_ref,
                   *, a_w, b_w, c_w):
    mod = mod_ref[0]
    h = _norm_mod(x_ref[0], nw_ref[...], mod[0:1], mod[1:2]).astype(D_IN)
    acc = _dot(h, wm_ref[0])
    kt = _dot_nt(wk_ref[0], h)
    o = 0
    qa = _rope_lanes(acc[:, o:o + a_w], cos_ref[...], sin_ref[...])
    qa_ref[0] = (qa * (A_QK ** -0.5 * math.log2(math.e))).astype(D_ATT)
    o += a_w
    va_ref[0] = acc[:, o:o + a_w].astype(D_ATT)
    o += a_w
    qkvb_ref[0] = acc[:, o:o + 3 * b_w]
    o += 3 * b_w
    gate_ref[0] = acc[:, o:o + b_w]
    o += b_w
    qc_ref[0] = (acc[:, o:o + c_w] * (HEAD_DIM ** -0.5 * math.log2(math.e))).astype(D_NA)
    o += c_w
    vc_ref[0] = acc[:, o:o + c_w].astype(D_NA)
    o += c_w
    ab_ref[0] = acc[:, o:o + LANES]
    kat_ref[0, 0] = _rope_rows(kt[:a_w], cost_ref[...], sint_ref[...]).astype(D_ATT)
    kct_ref[0, 0] = kt[a_w:].astype(D_NA)


def _inproj(xs, mods, nw, wm, wk, cos, sin, cost, sint, *, layer, n_batch, n_ctx_tiles, a_w, b_w, c_w):
    nb, tt, d = xs.shape
    nt = tt // TILE
    ctx_row = n_batch

    def mod_idx(b, i):
        return (jnp.where(i < n_ctx_tiles, ctx_row, b), 0, 0)

    tok = lambda w: pl.BlockSpec((1, TILE, w), lambda b, i: (b, i, 0))
    tokt = lambda w: pl.BlockSpec((1, 1, w, TILE), lambda b, i: (b, i, 0, 0))
    out_shape = (
        jax.ShapeDtypeStruct((nb, tt, a_w), D_ATT),
        jax.ShapeDtypeStruct((nb, nt, a_w, TILE), D_ATT),
        jax.ShapeDtypeStruct((nb, tt, a_w), D_ATT),
        jax.ShapeDtypeStruct((nb, tt, 3 * b_w), F32),
        jax.ShapeDtypeStruct((nb, tt, b_w), F32),
        jax.ShapeDtypeStruct((nb, tt, LANES), F32),
        jax.ShapeDtypeStruct((nb, tt, c_w), D_NA),
        jax.ShapeDtypeStruct((nb, nt, c_w, TILE), D_NA),
        jax.ShapeDtypeStruct((nb, tt, c_w), D_NA),
    )
    return pl.pallas_call(
        functools.partial(_inproj_kernel, a_w=a_w, b_w=b_w, c_w=c_w),
        grid=(nb, nt),
        in_specs=[
            tok(d),
            pl.BlockSpec((1, 6, d), mod_idx),
            pl.BlockSpec((1, d), lambda b, i: (0, 0)),
            pl.BlockSpec((1,) + wm.shape[1:], lambda b, i: (layer, 0, 0)),
            pl.BlockSpec((1,) + wk.shape[1:], lambda b, i: (layer, 0, 0)),
            pl.BlockSpec((TILE, a_w), lambda b, i: (i, 0)),
            pl.BlockSpec((TILE, a_w), lambda b, i: (i, 0)),
            pl.BlockSpec((a_w, TILE), lambda b, i: (0, i)),
            pl.BlockSpec((a_w, TILE), lambda b, i: (0, i)),
        ],
        out_specs=(tok(a_w), tokt(a_w), tok(a_w), tok(3 * b_w), tok(b_w), tok(LANES),
                   tok(c_w), tokt(c_w), tok(c_w)),
        out_shape=out_shape,
        compiler_params=_cparams(("parallel", "parallel")),
        name="inproj",
    )(xs, mods, nw, wm, wk, cos, sin, cost, sint)


def _diffattn_kernel(lam_ref, nw_ref, q_ref, kt_ref, v_ref, o_ref, mx_ref, acc_ref,
                     *, n_ctx_tiles, n_key_tiles, lam_init):
    i = pl.program_id(2)
    lv = lam_ref[...]
    lam = (jnp.exp(jnp.sum(lv[0:1] * lv[1:2], axis=-1, keepdims=True))
           - jnp.exp(jnp.sum(lv[2:3] * lv[3:4], axis=-1, keepdims=True)) + lam_init)
    q = q_ref[0]
    tq = q.shape[0]
    lane = lax.broadcasted_iota(jnp.int32, (tq, LANES), 1)
    lo = lane < HEAD_DIM
    zero = jnp.zeros_like(q)
    qm = [jnp.where((lane // A_QK) == m, q, zero) for m in range(LANES // A_QK)]
    n_maps = len(qm)
    n_lat = n_key_tiles - n_ctx_tiles
    group = max(u for u in (16, 8, 4, 2, 1) if n_lat % u == 0)
    ctx_tiles = list(range(n_ctx_tiles))
    lat_tiles = lambda g: [n_ctx_tiles + g * group + u for u in range(group)]

    lane_v = lax.broadcasted_iota(jnp.int32, (TILE, LANES), 1)

    def visit(tiles, first):
        ktiles = [kt_ref[0, j] for j in tiles]
        v_ext = []
        for j in tiles:
            start = j * TILE if isinstance(j, int) else pl.multiple_of(j * TILE, TILE)
            vtile = v_ref[0, pl.ds(start, TILE), :]
            ones = jnp.ones_like(vtile)
            v_ext.append((jnp.where(lane_v < HEAD_DIM, vtile, ones), jnp.where(lane_v < HEAD_DIM, ones, vtile)))
        scores = [[_dot(qm[m], ktile) for ktile in ktiles] for m in range(n_maps)]
        m_new = []
        for m in range(n_maps):
            top = scores[m][0]
            for s in scores[m][1:]:
                top = jnp.maximum(top, s)
            top = jnp.max(jnp.maximum(top[:, :LANES], top[:, LANES:]), axis=-1, keepdims=True)
            m_new.append(top if first else jnp.maximum(mx_ref[m], top))
        probs = [[jnp.exp2(s - m_new[m]).astype(D_ATT) for s in scores[m]] for m in range(n_maps)]
        for m in range(n_maps):
            acc = None if first else acc_ref[m] * jnp.exp2(mx_ref[m] - m_new[m])
            for p, ve in zip(probs[m], v_ext):
                pv = _dot(p, ve[m // 2])
                acc = pv if acc is None else acc + pv
            mx_ref[m] = m_new[m]
            acc_ref[m] = acc

    @pl.when(i < n_ctx_tiles)
    def _():
        visit(ctx_tiles, True)

    @pl.when(i >= n_ctx_tiles)
    def _():
        visit(ctx_tiles + lat_tiles(0), True)
        lax.fori_loop(1, n_lat // group, lambda g, c: (visit(lat_tiles(g), False), c)[1], 0)
    attn = []
    for m in range(n_maps):
        a = acc_ref[m]
        attn.append(a / pltpu.roll(a, HEAD_DIM, axis=1))
    o = jnp.where(lo, attn[0] - lam * attn[1], attn[2] - lam * attn[3])
    sq = o * o
    ms_lo = jnp.sum(jnp.where(lo, sq, 0.0), axis=-1, keepdims=True) * (1.0 / HEAD_DIM)
    ms_hi = jnp.sum(jnp.where(lo, 0.0, sq), axis=-1, keepdims=True) * (1.0 / HEAD_DIM)
    y = o * lax.rsqrt(jnp.where(lo, ms_lo, ms_hi) + EPS) * nw_ref[...]
    o_ref[0] = (y * (1.0 - lam_init)).astype(D_OUT)


def _diffattn(lam_vecs, nw2, qa, kat, va, *, n_ctx_tiles, lam_init):
    nb, tt, a_w = qa.shape
    nt = tt // TILE
    return pl.pallas_call(
        functools.partial(_diffattn_kernel, n_ctx_tiles=n_ctx_tiles, n_key_tiles=nt, lam_init=lam_init),
        grid=(nb, a_w // LANES, nt),
        in_specs=[
            pl.BlockSpec(lam_vecs.shape, lambda b, p, i: (0, 0)),
            pl.BlockSpec((1, LANES), lambda b, p, i: (0, 0)),
            pl.BlockSpec((1, TILE, LANES), lambda b, p, i: (b, i, p)),
            pl.BlockSpec((1, nt, LANES, TILE), lambda b, p, i: (b, 0, p, 0)),
            pl.BlockSpec((1, tt, LANES), lambda b, p, i: (b, 0, p)),
        ],
        out_specs=pl.BlockSpec((1, TILE, LANES), lambda b, p, i: (b, i, p)),
        out_shape=jax.ShapeDtypeStruct((nb, tt, a_w), D_OUT),
        scratch_shapes=[pltpu.VMEM((LANES // A_QK, TILE, 1), F32),
                        pltpu.VMEM((LANES // A_QK, TILE, LANES), F32)],
        compiler_params=_cparams(("parallel", "parallel", "parallel")),
        name="diffattn",
    )(lam_vecs, nw2, qa, kat, va)


def _na_kernel(q_ref, kt_ref, v_ref, bias_ref, o_ref, *, n_ctx_tiles, n_key_tiles):
    i = pl.program_id(1)
    n_lat = n_key_tiles - n_ctx_tiles
    w0 = n_ctx_tiles + jnp.clip(i - n_ctx_tiles - 1, 0, n_lat - NA_WIN_TILES)
    tq = q_ref.shape[1]
    n_pairs = q_ref.shape[2] // LANES
    lane = lax.broadcasted_iota(jnp.int32, (tq, LANES), 1)
    lo = lane < HEAD_DIM
    heads = [(p, hd) for p in range(n_pairs) for hd in range(2)]
    qh = []
    for p in range(n_pairs):
        q = q_ref[0, :, LANES * p:LANES * (p + 1)]
        zero = jnp.zeros_like(q)
        qh += [jnp.where(lo, q, zero), jnp.where(lo, zero, q)]
    tiles = [w0 + c for c in range(NA_WIN_TILES)] + list(range(n_ctx_tiles))
    scores = []
    for h, (p, hd) in enumerate(heads):
        row = []
        for c, t in enumerate(tiles):
            s = _dot(qh[h], kt_ref[0, t, LANES * p:LANES * (p + 1), :])
            if c < NA_WIN_TILES:
                s = s + bias_ref[0, h, 0, :, TILE * c:TILE * (c + 1)]
            row.append(s)
        scores.append(row)
    tops = []
    for row in scores:
        top = row[0]
        for s in row[1:]:
            top = jnp.maximum(top, s)
        tops.append(jnp.max(jnp.maximum(top[:, :LANES], top[:, LANES:]), axis=-1, keepdims=True))
    probs = [[jnp.exp2(s - top).astype(D_NA) for s in row] for row, top in zip(scores, tops)]
    lane_v = lax.broadcasted_iota(jnp.int32, (TILE, LANES), 1)
    accs = [jnp.zeros((tq, LANES), F32) for _ in heads]
    for c, t in enumerate(tiles):
        start = t * TILE if isinstance(t, int) else pl.multiple_of(t * TILE, TILE)
        for p in range(n_pairs):
            vt = v_ref[0, pl.ds(start, TILE), LANES * p:LANES * (p + 1)]
            ones = jnp.ones_like(vt)
            v_ext = (jnp.where(lane_v < HEAD_DIM, vt, ones), jnp.where(lane_v < HEAD_DIM, ones, vt))
            for hd in range(2):
                h = 2 * p + hd
                accs[h] = accs[h] + _dot(probs[h][c], v_ext[hd])
    outs = [a / pltpu.roll(a, HEAD_DIM, axis=1) for a in accs]
    o_ref[0] = jnp.concatenate([jnp.where(lo, outs[2 * p], outs[2 * p + 1]) for p in range(n_pairs)],
                               axis=1).astype(D_OUT)


def _na(qc, kct, vc, bias, *, layer, n_ctx_tiles):
    nb, tt, c_w = qc.shape
    nt = tt // TILE
    n_lat = nt - n_ctx_tiles

    def bias_idx(b, i):
        il = i - n_ctx_tiles
        cls = jnp.where(il < 0, 3, jnp.where(il == 0, 0, jnp.where(il == n_lat - 1, 2, 1)))
        return (layer, 0, cls, 0, 0)

    return pl.pallas_call(
        functools.partial(_na_kernel, n_ctx_tiles=n_ctx_tiles, n_key_tiles=nt),
        grid=(nb, nt),
        in_specs=[
            pl.BlockSpec((1, TILE, c_w), lambda b, i: (b, i, 0)),
            pl.BlockSpec((1, nt, c_w, TILE), lambda b, i: (b, 0, 0, 0)),
            pl.BlockSpec((1, tt, c_w), lambda b, i: (b, 0, 0)),
            pl.BlockSpec((1, bias.shape[1], 1, TILE, NA_WIN_TILES * TILE), bias_idx),
        ],
        out_specs=pl.BlockSpec((1, TILE, c_w), lambda b, i: (b, i, 0)),
        out_shape=jax.ShapeDtypeStruct((nb, tt, c_w), D_OUT),
        compiler_params=_cparams(("parallel", "parallel")),
        name="nbr_attn",
    )(qc, kct, vc, bias)


def _na_bias_kernel(u_ref, o_ref, *, plan):
    neg = jnp.full((GRID_W, GRID_W), NEG, F32)
    for cls, rows_plan in enumerate(plan):
        for qr, row_plan in enumerate(rows_plan):
            blocks = [neg if dr is None else u_ref[0, 0, dr] for dr in row_plan]
            o_ref[0, 0, cls, GRID_W * qr:GRID_W * (qr + 1), :] = jnp.concatenate(blocks, axis=1)
    o_ref[0, 0, len(plan)] = jnp.full(o_ref.shape[3:], NEG, F32)


def _na_bias_table(rel_bias, seq):
    rows = seq // GRID_W
    rpt = TILE // GRID_W
    wrows = NA_WIN_TILES * rpt
    assert rows >= wrows and rows >= WIN_ROWS
    n_layers, n_heads = rel_bias.shape[:2]
    n_dr, n_dc = 2 * WIN_ROWS - 1, 2 * WIN_COLS - 1
    qcol, kcol = np.arange(GRID_W)[:, None], np.arange(GRID_W)[None, :]
    kc0 = np.clip(qcol - WIN_COLS // 2, 0, GRID_W - WIN_COLS)
    col_ok = (kcol >= kc0) & (kcol < kc0 + WIN_COLS)
    dcol = kcol - qcol + (WIN_COLS - 1)
    sel_c = ((dcol[None] == np.arange(n_dc)[:, None, None]) & col_ok[None]).astype(np.float32)
    blocks = jnp.einsum("lhdc,cqk->lhdqk", rel_bias.astype(F32) * math.log2(math.e), sel_c,
                        precision=lax.Precision.HIGHEST)
    blocks = jnp.where(jnp.asarray(col_ok), blocks, NEG)
    n_lat = rows // rpt
    plan = []
    for il in (0, min(1, n_lat - 1), n_lat - 1):
        ws = int(np.clip(il - 1, 0, n_lat - NA_WIN_TILES)) * rpt
        rows_plan = []
        for qr in range(rpt):
            r_abs = il * rpt + qr
            kr0 = int(np.clip(r_abs - WIN_ROWS // 2, 0, rows - WIN_ROWS))
            rows_plan.append(tuple((ws + kr) - r_abs + (WIN_ROWS - 1) if kr0 <= ws + kr < kr0 + WIN_ROWS else None
                                   for kr in range(wrows)))
        plan.append(tuple(rows_plan))
    n_cls = len(plan) + 1
    return pl.pallas_call(
        functools.partial(_na_bias_kernel, plan=tuple(plan)),
        grid=(n_layers, n_heads),
        in_specs=[pl.BlockSpec((1, 1, n_dr, GRID_W, GRID_W), lambda l, h: (l, h, 0, 0, 0))],
        out_specs=pl.BlockSpec((1, 1, n_cls, TILE, NA_WIN_TILES * TILE), lambda l, h: (l, h, 0, 0, 0)),
        out_shape=jax.ShapeDtypeStruct((n_layers, n_heads, n_cls, TILE, NA_WIN_TILES * TILE), F32),
        compiler_params=_cparams(("parallel", "parallel")),
        name="nbr_bias_table",
    )(blocks)


def _gdn_prep_kernel(x_ref, prev_ref, next_ref, ab_ref, cw_ref, alog_ref, dtb_ref,
                     q_ref, k_ref, v_ref, sc_ref, sct_ref, *, n_ctx_tiles, n_tiles, n_heads):
    i = pl.program_id(1)
    first = jnp.logical_or(i == 0, i == n_ctx_tiles)
    last = jnp.logical_or(i == n_ctx_tiles - 1, i == n_tiles - 1)
    prev = jnp.where(first, 0.0, prev_ref[0, 0])
    nxt = jnp.where(last, 0.0, next_ref[0, 0])
    ext = jnp.concatenate([prev, x_ref[0], nxt], axis=0)
    cw = cw_ref[...]
    y = None
    for j in range(CONV_W):
        o = SUBLANES - CONV_W // 2 + j
        term = ext[o:o + TILE] * cw[j:j + 1]
        y = term if y is None else y + term
    y = _silu(y)
    b_w = n_heads * HEAD_DIM
    lo = lax.broadcasted_iota(jnp.int32, (TILE, LANES), 1) < HEAD_DIM

    def l2norm_heads(z):
        parts = []
        for p in range(z.shape[1] // LANES):
            zp = z[:, LANES * p:LANES * (p + 1)]
            sq = zp * zp
            s_lo = jnp.sum(jnp.where(lo, sq, 0.0), axis=-1, keepdims=True)
            s_hi = jnp.sum(jnp.where(lo, 0.0, sq), axis=-1, keepdims=True)
            parts.append(zp * lax.rsqrt(jnp.where(lo, s_lo, s_hi) + EPS))
        return jnp.concatenate(parts, axis=1)

    q_ref[0] = l2norm_heads(y[:, :b_w]) * (HEAD_DIM ** -0.5)
    k_ref[0] = l2norm_heads(y[:, b_w:2 * b_w])
    v_ref[0] = y[:, 2 * b_w:]

    ab = ab_ref[0]
    lane = lax.broadcasted_iota(jnp.int32, ab.shape, 1)
    z = ab + dtb_ref[...]
    softplus = jnp.maximum(z, 0.0) + jnp.log(1.0 + jnp.exp(-jnp.abs(z)))
    g = jnp.where(lane < 2 * n_heads, -jnp.exp(alog_ref[...]) * softplus, 0.0)
    beta = 1.0 / (1.0 + jnp.exp(-ab))
    r = lax.broadcasted_iota(jnp.int32, (TILE, TILE), 0)
    c = lax.broadcasted_iota(jnp.int32, (TILE, TILE), 1)
    same = (r // GDN_CHUNK) == (c // GDN_CHUNK)
    tri_f = jnp.where(jnp.logical_and(same, c <= r), 1.0, 0.0).astype(BF16)
    tri_b = jnp.where(jnp.logical_and(same, c >= r), 1.0, 0.0).astype(BF16)
    gc = jnp.where(lane < n_heads, _dot_sel_left(tri_f, g), _dot_sel_left(tri_b, g))
    rr = lax.broadcasted_iota(jnp.int32, (LANES, LANES), 0)
    cc = lax.broadcasted_iota(jnp.int32, (LANES, LANES), 1)
    hh, jj = cc // SUBLANES, cc % SUBLANES
    valid = hh < n_heads
    sel_gb = jnp.where(jnp.logical_and(valid, jnp.logical_and(jj < 4, rr == jj * n_heads + hh)), 1.0, 0.0).astype(BF16)
    sel_gc = jnp.where(jnp.logical_and(valid, jnp.logical_and(jnp.logical_and(jj >= 4, jj < 6),
                                                              rr == (jj - 4) * n_heads + hh)), 1.0, 0.0).astype(BF16)
    gb = jnp.where(lane < 2 * n_heads, g, beta)
    packed = _dot_sel_right(gb, sel_gb) + _dot_sel_right(gc, sel_gc)
    sc_ref[0] = packed
    sct_ref[0] = packed.T


def _gdn_prep(qkvb, ab, conv_w, alog_v, dtb_v, *, n_ctx_tiles, n_heads):
    nb, tt, w3 = qkvb.shape
    nt = tt // TILE
    rows8 = TILE // SUBLANES
    x8 = qkvb.reshape(nb, tt // SUBLANES, SUBLANES, w3)
    tok = pl.BlockSpec((1, TILE, w3 // 3), lambda b, i: (b, i, 0))
    return pl.pallas_call(
        functools.partial(_gdn_prep_kernel, n_ctx_tiles=n_ctx_tiles, n_tiles=nt, n_heads=n_heads),
        grid=(nb, nt),
        in_specs=[
            pl.BlockSpec((1, TILE, w3), lambda b, i: (b, i, 0)),
            pl.BlockSpec((1, 1, SUBLANES, w3), lambda b, i: (b, jnp.maximum(i * rows8 - 1, 0), 0, 0)),
            pl.BlockSpec((1, 1, SUBLANES, w3), lambda b, i: (b, jnp.minimum((i + 1) * rows8, nt * rows8 - 1), 0, 0)),
            pl.BlockSpec((1, TILE, LANES), lambda b, i: (b, i, 0)),
            pl.BlockSpec((CONV_W, w3), lambda b, i: (0, 0)),
            pl.BlockSpec((1, LANES), lambda b, i: (0, 0)),
            pl.BlockSpec((1, LANES), lambda b, i: (0, 0)),
        ],
        out_specs=(tok, tok, tok, pl.BlockSpec((1, TILE, LANES), lambda b, i: (b, i, 0)),
                   pl.BlockSpec((1, LANES, TILE), lambda b, i: (b, 0, i))),
        out_shape=(
            jax.ShapeDtypeStruct((nb, tt, w3 // 3), F32),
            jax.ShapeDtypeStruct((nb, tt, w3 // 3), F32),
            jax.ShapeDtypeStruct((nb, tt, w3 // 3), F32),
            jax.ShapeDtypeStruct((nb, tt, LANES), F32),
            jax.ShapeDtypeStruct((nb, LANES, tt), F32),
        ),
        compiler_params=_cparams(("parallel", "parallel")),
        name="gdn_prep",
    )(qkvb, x8, x8, ab, conv_w, alog_v, dtb_v)


def _bmm(a, b):
    return jnp.einsum("nik,nkj->nij", a, b, preferred_element_type=F32)


def _bmm_nt(a, b):
    return jnp.einsum("nik,njk->nij", a, b, preferred_element_type=F32)


def _pair_blockdiag(x, lo):
    zero = jnp.zeros_like(x)
    return jnp.concatenate([jnp.where(lo, x, zero), jnp.where(lo, zero, x)], axis=1)


def _pmm(p, x, lo, dtype):
    return _bmm(p.astype(dtype), _pair_blockdiag(x.astype(dtype), lo))


def _tri_inverse(lmat, ii, jj, lo):
    blk = TRI_BASE
    same = (ii // blk) == (jj // blk)
    dmat = jnp.where(same, lmat, 0.0)
    x = jnp.where(ii == jj, 1.0, 0.0) - dmat
    p = _pmm(dmat, dmat, lo, D_GI)
    n_sq = int(math.log2(blk)) - 1
    for step in range(n_sq):
        x = x + _pmm(p, x, lo, D_GI)
        if step < n_sq - 1:
            p = _pmm(p, p, lo, D_GI)
    while blk < GDN_CHUNK:
        wider = (ii // (2 * blk)) == (jj // (2 * blk))
        off = jnp.where(jnp.logical_and(wider, jnp.logical_not(same)), lmat, 0.0)
        x = x - _pmm(x, _pmm(off, x, lo, D_GI), lo, D_GI)
        same = wider
        blk *= 2
    return x


def _gdn_local_kernel(q_ref, k_ref, v_ref, sc_ref, sct_ref,
                      u_ref, w_ref, qh_ref, aqk_ref, ktt_ref, egl_ref):
    c = GDN_CHUNK
    cpt = TILE // c
    n_pairs = q_ref.shape[2] // LANES
    n = n_pairs * 2 * cpt
    lo = lax.broadcasted_iota(jnp.int32, (1, 1, LANES), 2) < HEAD_DIM

    def both(x):
        parts = []
        for p in range(n_pairs):
            x3 = x[:, LANES * p:LANES * (p + 1)].reshape(cpt, c, LANES)
            parts += [x3, x3]
        return jnp.concatenate(parts, axis=0)

    sc = sc_ref[0].reshape(cpt, c, LANES)

    def per_dir(col):
        parts = []
        for p in range(n_pairs):
            l0, l1 = SUBLANES * 2 * p + col, SUBLANES * (2 * p + 1) + col
            for d in range(2):
                parts.append(jnp.where(lo, sc[:, :, l0 + d:l0 + d + 1], sc[:, :, l1 + d:l1 + d + 1]))
        return jnp.concatenate(parts, axis=0)

    q, k, v = both(q_ref[0]), both(k_ref[0]), both(v_ref[0])
    beta = per_dir(2)
    gcc = per_dir(4)
    sct = sct_ref[0]
    row = lambda h, d: SUBLANES * h + 4 + d
    gcr = jnp.stack([jnp.concatenate([sct[row(2 * p, d):row(2 * p, d) + 1, ch * c:(ch + 1) * c],
                                      sct[row(2 * p + 1, d):row(2 * p + 1, d) + 1, ch * c:(ch + 1) * c]], axis=1)
                     for p in range(n_pairs) for d in range(2) for ch in range(cpt)], axis=0)

    ii = lax.broadcasted_iota(jnp.int32, (n, c, LANES), 1)
    jj = lax.broadcasted_iota(jnp.int32, (n, c, LANES), 2) % c
    fwd = (lax.broadcasted_iota(jnp.int32, (n, c, LANES), 0) // cpt) % 2 == 0
    incl = jnp.logical_or(jnp.logical_and(fwd, ii >= jj), jnp.logical_and(jnp.logical_not(fwd), ii <= jj))
    strict = jnp.logical_and(incl, ii != jj)
    decay = jnp.where(incl, jnp.exp(jnp.where(incl, gcc - gcr, 0.0)), 0.0)
    kb = k * beta
    a = _bmm_nt(jnp.concatenate([kb, q], axis=1).astype(D_GA), _pair_blockdiag(k.astype(D_GA), lo))
    lmat = jnp.where(strict, a[:, :c] * decay, 0.0)
    aqk = a[:, c:] * decay
    tinv = _tri_inverse(lmat, ii, jj, lo)
    egc = jnp.exp(gcc)
    u = _pmm(tinv, v * beta, lo, D_GT)
    w = _pmm(tinv, kb * egc, lo, D_GT)
    fwd1 = (lax.broadcasted_iota(jnp.int32, (n, 1, 1), 0) // cpt) % 2 == 0
    glast = jnp.where(fwd1, gcc[:, c - 1:c, :], gcc[:, 0:1, :])
    ktail = k * jnp.exp(glast - gcc)
    ktt = jnp.concatenate([jnp.swapaxes(ktail[:, :, :HEAD_DIM], 1, 2), jnp.swapaxes(ktail[:, :, HEAD_DIM:], 1, 2)],
                          axis=2)
    split = lambda z: z.reshape((n_pairs, 2, cpt) + z.shape[1:])
    u_ref[0] = split(u)
    w_ref[0] = split(w.astype(D_GDN))
    qh_ref[0] = split((q * egc).astype(D_GDN))
    aqk_ref[0] = split(aqk.astype(D_GDN))
    ktt_ref[0] = split(ktt.astype(D_GDN))
    egl_ref[0] = split(jnp.exp(glast))


def _gdn_local(qn, kn, vv, sc, sct):
    nb, tt, b_w = qn.shape
    n_pairs = b_w // LANES
    nt = tt // TILE
    nc = tt // GDN_CHUNK
    cpt = TILE // GDN_CHUNK
    tok = pl.BlockSpec((1, TILE, b_w), lambda b, i: (b, i, 0))
    chunked = lambda r: pl.BlockSpec((1, n_pairs, 2, cpt, r, LANES), lambda b, i: (b, 0, 0, i, 0, 0))
    shape = lambda r, dt: jax.ShapeDtypeStruct((nb, n_pairs, 2, nc, r, LANES), dt)
    return pl.pallas_call(
        _gdn_local_kernel,
        grid=(nb, nt),
        in_specs=[tok, tok, tok,
                  pl.BlockSpec((1, TILE, LANES), lambda b, i: (b, i, 0)),
                  pl.BlockSpec((1, LANES, TILE), lambda b, i: (b, 0, i))],
        out_specs=(chunked(GDN_CHUNK), chunked(GDN_CHUNK), chunked(GDN_CHUNK),
                   chunked(GDN_CHUNK), chunked(HEAD_DIM), chunked(1)),
        out_shape=(
            shape(GDN_CHUNK, F32),
            shape(GDN_CHUNK, D_GDN),
            shape(GDN_CHUNK, D_GDN),
            shape(GDN_CHUNK, D_GDN),
            shape(HEAD_DIM, D_GDN),
            shape(1, F32),
        ),
        compiler_params=_cparams(("parallel", "parallel")),
        name="gdn_local",
    )(qn, kn, vv, sc, sct)


def _gdn_scan_kernel(uf, ub, wf, wb, qf, qb, af, ab, kf, kb, ef, eb, of_ref, ob_ref, s_ref, *, cpt):
    @pl.when(pl.program_id(0) == 0)
    def _():
        s_ref[...] = jnp.zeros_like(s_ref)

    nb, n_pairs = uf.shape[:2]
    n = nb * n_pairs
    lo = lax.broadcasted_iota(jnp.int32, (1, 1, LANES), 2) < HEAD_DIM
    s = s_ref[...]
    for t in range(cpt):
        tb = cpt - 1 - t
        flat = lambda z: z.reshape((n,) + z.shape[2:])
        pair = lambda f, b: jnp.concatenate([flat(f[:, :, 0, t]), flat(b[:, :, 0, tb])], axis=0)
        sd = _pair_blockdiag(s.astype(D_GS), lo)
        v_new = pair(uf, ub) - _bmm(pair(wf, wb), sd)
        vd = _pair_blockdiag(v_new.astype(D_GS), lo)
        o = _bmm(pair(qf, qb), sd) + _bmm(pair(af, ab), vd)
        s = s * pair(ef, eb) + _bmm(pair(kf, kb), vd)
        of_ref[:, :, t] = o[:n].reshape((nb, n_pairs) + o.shape[1:])
        ob_ref[:, :, tb] = o[n:].reshape((nb, n_pairs) + o.shape[1:])
    s_ref[...] = s


def _gdn_scan(u, w, qh, aqk, ktt, egl, *, n_ctx_tiles):
    nb, n_pairs, _, nc, c, _ = u.shape
    cpt = TILE // c
    nt = nc // cpt

    def bwd_tile(j):
        return jnp.where(j < n_ctx_tiles, n_ctx_tiles - 1 - j, nt - 1 + n_ctx_tiles - j)

    def specs(a):
        blk = (nb, n_pairs, 1, cpt) + a.shape[4:]
        return [pl.BlockSpec(blk, lambda j: (0, 0, 0, j, 0, 0)),
                pl.BlockSpec(blk, lambda j: (0, 0, 1, bwd_tile(j), 0, 0))]

    args, in_specs = [], []
    for a in (u, w, qh, aqk, ktt, egl):
        args += [a, a]
        in_specs += specs(a)
    oshape = jax.ShapeDtypeStruct((nb, n_pairs, nc, c, LANES), F32)
    return pl.pallas_call(
        functools.partial(_gdn_scan_kernel, cpt=cpt),
        grid=(nt,),
        in_specs=in_specs,
        out_specs=(pl.BlockSpec((nb, n_pairs, cpt, c, LANES), lambda j: (0, 0, j, 0, 0)),
                   pl.BlockSpec((nb, n_pairs, cpt, c, LANES), lambda j: (0, 0, bwd_tile(j), 0, 0))),
        out_shape=(oshape, oshape),
        scratch_shapes=[pltpu.VMEM((2 * nb * n_pairs, HEAD_DIM, LANES), F32)],
        compiler_params=_cparams(("arbitrary",)),
        name="gdn_scan",
    )(*args)


def _mix_ffn_kernel(x_ref, mod_ref, ya_ref, of_ref, ob_ref, gate_ref, gnw_ref, yc_ref, w_ref,
                    nw_ref, wi_ref, wo_ref, fnw_ref, o_ref, *, d_ff, final):
    mod = mod_ref[0]
    lo = lax.broadcasted_iota(jnp.int32, (TILE, LANES), 1) < HEAD_DIM
    pairs = []
    for p in range(of_ref.shape[1]):
        o = of_ref[0, p] + ob_ref[0, p]
        sq = o * o
        ms_lo = jnp.sum(jnp.where(lo, sq, 0.0), axis=-1, keepdims=True) * (1.0 / HEAD_DIM)
        ms_hi = jnp.sum(jnp.where(lo, 0.0, sq), axis=-1, keepdims=True) * (1.0 / HEAD_DIM)
        pairs.append(o * lax.rsqrt(jnp.where(lo, ms_lo, ms_hi) + EPS) * gnw_ref[...])
    yb = (jnp.concatenate(pairs, axis=1) * _silu(gate_ref[0])).astype(D_OUT)
    y = jnp.concatenate([ya_ref[0], yb, yc_ref[0]], axis=1)
    x = x_ref[0] + mod[2:3] * _dot(y, w_ref[0])

    h = _norm_mod(x, nw_ref[...], mod[3:4], mod[4:5]).astype(D_FFN)
    gu = _dot(h, wi_ref[0])
    a = (_silu(gu[:, :d_ff]) * gu[:, d_ff:]).astype(D_FFN)
    out = x + mod[5:6] * _dot(a, wo_ref[0])
    if final:
        out = out * lax.rsqrt(jnp.mean(out * out, axis=-1, keepdims=True) + EPS) * fnw_ref[...]
    o_ref[0] = out


def _mix_ffn(xs, mods, ya, scan_f, scan_b, gate, gnw2, yc, w, nw, wi, wo, fnw,
             *, layer, tile0, final, n_batch, n_ctx_tiles):
    nb, tt, d = xs.shape
    nt = tt // TILE - tile0
    a_w, b_w, c_w = ya.shape[2], gate.shape[2], yc.shape[2]
    n_pairs = scan_f.shape[1]
    d_ff = wo.shape[1]

    def mod_idx(b, i):
        return (jnp.where(i + tile0 < n_ctx_tiles, n_batch, b), 0, 0)

    tok = lambda wd: pl.BlockSpec((1, TILE, wd), lambda b, i: (b, i + tile0, 0))
    pairs = pl.BlockSpec((1, n_pairs, TILE, LANES), lambda b, i: (b, 0, i + tile0, 0))
    vec = lambda wd: pl.BlockSpec((1, wd), lambda b, i: (0, 0))
    weight = lambda arr: pl.BlockSpec((1,) + arr.shape[1:], lambda b, i: (layer, 0, 0), pipeline_mode=pl.Buffered(1))
    return pl.pallas_call(
        functools.partial(_mix_ffn_kernel, d_ff=d_ff, final=final),
        grid=(nb, nt),
        in_specs=[tok(d), pl.BlockSpec((1, 6, d), mod_idx), tok(a_w), pairs, pairs, tok(b_w), vec(LANES), tok(c_w),
                  weight(w), vec(d), weight(wi), weight(wo), vec(d)],
        out_specs=pl.BlockSpec((1, TILE, d), lambda b, i: (b, i, 0)),
        out_shape=jax.ShapeDtypeStruct((nb, nt * TILE, d), F32),
        compiler_params=_cparams(("parallel", "parallel")),
        name="mix_ffn",
    )(xs, mods, ya, scan_f, scan_b, gate, gnw2, yc, w, nw, wi, wo, fnw)


def _rope_tables(seq, ctx_len, n_rep):
    half = A_QK // 2
    inv_freq = 1.0 / (ROPE_THETA ** (jnp.arange(0, half, 2, dtype=F32) / half))
    t = jnp.arange(seq, dtype=jnp.int32)
    ang_r = (t // GRID_W).astype(F32)[:, None] * inv_freq
    ang_c = (t % GRID_W).astype(F32)[:, None] * inv_freq
    ang = jnp.concatenate([ang_r, ang_r, ang_c, ang_c], axis=-1)
    cos = jnp.concatenate([jnp.ones((ctx_len, A_QK), F32), jnp.cos(ang)], axis=0)
    sin = jnp.concatenate([jnp.zeros((ctx_len, A_QK), F32), jnp.sin(ang)], axis=0)
    return jnp.tile(cos, (1, n_rep)), jnp.tile(sin, (1, n_rep))


def kernel(x, c, ctx, c_ctx, w_mod, b_mod, norm1_w, norm2_w, w_in, w_out, lambda_q1, lambda_k1, lambda_q2,
           lambda_k2, diff_norm_w, conv_w, a_log, dt_bias, gdn_norm_w, na_bias, w_ffn_in, w_ffn_out,
           final_norm_w):
    nb, seq, d = x.shape
    ctx_len = ctx.shape[1]
    depth = w_mod.shape[0]
    b_heads = a_log.shape[-1]
    c_heads = na_bias.shape[1]
    b_w = b_heads * HEAD_DIM
    c_w = c_heads * HEAD_DIM
    a_w = w_out.shape[1] - b_w - c_w
    assert seq % TILE == 0 and ctx_len % TILE == 0 and seq % GRID_W == 0
    assert a_w % LANES == 0 and b_w % LANES == 0 and c_w % LANES == 0 and 4 * b_heads <= LANES
    n_ctx_tiles = ctx_len // TILE

    xs = jnp.concatenate([ctx, x], axis=1)

    n_rows = -(-(nb + 1) // SUBLANES) * SUBLANES
    cc = jnp.concatenate([c, c_ctx[None, :], jnp.zeros((n_rows - nb - 1, d), F32)], axis=0)
    mods_all = _modulation(cc, w_mod, b_mod).reshape(depth, n_rows, 6, d)

    cos, sin = _rope_tables(seq, ctx_len, a_w // A_QK)
    cost, sint = cos.T, sin.T

    sizes = (3 * a_w, 3 * b_w, b_w, 2 * b_heads, 2 * b_heads, 3 * c_w)
    offs = np.concatenate([[0], np.cumsum(sizes)])
    o_a, o_b, o_g, o_al, o_be, o_c = (int(v) for v in offs[:6])
    pad = LANES - 4 * b_heads

    wm_all = jnp.concatenate([
        w_in[:, :, o_a:o_a + a_w],
        w_in[:, :, o_a + 2 * a_w:o_a + 3 * a_w],
        w_in[:, :, o_b:o_b + 3 * b_w],
        w_in[:, :, o_g:o_g + b_w],
        w_in[:, :, o_c:o_c + c_w],
        w_in[:, :, o_c + 2 * c_w:o_c + 3 * c_w],
        w_in[:, :, o_al:o_al + 4 * b_heads],
        jnp.zeros((depth, d, pad), F32),
    ], axis=2).astype(D_IN)
    wk_all = jnp.swapaxes(jnp.concatenate([w_in[:, :, o_a + a_w:o_a + 2 * a_w],
                                           w_in[:, :, o_c + c_w:o_c + 2 * c_w]], axis=2), 1, 2).astype(D_IN)
    w_out_all = w_out.astype(D_OUT)
    w_ffn_in_all = w_ffn_in.astype(D_FFN)
    w_ffn_out_all = w_ffn_out.astype(D_FFN)
    bias_all = _na_bias_table(na_bias, seq)
    nw2_all = jnp.tile(diff_norm_w, (1, LANES // HEAD_DIM))
    gnw2_all = jnp.tile(gdn_norm_w, (1, LANES // HEAD_DIM))
    zpad = jnp.zeros((depth, LANES - 2 * b_heads), F32)
    alog_all = jnp.concatenate([a_log.reshape(depth, -1), zpad], axis=1)
    dtb_all = jnp.concatenate([dt_bias.reshape(depth, -1), zpad], axis=1)
    fnw = final_norm_w[None, :]

    for l in range(depth):
        last = l == depth - 1
        mods = mods_all[l]
        lam_init = 0.8 - 0.6 * math.exp(-0.3 * l)

        qa, kat, va, qkvb, gate, ab, qc, kct, vc = _inproj(
            xs, mods, norm1_w[l][None, :], wm_all, wk_all, cos, sin, cost, sint,
            layer=l, n_batch=nb, n_ctx_tiles=n_ctx_tiles, a_w=a_w, b_w=b_w, c_w=c_w)

        lam_vecs = jnp.stack([lambda_q1[l], lambda_k1[l], lambda_q2[l], lambda_k2[l]], axis=0).astype(F32)
        ya = _diffattn(lam_vecs, nw2_all[l][None, :], qa, kat, va, n_ctx_tiles=n_ctx_tiles, lam_init=lam_init)

        yc = _na(qc, kct, vc, bias_all, layer=l, n_ctx_tiles=n_ctx_tiles)

        qn, kn, vv, sc, sct = _gdn_prep(qkvb, ab, conv_w[l], alog_all[l][None, :], dtb_all[l][None, :],
                                        n_ctx_tiles=n_ctx_tiles, n_heads=b_heads)
        u, w, qh, aqk, ktt, egl = _gdn_local(qn, kn, vv, sc, sct)
        scan_f, scan_b = _gdn_scan(u, w, qh, aqk, ktt, egl, n_ctx_tiles=n_ctx_tiles)
        scan_f = scan_f.reshape(nb, b_w // LANES, ctx_len + seq, LANES)
        scan_b = scan_b.reshape(nb, b_w // LANES, ctx_len + seq, LANES)

        tile0 = n_ctx_tiles if last else 0
        xs = _mix_ffn(xs, mods, ya, scan_f, scan_b, gate, gnw2_all[l][None, :], yc, w_out_all,
                      norm2_w[l][None, :], w_ffn_in_all, w_ffn_out_all, fnw,
                      layer=l, tile0=tile0, final=last, n_batch=nb, n_ctx_tiles=n_ctx_tiles)

    return xs
```
